```python
import jax, jax.numpy as jnp
from jax import lax
import numpy as np

D_MODEL = 2048
BATCH = 8
SEQ = 2048
DEPTH = 1
DEC_BATCH = 16
DEC_SEQ = 32
PAST_LEN = 2048

CHUNK = 64
D_M = D_MODEL // 2
H_M = 4
DH_M = D_M // H_M
D_H = D_MODEL // 2
DH_H = 128
H_H = D_H // DH_H
CONV_W = 4
D_FF = 4 * D_MODEL
EPS = 1e-6

kernel_name = "hybrid_mlstm_hgrn2_streaming_step"


def _split_points():
    sizes = [2 * D_M, D_M, D_M, 2 * H_M, D_H, D_H, D_H, D_H, D_MODEL, D_MODEL]
    pts, acc = [], 0
    for s in sizes[:-1]:
        acc += s
        pts.append(acc)
    return pts, acc + sizes[-1]


def _rmsnorm(x, g):
    xf = x.astype(jnp.float32)
    y = xf * lax.rsqrt(jnp.mean(xf * xf, axis=-1, keepdims=True) + EPS)
    return (y * g.astype(jnp.float32)).astype(x.dtype)


def _causal_conv(x_pad, w, b):
    T = x_pad.shape[1] - CONV_W + 1
    out = b
    for i in range(CONV_W):
        out = out + x_pad[:, i:i + T] * w[i]
    return out


def _to_blocks(a, L):
    B, T = a.shape[0], a.shape[1]
    a = a.reshape((B, T // L, L) + a.shape[2:])
    perm = (1, 0, 3, 2) + tuple(range(4, a.ndim))
    return a.transpose(perm)


def _from_blocks(a):
    NC, B, H, L, d = a.shape
    return a.transpose(1, 0, 3, 2, 4).reshape(B, NC * L, H, d)


def _mlstm_chunkwise(q, k, v, ig, lf, C0, n0, m0):
    T = q.shape[1]
    L = min(CHUNK, T)
    causal = jnp.tril(jnp.ones((L, L), dtype=bool))

    def step(carry, inp):
        C, n, m = carry
        qc, kc, vc, igc, lfc = inp
        bcum = jnp.cumsum(lfc, axis=-1)
        dlog = bcum[..., :, None] - bcum[..., None, :] + igc[..., None, :]
        dlog = jnp.where(causal, dlog, -jnp.inf)
        m_inter = bcum + m[..., None]
        m_t = jnp.maximum(m_inter, jnp.max(dlog, axis=-1))
        s = jnp.einsum('bhld,bhsd->bhls', qc, kc) * jnp.exp(dlog - m_t[..., None])
        sc_inter = jnp.exp(m_inter - m_t)
        num = sc_inter[..., None] * jnp.einsum('bhvd,bhld->bhlv', C, qc) + jnp.einsum('bhls,bhsv->bhlv', s, vc)
        den = sc_inter * jnp.einsum('bhd,bhld->bhl', n, qc) + jnp.sum(s, axis=-1)
        h = num / jnp.maximum(jnp.abs(den), jnp.exp(-m_t))[..., None]
        m_new = m_t[..., -1]
        decay = jnp.exp(bcum[..., -1] + m - m_new)
        wgt = jnp.exp(bcum[..., -1:] - bcum + igc - m_new[..., None])
        C_new = decay[..., None, None] * C + jnp.einsum('bhs,bhsv,bhsd->bhvd', wgt, vc, kc)
        n_new = decay[..., None] * n + jnp.einsum('bhs,bhsd->bhd', wgt, kc)
        return (C_new, n_new, m_new), h

    xs = (_to_blocks(q, L), _to_blocks(k, L), _to_blocks(v, L), _to_blocks(ig, L), _to_blocks(lf, L))
    (C1, n1, m1), hb = lax.scan(step, (C0, n0, m0), xs)
    return _from_blocks(hb), C1, n1, m1


def _hgrn2_chunkwise(q, k, v, lf, S0):
    T = q.shape[1]
    L = min(CHUNK, T)
    causal = jnp.tril(jnp.ones((L, L), dtype=bool))[:, :, None]

    def step(S, inp):
        qc, kc, vc, lfc = inp
        a = jnp.cumsum(lfc, axis=2)
        diff = a[:, :, :, None, :] - a[:, :, None, :, :]
        dec = jnp.exp(jnp.where(causal, diff, -jnp.inf))
        scores = jnp.einsum('bhlc,bhsc,bhlsc->bhls', qc, kc, dec)
        o = jnp.einsum('bhlc,bhcv->bhlv', qc * jnp.exp(a), S) + jnp.einsum('bhls,bhsv->bhlv', scores, vc)
        aL = a[:, :, -1]
        S_new = jnp.exp(aL)[..., None] * S + jnp.einsum('bhsc,bhsv->bhcv', kc * jnp.exp(aL[:, :, None] - a), vc)
        return S_new, o

    xs = (_to_blocks(q, L), _to_blocks(k, L), _to_blocks(v, L), _to_blocks(lf, L))
    S1, ob = lax.scan(step, S0, xs)
    return _from_blocks(ob), S1


def _layer(x, conv_prev, C0, n0, m0, S0, lb, g_mix, w_in, b_if, w_conv, b_conv, g_mnorm, g_hnorm,
           w_branch_a, w_branch_b, w_out, g_ffn, w_ff1, w_ff2):
    f32 = jnp.float32
    B, T, _ = x.shape
    pts, _ = _split_points()
    h = _rmsnorm(x, g_mix)
    proj = h @ w_in
    qk_pre, v_m, o_m, if_pre, f_h, i_h, q_h, g_h, gate_a, gate_b = jnp.split(proj, pts, axis=-1)

    qk_pad = jnp.concatenate([conv_prev.astype(qk_pre.dtype), qk_pre], axis=1)
    new_conv = qk_pad[:, -(CONV_W - 1):]
    qk = jax.nn.silu(_causal_conv(qk_pad, w_conv, b_conv))
    q_m, k_m = jnp.split(qk.astype(f32), 2, axis=-1)
    q_m = q_m.reshape(B, T, H_M, DH_M)
    k_m = k_m.reshape(B, T, H_M, DH_M) * (DH_M ** -0.5)
    vm = v_m.astype(f32).reshape(B, T, H_M, DH_M)
    ig, fg = jnp.split((if_pre + b_if).astype(f32), 2, axis=-1)
    lf_m = jax.nn.log_sigmoid(fg)
    hm, C1, n1, m1 = _mlstm_chunkwise(q_m, k_m, vm, ig, lf_m, C0, n0, m0)
    hm = _rmsnorm(hm, g_mnorm).reshape(B, T, D_M) * jax.nn.sigmoid(o_m.astype(f32))
    y_a = hm.astype(x.dtype) @ w_branch_a

    sig_f = jax.nn.sigmoid(f_h.astype(f32)).reshape(B, T, H_H, DH_H)
    f = lb + (1.0 - lb) * sig_f
    lf_h = jnp.log(f)
    k_h = 1.0 - f
    qh = q_h.astype(f32).reshape(B, T, H_H, DH_H)
    vh = i_h.astype(f32).reshape(B, T, H_H, DH_H)
    oh, S1 = _hgrn2_chunkwise(qh, k_h, vh, lf_h, S0)
    oh = _rmsnorm(oh.reshape(B, T, D_H), g_hnorm) * jax.nn.silu(g_h.astype(f32))
    y_b = oh.astype(x.dtype) @ w_branch_b

    u = jax.nn.sigmoid(gate_a) * y_a + jax.nn.sigmoid(gate_b) * y_b
    x = x + u @ w_out

    h2 = _rmsnorm(x, g_ffn)
    x = x + jnp.square(jax.nn.relu(h2 @ w_ff1)) @ w_ff2
    return x, new_conv, C1, n1, m1, S1


def setup_inputs(seed: int = 0) -> dict:
    key = jax.random.key(seed)
    ks = jax.random.split(key, 24)
    _, n_in = _split_points()
    nrm = jax.random.normal
    b_if = jnp.concatenate([
        0.1 * nrm(ks[7], (DEPTH, H_M)),
        jnp.broadcast_to(jnp.linspace(3.0, 6.0, H_M), (DEPTH, H_M)) + 0.1 * nrm(ks[8], (DEPTH, H_M)),
    ], axis=-1)
    return {
        "x_prompt": nrm(ks[0], (BATCH, SEQ, D_MODEL), jnp.float32),
        "x_sample": nrm(ks[1], (DEC_BATCH, DEC_SEQ, D_MODEL), jnp.float32),
        "cache_mlstm_conv": nrm(ks[2], (DEPTH, DEC_BATCH, CONV_W - 1, 2 * D_M), jnp.float32),
        "state_mlstm_C": 0.3 * nrm(ks[3], (DEPTH, DEC_BATCH, H_M, DH_M, DH_M), jnp.float32),
        "state_mlstm_n": 0.3 * nrm(ks[4], (DEPTH, DEC_BATCH, H_M, DH_M), jnp.float32),
        "state_mlstm_m": 0.5 * nrm(ks[5], (DEPTH, DEC_BATCH, H_M), jnp.float32),
        "state_hgrn_S": 0.3 * nrm(ks[6], (DEPTH, DEC_BATCH, H_H, DH_H, DH_H), jnp.float32),
        "g_mix": 1.0 + 0.02 * nrm(ks[9], (DEPTH, D_MODEL)),
        "w_in": nrm(ks[10], (DEPTH, D_MODEL, n_in)) * D_MODEL ** -0.5,
        "b_if": b_if,
        "w_conv": nrm(ks[11], (DEPTH, CONV_W, 2 * D_M)) * CONV_W ** -0.5,
        "b_conv": 0.02 * nrm(ks[12], (DEPTH, 2 * D_M)),
        "g_mnorm": 1.0 + 0.02 * nrm(ks[13], (DEPTH, H_M, DH_M)),
        "g_hnorm": 1.0 + 0.02 * nrm(ks[14], (DEPTH, D_H)),
        "hgrn_lb_logits": 0.5 * nrm(ks[15], (DEPTH + 1, D_H)),
        "w_branch_a": nrm(ks[16], (DEPTH, D_M, D_MODEL)) * D_M ** -0.5,
        "w_branch_b": nrm(ks[17], (DEPTH, D_H, D_MODEL)) * D_H ** -0.5,
        "w_out": nrm(ks[18], (DEPTH, D_MODEL, D_MODEL)) * D_MODEL ** -0.5,
        "g_ffn": 1.0 + 0.02 * nrm(ks[19], (DEPTH, D_MODEL)),
        "w_ff1": nrm(ks[20], (DEPTH, D_MODEL, D_FF)) * D_MODEL ** -0.5,
        "w_ff2": nrm(ks[21], (DEPTH, D_FF, D_MODEL)) * D_FF ** -0.5,
        "g_final": 1.0 + 0.02 * nrm(ks[22], (D_MODEL,)),
    }


def reference(x_prompt, x_sample, cache_mlstm_conv, state_mlstm_C, state_mlstm_n, state_mlstm_m, state_hgrn_S,
              g_mix, w_in, b_if, w_conv, b_conv, g_mnorm, g_hnorm, hgrn_lb_logits, w_branch_a, w_branch_b,
              w_out, g_ffn, w_ff1, w_ff2, g_final):
    f32 = jnp.float32
    Bp = x_prompt.shape[0]
    lb_table = jnp.cumsum(jax.nn.softmax(hgrn_lb_logits.astype(f32), axis=0), axis=0)
    xp, xs = x_prompt, x_sample
    pc, pC, pn, pm, pS = [], [], [], [], []
    sc, sC, sn, sm, sS = [], [], [], [], []
    for l in range(DEPTH):
        lb = lb_table[l].reshape(H_H, DH_H)
        w = (g_mix[l], w_in[l], b_if[l], w_conv[l], b_conv[l], g_mnorm[l], g_hnorm[l],
             w_branch_a[l], w_branch_b[l], w_out[l], g_ffn[l], w_ff1[l], w_ff2[l])
        xp, c1, C1, n1, m1, S1 = _layer(
            xp, jnp.zeros((Bp, CONV_W - 1, 2 * D_M), xp.dtype),
            jnp.zeros((Bp, H_M, DH_M, DH_M), f32), jnp.zeros((Bp, H_M, DH_M), f32),
            jnp.zeros((Bp, H_M), f32), jnp.zeros((Bp, H_H, DH_H, DH_H), f32), lb, *w)
        pc.append(c1); pC.append(C1); pn.append(n1); pm.append(m1); pS.append(S1)
        xs, c2, C2, n2, m2, S2 = _layer(
            xs, cache_mlstm_conv[l], state_mlstm_C[l].astype(f32), state_mlstm_n[l].astype(f32),
            state_mlstm_m[l].astype(f32), state_hgrn_S[l].astype(f32), lb, *w)
        sc.append(c2); sC.append(C2); sn.append(n2); sm.append(m2); sS.append(S2)
    y_prompt = _rmsnorm(xp, g_final)
    y_sample = _rmsnorm(xs, g_final)
    return (y_prompt, y_sample,
            jnp.stack(pc), jnp.stack(pC), jnp.stack(pn), jnp.stack(pm), jnp.stack(pS),
            jnp.stack(sc), jnp.stack(sC), jnp.stack(sn), jnp.stack(sm), jnp.stack(sS))
```

```python
import functools

import jax
import jax.numpy as jnp
from jax import lax
from jax.experimental import pallas as pl
from jax.experimental.pallas import tpu as pltpu

F32 = jnp.float32
BF16 = jnp.bfloat16
HIGHEST = lax.Precision.HIGHEST

EPS = 1e-6
CHUNK = 64
CONV_W = 4
H_M = 4
DH_H = 128
SUB = 16
LANES = 128
CONV_PAD = 8
VMEM_LIMIT = 56 * 1024 * 1024

NT_DIMS = (((1,), (1,)), ((), ()))
TN_DIMS = (((0,), (0,)), ((), ()))


def _rms(x, g):
    return x * lax.rsqrt(jnp.mean(x * x, axis=-1, keepdims=True) + EPS) * g


def _sigmoid(x):
    return 1.0 / (1.0 + jnp.exp(-x))


def _params(sem):
    return pltpu.CompilerParams(dimension_semantics=sem, vmem_limit_bytes=VMEM_LIMIT)


def _inproj_kernel(x_ref, g_ref, w_ref, wif_ref, bif_ref, proj_ref, gates_ref, h_scr):
    @pl.when(pl.program_id(1) == 0)
    def _():
        hb = _rms(x_ref[...], g_ref[...]).astype(BF16)
        h_scr[...] = hb
        gates_ref[...] = jnp.dot(hb, wif_ref[...], preferred_element_type=F32) + bif_ref[...]

    proj_ref[...] = jnp.dot(h_scr[...], w_ref[...], preferred_element_type=F32)


def _inproj(x2d, g_mix, w_main, w_if, b_if):
    m, d = x2d.shape
    n = w_main.shape[1]
    tm = min(m, 1024)
    tn = 1024
    return pl.pallas_call(
        _inproj_kernel,
        grid=(m // tm, n // tn),
        in_specs=[
            pl.BlockSpec((tm, d), lambda i, j: (i, 0)),
            pl.BlockSpec((1, d), lambda i, j: (0, 0)),
            pl.BlockSpec((d, tn), lambda i, j: (0, j)),
            pl.BlockSpec((d, LANES), lambda i, j: (0, 0)),
            pl.BlockSpec((1, LANES), lambda i, j: (0, 0)),
        ],
        out_specs=[
            pl.BlockSpec((tm, tn), lambda i, j: (i, j)),
            pl.BlockSpec((tm, LANES), lambda i, j: (i, 0)),
        ],
        out_shape=[jax.ShapeDtypeStruct((m, n), F32), jax.ShapeDtypeStruct((m, LANES), F32)],
        scratch_shapes=[pltpu.VMEM((tm, d), BF16)],
        compiler_params=_params(("parallel", "arbitrary")),
        name="inproj",
    )(x2d, g_mix, w_main, w_if, b_if)


def _mlstm_kernel(qk_ref, v_ref, o_ref, gates_ref, convprev_ref, c0_ref, n0_ref, m0_ref,
                  wconv_ref, bconv_ref, gnorm_ref,
                  hmg_ref, newconv_ref, c1_ref, n1_ref, m1_ref,
                  xpad, c_scr, n_scr, m_scr, *, L, dm, dh):
    c = pl.program_id(1)
    nc = pl.num_programs(1)
    hist0 = CONV_PAD - (CONV_W - 1)

    @pl.when(c == 0)
    def _():
        xpad[hist0:CONV_PAD, :] = convprev_ref[...]
        c_scr[...] = c0_ref[...]
        n_scr[...] = n0_ref[...]
        m_scr[...] = m0_ref[...]

    xpad[CONV_PAD:CONV_PAD + L, :] = qk_ref[...]
    conv = bconv_ref[...]
    for i in range(CONV_W):
        conv = conv + xpad[hist0 + i:hist0 + i + L, :] * wconv_ref[i:i + 1, :]
    hist = xpad[CONV_PAD + L - (CONV_W - 1):CONV_PAD + L, :]
    xpad[hist0:CONV_PAD, :] = hist
    qk = conv * _sigmoid(conv)
    q = qk[:, :dm]
    k = qk[:, dm:] * (dh ** -0.5)

    g = gates_ref[...]
    col = lax.broadcasted_iota(jnp.int32, g.shape, 1)
    lf = jnp.minimum(g, 0.0) - jnp.log1p(jnp.exp(-jnp.abs(g)))
    x_cols = jnp.where(col < H_M, g, lf)
    rr = lax.broadcasted_iota(jnp.int32, (L, L), 0)
    cc = lax.broadcasted_iota(jnp.int32, (L, L), 1)
    causal = rr >= cc
    tril = causal.astype(F32)
    cum = jnp.dot(tril, x_cols, preferred_element_type=F32, precision=HIGHEST)
    xc = jnp.where(col < H_M, g, cum)
    eye = (lax.broadcasted_iota(jnp.int32, (8, LANES), 0)
           == lax.broadcasted_iota(jnp.int32, (8, LANES), 1)).astype(F32)
    rows = lax.dot_general(eye, xc, NT_DIMS, preferred_element_type=F32, precision=HIGHEST)

    vv = v_ref[...]
    oo = o_ref[...]
    m_all = m_scr[...]
    for h in range(H_M):
        sl = slice(h * dh, (h + 1) * dh)
        qf = q[:, sl]
        kf = k[:, sl]
        vf = vv[:, sl]
        qb = qf.astype(BF16)
        kb = kf.astype(BF16)
        bcol = cum[:, H_M + h:H_M + h + 1]
        igcol = g[:, h:h + 1]
        brow = rows[H_M + h:H_M + h + 1, :]
        igrow = rows[h:h + 1, :]
        m_prev = m_all[:, h:h + 1]
        dlog = jnp.where(causal, bcol - brow + igrow, -jnp.inf)
        m_inter = bcol + m_prev
        m_t = jnp.maximum(m_inter, jnp.max(dlog, axis=-1, keepdims=True))
        s = lax.dot_general(qb, kb, NT_DIMS, preferred_element_type=F32) * jnp.exp(dlog - m_t)
        sc_inter = jnp.exp(m_inter - m_t)
        c_old = c_scr[h]
        n_old = n_scr[h:h + 1, :]
        num = (sc_inter * lax.dot_general(qb, c_old.astype(BF16), NT_DIMS, preferred_element_type=F32)
               + jnp.dot(s.astype(BF16), vf.astype(BF16), preferred_element_type=F32))
        den = sc_inter * jnp.sum(qf * n_old, axis=-1, keepdims=True) + jnp.sum(s, axis=-1, keepdims=True)
        hout = num / jnp.maximum(jnp.abs(den), jnp.exp(-m_t))
        m_new = m_t[L - 1:L, :]
        b_last = bcol[L - 1:L, :]
        decay = jnp.exp(b_last + m_prev - m_new)
        wgt = jnp.exp(b_last - bcol + igcol - m_new)
        c_scr[h] = decay * c_old + lax.dot_general((vf * wgt).astype(BF16), kb, TN_DIMS,
                                                   preferred_element_type=F32)
        n_scr[h:h + 1, :] = decay * n_old + jnp.sum(wgt * kf, axis=0, keepdims=True)
        m_scr[:, h:h + 1] = m_new
        y = _rms(hout, gnorm_ref[:, sl]) * _sigmoid(oo[:, sl])
        hmg_ref[:, sl] = y.astype(BF16)

    @pl.when(c == nc - 1)
    def _():
        newconv_ref[...] = hist
        c1_ref[...] = c_scr[...]
        n1_ref[...] = n_scr[...]
        m1_ref[...] = m_scr[...]


def _mlstm(proj, gates, conv_prev, c0, n0, m0, w_conv, b_conv, g_mnorm, *, B, T):
    L = min(CHUNK, T)
    nc = T // L
    dm = c0.shape[1] * c0.shape[2]
    dh = c0.shape[2]
    row = lambda b, c: b * nc + c
    kern = functools.partial(_mlstm_kernel, L=L, dm=dm, dh=dh)
    return pl.pallas_call(
        kern,
        grid=(B, nc),
        in_specs=[
            pl.BlockSpec((L, 2 * dm), lambda b, c: (row(b, c), 0)),
            pl.BlockSpec((L, dm), lambda b, c: (row(b, c), 2)),
            pl.BlockSpec((L, dm), lambda b, c: (row(b, c), 3)),
            pl.BlockSpec((L, LANES), lambda b, c: (row(b, c), 0)),
            pl.BlockSpec((None, CONV_W - 1, 2 * dm), lambda b, c: (b, 0, 0)),
            pl.BlockSpec((None, H_M, dh, dh), lambda b, c: (b, 0, 0, 0)),
            pl.BlockSpec((None, H_M, dh), lambda b, c: (b, 0, 0)),
            pl.BlockSpec((None, 1, H_M), lambda b, c: (b, 0, 0)),
            pl.BlockSpec((CONV_W, 2 * dm), lambda b, c: (0, 0)),
            pl.BlockSpec((1, 2 * dm), lambda b, c: (0, 0)),
            pl.BlockSpec((1, dm), lambda b, c: (0, 0)),
        ],
        out_specs=[
            pl.BlockSpec((L, dm), lambda b, c: (row(b, c), 0)),
            pl.BlockSpec((None, CONV_W - 1, 2 * dm), lambda b, c: (b, 0, 0)),
            pl.BlockSpec((None, H_M, dh, dh), lambda b, c: (b, 0, 0, 0)),
            pl.BlockSpec((None, H_M, dh), lambda b, c: (b, 0, 0)),
            pl.BlockSpec((None, 1, H_M), lambda b, c: (b, 0, 0)),
        ],
        out_shape=[
            jax.ShapeDtypeStruct((B * T, dm), BF16),
            jax.ShapeDtypeStruct((B, CONV_W - 1, 2 * dm), F32),
            jax.ShapeDtypeStruct((B, H_M, dh, dh), F32),
            jax.ShapeDtypeStruct((B, H_M, dh), F32),
            jax.ShapeDtypeStruct((B, 1, H_M), F32),
        ],
        scratch_shapes=[
            pltpu.VMEM((CONV_PAD + L, 2 * dm), F32),
            pltpu.VMEM((H_M, dh, dh), F32),
            pltpu.VMEM((H_M, dh), F32),
            pltpu.VMEM((1, H_M), F32),
        ],
        compiler_params=_params(("parallel", "arbitrary")),
        name="mlstm",
    )(proj, proj, proj, gates, conv_prev, c0, n0, m0, w_conv, b_conv, g_mnorm)


def _hgrn_kernel(f_ref, i_ref, q_ref, g_ref, lbl_ref, s0_ref, gnorm_ref,
                 ohg_ref, s1_ref,
                 st_scr, a_scr, q_scr, k_scr, sc_scr, o_scr, *, L, nh):
    c = pl.program_id(1)
    nc = pl.num_programs(1)
    nb = L // SUB

    @pl.when(c == 0)
    def _():
        for h in range(nh):
            st_scr[h] = s0_ref[h].T

    lbl = lbl_ref[...]
    mx = jnp.max(lbl, axis=0, keepdims=True)
    ex = jnp.exp(lbl - mx)
    lb = ex[0:1, :] / jnp.sum(ex, axis=0, keepdims=True)

    f = lb + (1.0 - lb) * _sigmoid(f_ref[...])
    lf = jnp.log(f)
    kk = 1.0 - f
    qq = q_ref[...]
    vv = i_ref[...]

    ridx = lax.broadcasted_iota(jnp.int32, lf.shape, 0)
    a = lf
    sh = 1
    while sh < L:
        a = a + jnp.where(ridx >= sh, pltpu.roll(a, sh, 0), 0.0)
        sh *= 2
    a_scr[...] = a
    q_scr[...] = qq
    k_scr[...] = kk

    lane = lax.broadcasted_iota(jnp.int32, (SUB, LANES), 1)
    sc_scr[...] = jnp.zeros_like(sc_scr)

    def diag_body(s, carry):
        r0 = pl.multiple_of((s // SUB) * SUB, SUB)
        a_blk = a_scr[pl.ds(r0, SUB), :]
        q_blk = q_scr[pl.ds(r0, SUB), :]
        a_s = a_scr[pl.ds(s, 1), :]
        k_s = k_scr[pl.ds(s, 1), :]
        p = q_blk * jnp.exp(jnp.minimum(a_blk - a_s, 0.0)) * k_s
        for h in range(nh):
            colv = jnp.sum(p[:, h * DH_H:(h + 1) * DH_H], axis=-1, keepdims=True)
            old = sc_scr[h, pl.ds(r0, SUB), :]
            sc_scr[h, pl.ds(r0, SUB), :] = jnp.where(lane == s, colv, old)
        return carry

    lax.fori_loop(0, L, diag_body, 0)

    a_last = a[L - 1:L, :]
    qa = (qq * jnp.exp(a)).astype(BF16)
    kdec = (kk * jnp.exp(a_last - a)).astype(BF16)
    vb = vv.astype(BF16)
    dec_last = jnp.exp(a_last)

    qs_blocks = [None]
    ks_blocks = [None]
    for i in range(1, nb):
        a_ref = a[i * SUB - 1:i * SUB, :]
        qs_blocks.append((qq[i * SUB:(i + 1) * SUB, :] * jnp.exp(a[i * SUB:(i + 1) * SUB, :] - a_ref)).astype(BF16))
        ks_blocks.append((kk * jnp.exp(jnp.minimum(a_ref - a, 0.0))).astype(BF16))

    rr = lax.broadcasted_iota(jnp.int32, (SUB, L), 0)
    cc = lax.broadcasted_iota(jnp.int32, (SUB, L), 1)
    for h in range(nh):
        sl = slice(h * DH_H, (h + 1) * DH_H)
        st = st_scr[h]
        blocks = []
        for i in range(nb):
            dg = sc_scr[h, i * SUB:(i + 1) * SUB, :][:, :L]
            in_diag = (cc >= i * SUB) & (cc <= rr + i * SUB)
            blk = jnp.where(in_diag, dg, 0.0)
            if i > 0:
                off = lax.dot_general(qs_blocks[i][:, sl], ks_blocks[i][:, sl], NT_DIMS,
                                      preferred_element_type=F32)
                blk = jnp.where(cc < i * SUB, off, blk)
            blocks.append(blk)
        scores = jnp.concatenate(blocks, axis=0) if nb > 1 else blocks[0]
        o_h = (lax.dot_general(qa[:, sl], st.astype(BF16), NT_DIMS, preferred_element_type=F32)
               + jnp.dot(scores.astype(BF16), vb[:, sl], preferred_element_type=F32))
        o_scr[:, sl] = o_h
        st_scr[h] = st * dec_last[:, sl] + lax.dot_general(vb[:, sl], kdec[:, sl], TN_DIMS,
                                                           preferred_element_type=F32)

    gg = g_ref[...]
    y = _rms(o_scr[...], gnorm_ref[...]) * (gg * _sigmoid(gg))
    ohg_ref[...] = y.astype(BF16)

    @pl.when(c == nc - 1)
    def _():
        for h in range(nh):
            s1_ref[h] = st_scr[h].T


def _hgrn(proj, lb_logits, s0, g_hnorm, *, B, T):
    L = min(CHUNK, T)
    nc = T // L
    nh = s0.shape[1]
    dhh = nh * DH_H
    row = lambda b, c: b * nc + c
    kern = functools.partial(_hgrn_kernel, L=L, nh=nh)
    return pl.pallas_call(
        kern,
        grid=(B, nc),
        in_specs=[
            pl.BlockSpec((L, dhh), lambda b, c: (row(b, c), 4)),
            pl.BlockSpec((L, dhh), lambda b, c: (row(b, c), 5)),
            pl.BlockSpec((L, dhh), lambda b, c: (row(b, c), 6)),
            pl.BlockSpec((L, dhh), lambda b, c: (row(b, c), 7)),
            pl.BlockSpec((2, dhh), lambda b, c: (0, 0)),
            pl.BlockSpec((None, nh, DH_H, DH_H), lambda b, c: (b, 0, 0, 0)),
            pl.BlockSpec((1, dhh), lambda b, c: (0, 0)),
        ],
        out_specs=[
            pl.BlockSpec((L, dhh), lambda b, c: (row(b, c), 0)),
            pl.BlockSpec((None, nh, DH_H, DH_H), lambda b, c: (b, 0, 0, 0)),
        ],
        out_shape=[
            jax.ShapeDtypeStruct((B * T, dhh), BF16),
            jax.ShapeDtypeStruct((B, nh, DH_H, DH_H), F32),
        ],
        scratch_shapes=[
            pltpu.VMEM((nh, DH_H, DH_H), F32),
            pltpu.VMEM((L, dhh), F32),
            pltpu.VMEM((L, dhh), F32),
            pltpu.VMEM((L, dhh), F32),
            pltpu.VMEM((nh, L, LANES), F32),
            pltpu.VMEM((L, dhh), F32),
        ],
        compiler_params=_params(("parallel", "arbitrary")),
        name="hgrn",
    )(proj, proj, proj, proj, lb_logits, s0, g_hnorm)


def _merge_kernel(hmg_ref, ohg_ref, ga_ref, gb_ref, x_ref, wa_ref, wb_ref, wo_ref, x1_ref):
    ya = jnp.dot(hmg_ref[...], wa_ref[...], preferred_element_type=F32)
    yb = jnp.dot(ohg_ref[...], wb_ref[...], preferred_element_type=F32)
    u = _sigmoid(ga_ref[...]) * ya + _sigmoid(gb_ref[...]) * yb
    x1_ref[...] = x_ref[...] + jnp.dot(u.astype(BF16), wo_ref[...], preferred_element_type=F32)


def _merge(hmg, ohg, proj, x2d, w_a, w_b, w_o):
    m, d = x2d.shape
    dm = hmg.shape[1]
    tm = min(m, 256)
    const = lambda i: (0, 0)
    return pl.pallas_call(
        _merge_kernel,
        grid=(m // tm,),
        in_specs=[
            pl.BlockSpec((tm, dm), lambda i: (i, 0)),
            pl.BlockSpec((tm, dm), lambda i: (i, 0)),
            pl.BlockSpec((tm, d), lambda i: (i, 4)),
            pl.BlockSpec((tm, d), lambda i: (i, 5)),
            pl.BlockSpec((tm, d), lambda i: (i, 0)),
            pl.BlockSpec((dm, d), const, pipeline_mode=pl.Buffered(1)),
            pl.BlockSpec((dm, d), const, pipeline_mode=pl.Buffered(1)),
            pl.BlockSpec((d, d), const, pipeline_mode=pl.Buffered(1)),
        ],
        out_specs=pl.BlockSpec((tm, d), lambda i: (i, 0)),
        out_shape=jax.ShapeDtypeStruct((m, d), F32),
        compiler_params=_params(("parallel",)),
        name="merge",
    )(hmg, ohg, proj, proj, x2d, w_a, w_b, w_o)


def _ffn_kernel(x1_ref, gffn_ref, w1_ref, w2_ref, gfin_ref, y_ref, h_scr, acc_scr):
    j = pl.program_id(1)

    @pl.when(j == 0)
    def _():
        h_scr[...] = _rms(x1_ref[...], gffn_ref[...]).astype(BF16)
        acc_scr[...] = jnp.zeros_like(acc_scr)

    t = jnp.maximum(jnp.dot(h_scr[...], w1_ref[...], preferred_element_type=F32), 0.0)
    acc_scr[...] += jnp.dot((t * t).astype(BF16), w2_ref[...], preferred_element_type=F32)

    @pl.when(j == pl.num_programs(1) - 1)
    def _():
        y_ref[...] = _rms(x1_ref[...] + acc_scr[...], gfin_ref[...])


def _ffn(x1, g_ffn, w1, w2, g_final):
    m, d = x1.shape
    dff = w1.shape[1]
    tm = min(m, 512)
    tf = 512
    return pl.pallas_call(
        _ffn_kernel,
        grid=(m // tm, dff // tf),
        in_specs=[
            pl.BlockSpec((tm, d), lambda i, j: (i, 0)),
            pl.BlockSpec((1, d), lambda i, j: (0, 0)),
            pl.BlockSpec((d, tf), lambda i, j: (0, j)),
            pl.BlockSpec((tf, d), lambda i, j: (j, 0)),
            pl.BlockSpec((1, d), lambda i, j: (0, 0)),
        ],
        out_specs=pl.BlockSpec((tm, d), lambda i, j: (i, 0)),
        out_shape=jax.ShapeDtypeStruct((m, d), F32),
        scratch_shapes=[pltpu.VMEM((tm, d), BF16), pltpu.VMEM((tm, d), F32)],
        compiler_params=_params(("parallel", "arbitrary")),
        name="ffn",
    )(x1, g_ffn, w1, w2, g_final)


def _group(x, conv_prev, c0, n0, m0, s0, wts):
    B, T, d = x.shape
    x2d = x.reshape(B * T, d)
    proj, gates = _inproj(x2d, wts["g_mix"], wts["w_main"], wts["w_if"], wts["b_if"])
    hmg, new_conv, c1, n1, m1 = _mlstm(proj, gates, conv_prev, c0, n0, m0.reshape(B, 1, -1),
                                       wts["w_conv"], wts["b_conv"], wts["g_mnorm"], B=B, T=T)
    ohg, s1 = _hgrn(proj, wts["lb_logits"], s0, wts["g_hnorm"], B=B, T=T)
    x1 = _merge(hmg, ohg, proj, x2d, wts["w_a"], wts["w_b"], wts["w_o"])
    y = _ffn(x1, wts["g_ffn"], wts["w_ff1"], wts["w_ff2"], wts["g_final"])
    return (y.reshape(B, T, d), new_conv[None], c1[None], n1[None], m1.reshape(1, B, -1), s1[None])


def kernel(x_prompt, x_sample, cache_mlstm_conv, state_mlstm_C, state_mlstm_n, state_mlstm_m, state_hgrn_S,
           g_mix, w_in, b_if, w_conv, b_conv, g_mnorm, g_hnorm, hgrn_lb_logits, w_branch_a, w_branch_b,
           w_out, g_ffn, w_ff1, w_ff2, g_final):
    depth = w_in.shape[0]
    assert depth == 1, "single-layer trunk only"
    d = x_prompt.shape[-1]
    dm = w_branch_a.shape[1]
    dhh = w_branch_b.shape[1]
    nh_h = dhh // DH_H
    dh_m = dm // H_M
    assert hgrn_lb_logits.shape[0] == depth + 1
    assert 2 * dm == d and dhh == dm

    w = w_in[0]
    n_if = 2 * H_M
    p_if = 4 * dm
    w_main = jnp.concatenate([w[:, :p_if], w[:, p_if + n_if:]], axis=1).astype(BF16)
    w_if = jnp.pad(w[:, p_if:p_if + n_if], ((0, 0), (0, LANES - n_if))).astype(BF16)
    bif = jnp.pad(b_if[0].astype(F32), (0, LANES - n_if)).reshape(1, LANES)
    wts = dict(
        g_mix=g_mix[0].reshape(1, d), w_main=w_main, w_if=w_if, b_if=bif,
        w_conv=w_conv[0], b_conv=b_conv[0].reshape(1, -1), g_mnorm=g_mnorm[0].reshape(1, dm),
        lb_logits=hgrn_lb_logits.astype(F32), g_hnorm=g_hnorm[0].reshape(1, dhh),
        w_a=w_branch_a[0].astype(BF16), w_b=w_branch_b[0].astype(BF16), w_o=w_out[0].astype(BF16),
        g_ffn=g_ffn[0].reshape(1, d), w_ff1=w_ff1[0].astype(BF16), w_ff2=w_ff2[0].astype(BF16),
        g_final=g_final.reshape(1, d),
    )

    bp = x_prompt.shape[0]
    zeros = lambda *s: jnp.zeros(s, F32)
    outs_p = _group(x_prompt, zeros(bp, CONV_W - 1, 2 * dm), zeros(bp, H_M, dh_m, dh_m), zeros(bp, H_M, dh_m),
                    zeros(bp, H_M), zeros(bp, nh_h, DH_H, DH_H), wts)
    outs_s = _group(x_sample, cache_mlstm_conv[0], state_mlstm_C[0].astype(F32), state_mlstm_n[0].astype(F32),
                    state_mlstm_m[0].astype(F32), state_hgrn_S[0].astype(F32), wts)
    return (outs_p[0], outs_s[0]) + outs_p[1:] + outs_s[1:]
```

```python
import functools

import jax
import jax.numpy as jnp
from jax import lax
from jax.experimental import pallas as pl
from jax.experimental.pallas import tpu as pltpu

F32 = jnp.float32
BF16 = jnp.bfloat16
HIGHEST = lax.Precision.HIGHEST

EPS = 1e-6
CHUNK = 64
CONV_W = 4
H_M = 4
DH_H = 128
SUB = 32
SAFE_SPAN = 60.0
LANES = 128
CONV_PAD = 8
VMEM_LIMIT = 56 * 1024 * 1024

NT_DIMS = (((1,), (1,)), ((), ()))
TN_DIMS = (((0,), (0,)), ((), ()))


def _rms(x, g):
    return x * lax.rsqrt(jnp.mean(x * x, axis=-1, keepdims=True) + EPS) * g


def _sigmoid(x):
    return 1.0 / (1.0 + jnp.exp(-x))


def _params(sem):
    return pltpu.CompilerParams(dimension_semantics=sem, vmem_limit_bytes=VMEM_LIMIT)


def _inproj_kernel(x_ref, g_ref, w_ref, wif_ref, bif_ref, proj_ref, gates_ref, h_scr):
    @pl.when(pl.program_id(1) == 0)
    def _():
        hb = _rms(x_ref[...], g_ref[...]).astype(BF16)
        h_scr[...] = hb
        gates_ref[...] = jnp.dot(hb, wif_ref[...], preferred_element_type=F32) + bif_ref[...]

    proj_ref[...] = jnp.dot(h_scr[...], w_ref[...], preferred_element_type=F32)


def _inproj(x2d, g_mix, w_main, w_if, b_if):
    m, d = x2d.shape
    n = w_main.shape[1]
    tm = min(m, 1024)
    tn = 1024
    return pl.pallas_call(
        _inproj_kernel,
        grid=(m // tm, n // tn),
        in_specs=[
            pl.BlockSpec((tm, d), lambda i, j: (i, 0)),
            pl.BlockSpec((1, d), lambda i, j: (0, 0)),
            pl.BlockSpec((d, tn), lambda i, j: (0, j)),
            pl.BlockSpec((d, LANES), lambda i, j: (0, 0)),
            pl.BlockSpec((1, LANES), lambda i, j: (0, 0)),
        ],
        out_specs=[
            pl.BlockSpec((tm, tn), lambda i, j: (i, j)),
            pl.BlockSpec((tm, LANES), lambda i, j: (i, 0)),
        ],
        out_shape=[jax.ShapeDtypeStruct((m, n), F32), jax.ShapeDtypeStruct((m, LANES), F32)],
        scratch_shapes=[pltpu.VMEM((tm, d), BF16)],
        compiler_params=_params(("parallel", "arbitrary")),
        name="inproj",
    )(x2d, g_mix, w_main, w_if, b_if)


def _mlstm_kernel(qk_ref, v_ref, o_ref, gates_ref, convprev_ref, c0_ref, n0_ref, m0_ref,
                  wconv_ref, bconv_ref, gnorm_ref,
                  hmg_ref, newconv_ref, c1_ref, n1_ref, m1_ref,
                  xpad, c_scr, n_scr, m_scr, *, L, dm, dh):
    c = pl.program_id(1)
    nc = pl.num_programs(1)
    hist0 = CONV_PAD - (CONV_W - 1)

    @pl.when(c == 0)
    def _():
        xpad[hist0:CONV_PAD, :] = convprev_ref[...]
        c_scr[...] = c0_ref[...]
        n_scr[...] = n0_ref[...]
        m_scr[...] = m0_ref[...]

    xpad[CONV_PAD:CONV_PAD + L, :] = qk_ref[...]
    conv = bconv_ref[...]
    for i in range(CONV_W):
        conv = conv + xpad[hist0 + i:hist0 + i + L, :] * wconv_ref[i:i + 1, :]
    hist = xpad[CONV_PAD + L - (CONV_W - 1):CONV_PAD + L, :]
    xpad[hist0:CONV_PAD, :] = hist
    qk = conv * _sigmoid(conv)
    q = qk[:, :dm]
    k = qk[:, dm:] * (dh ** -0.5)

    g = gates_ref[...]
    col = lax.broadcasted_iota(jnp.int32, g.shape, 1)
    lf = jnp.minimum(g, 0.0) - jnp.log1p(jnp.exp(-jnp.abs(g)))
    x_cols = jnp.where(col < H_M, g, lf)
    rr = lax.broadcasted_iota(jnp.int32, (L, L), 0)
    cc = lax.broadcasted_iota(jnp.int32, (L, L), 1)
    causal = rr >= cc
    tril = causal.astype(F32)
    cum = jnp.dot(tril, x_cols, preferred_element_type=F32, precision=HIGHEST)
    xc = jnp.where(col < H_M, g, cum)
    eye = (lax.broadcasted_iota(jnp.int32, (8, LANES), 0)
           == lax.broadcasted_iota(jnp.int32, (8, LANES), 1)).astype(F32)
    rows = lax.dot_general(eye, xc, NT_DIMS, preferred_element_type=F32, precision=HIGHEST)

    vv = v_ref[...]
    oo = o_ref[...]
    m_all = m_scr[...]
    for h in range(H_M):
        sl = slice(h * dh, (h + 1) * dh)
        qf = q[:, sl]
        kf = k[:, sl]
        vf = vv[:, sl]
        qb = qf.astype(BF16)
        kb = kf.astype(BF16)
        bcol = cum[:, H_M + h:H_M + h + 1]
        igcol = g[:, h:h + 1]
        brow = rows[H_M + h:H_M + h + 1, :]
        igrow = rows[h:h + 1, :]
        m_prev = m_all[:, h:h + 1]
        dlog = jnp.where(causal, bcol - brow + igrow, -jnp.inf)
        m_inter = bcol + m_prev
        m_t = jnp.maximum(m_inter, jnp.max(dlog, axis=-1, keepdims=True))
        s = lax.dot_general(qb, kb, NT_DIMS, preferred_element_type=F32) * jnp.exp(dlog - m_t)
        sc_inter = jnp.exp(m_inter - m_t)
        c_old = c_scr[h]
        n_old = n_scr[h:h + 1, :]
        num = (sc_inter * lax.dot_general(qb, c_old.astype(BF16), NT_DIMS, preferred_element_type=F32)
               + jnp.dot(s.astype(BF16), vf.astype(BF16), preferred_element_type=F32))
        den = sc_inter * jnp.sum(qf * n_old, axis=-1, keepdims=True) + jnp.sum(s, axis=-1, keepdims=True)
        hout = num / jnp.maximum(jnp.abs(den), jnp.exp(-m_t))
        m_new = m_t[L - 1:L, :]
        b_last = bcol[L - 1:L, :]
        decay = jnp.exp(b_last + m_prev - m_new)
        wgt = jnp.exp(b_last - bcol + igcol - m_new)
        c_scr[h] = decay * c_old + lax.dot_general((vf * wgt).astype(BF16), kb, TN_DIMS,
                                                   preferred_element_type=F32)
        n_scr[h:h + 1, :] = decay * n_old + jnp.sum(wgt * kf, axis=0, keepdims=True)
        m_scr[:, h:h + 1] = m_new
        y = _rms(hout, gnorm_ref[:, sl]) * _sigmoid(oo[:, sl])
        hmg_ref[:, sl] = y.astype(BF16)

    @pl.when(c == nc - 1)
    def _():
        newconv_ref[...] = hist
        c1_ref[...] = c_scr[...]
        n1_ref[...] = n_scr[...]
        m1_ref[...] = m_scr[...]


def _mlstm(proj, gates, conv_prev, c0, n0, m0, w_conv, b_conv, g_mnorm, *, B, T):
    L = min(CHUNK, T)
    nc = T // L
    dm = c0.shape[1] * c0.shape[2]
    dh = c0.shape[2]
    row = lambda b, c: b * nc + c
    kern = functools.partial(_mlstm_kernel, L=L, dm=dm, dh=dh)
    return pl.pallas_call(
        kern,
        grid=(B, nc),
        in_specs=[
            pl.BlockSpec((L, 2 * dm), lambda b, c: (row(b, c), 0)),
            pl.BlockSpec((L, dm), lambda b, c: (row(b, c), 2)),
            pl.BlockSpec((L, dm), lambda b, c: (row(b, c), 3)),
            pl.BlockSpec((L, LANES), lambda b, c: (row(b, c), 0)),
            pl.BlockSpec((None, CONV_W - 1, 2 * dm), lambda b, c: (b, 0, 0)),
            pl.BlockSpec((None, H_M, dh, dh), lambda b, c: (b, 0, 0, 0)),
            pl.BlockSpec((None, H_M, dh), lambda b, c: (b, 0, 0)),
            pl.BlockSpec((None, 1, H_M), lambda b, c: (b, 0, 0)),
            pl.BlockSpec((CONV_W, 2 * dm), lambda b, c: (0, 0)),
            pl.BlockSpec((1, 2 * dm), lambda b, c: (0, 0)),
            pl.BlockSpec((1, dm), lambda b, c: (0, 0)),
        ],
        out_specs=[
            pl.BlockSpec((L, dm), lambda b, c: (row(b, c), 0)),
            pl.BlockSpec((None, CONV_W - 1, 2 * dm), lambda b, c: (b, 0, 0)),
            pl.BlockSpec((None, H_M, dh, dh), lambda b, c: (b, 0, 0, 0)),
            pl.BlockSpec((None, H_M, dh), lambda b, c: (b, 0, 0)),
            pl.BlockSpec((None, 1, H_M), lambda b, c: (b, 0, 0)),
        ],
        out_shape=[
            jax.ShapeDtypeStruct((B * T, dm), BF16),
            jax.ShapeDtypeStruct((B, CONV_W - 1, 2 * dm), F32),
            jax.ShapeDtypeStruct((B, H_M, dh, dh), F32),
            jax.ShapeDtypeStruct((B, H_M, dh), F32),
            jax.ShapeDtypeStruct((B, 1, H_M), F32),
        ],
        scratch_shapes=[
            pltpu.VMEM((CONV_PAD + L, 2 * dm), F32),
            pltpu.VMEM((H_M, dh, dh), F32),
            pltpu.VMEM((H_M, dh), F32),
            pltpu.VMEM((1, H_M), F32),
        ],
        compiler_params=_params(("parallel", "arbitrary")),
        name="mlstm",
    )(proj, proj, proj, gates, conv_prev, c0, n0, m0, w_conv, b_conv, g_mnorm)


def _hgrn_kernel(f_ref, i_ref, q_ref, g_ref, lbl_ref, s0_ref, gnorm_ref,
                 ohg_ref, s1_ref,
                 st_scr, a_scr, q_scr, k_scr, sc_scr, o_scr, *, L, nh):
    c = pl.program_id(1)
    nc = pl.num_programs(1)
    sub = min(SUB, L)
    nb = L // sub

    @pl.when(c == 0)
    def _():
        sc_scr[...] = jnp.zeros_like(sc_scr)
        for h in range(nh):
            st_scr[h] = s0_ref[h].T

    lbl = lbl_ref[...]
    mx = jnp.max(lbl, axis=0, keepdims=True)
    ex = jnp.exp(lbl - mx)
    lb = ex[0:1, :] / jnp.sum(ex, axis=0, keepdims=True)

    f = lb + (1.0 - lb) * _sigmoid(f_ref[...])
    lf = jnp.log(f)
    kk = 1.0 - f
    qq = q_ref[...]
    vv = i_ref[...]

    ridx = lax.broadcasted_iota(jnp.int32, lf.shape, 0)
    a = lf
    sh = 1
    while sh < L:
        a = a + jnp.where(ridx >= sh, pltpu.roll(a, sh, 0), 0.0)
        sh *= 2

    refs = [jnp.zeros_like(a[0:1, :])] + [a[i * sub - 1:i * sub, :] for i in range(1, nb)]
    span = refs[0] - a[sub - 1:sub, :]
    for i in range(1, nb):
        span = jnp.maximum(span, refs[i] - a[(i + 1) * sub - 1:(i + 1) * sub, :])
    use_exact = jnp.logical_not(jnp.max(span) <= SAFE_SPAN)

    @pl.when(use_exact)
    def _():
        a_scr[...] = a
        q_scr[...] = qq
        k_scr[...] = kk
        lane = lax.broadcasted_iota(jnp.int32, (sub, LANES), 1)

        def diag_body(s, carry):
            r0 = pl.multiple_of((s // sub) * sub, sub)
            a_blk = a_scr[pl.ds(r0, sub), :]
            q_blk = q_scr[pl.ds(r0, sub), :]
            a_s = a_scr[pl.ds(s, 1), :]
            k_s = k_scr[pl.ds(s, 1), :]
            p = q_blk * jnp.exp(jnp.minimum(a_blk - a_s, 0.0)) * k_s
            for h in range(nh):
                colv = jnp.sum(p[:, h * DH_H:(h + 1) * DH_H], axis=-1, keepdims=True)
                old = sc_scr[h, pl.ds(r0, sub), :]
                sc_scr[h, pl.ds(r0, sub), :] = jnp.where(lane == s, colv, old)
            return carry

        lax.fori_loop(0, L, diag_body, 0)

    a_last = a[L - 1:L, :]
    qa = (qq * jnp.exp(a)).astype(BF16)
    kdec = (kk * jnp.exp(a_last - a)).astype(BF16)
    vb = vv.astype(BF16)
    dec_last = jnp.exp(a_last)

    qs = []
    ks = []
    for i in range(nb):
        rows = slice(i * sub, (i + 1) * sub)
        nk = (i + 1) * sub
        qs.append(qa[rows, :] if i == 0 else (qq[rows, :] * jnp.exp(a[rows, :] - refs[i])).astype(BF16))
        ks.append((kk[:nk, :] * jnp.exp(jnp.minimum(refs[i] - a[:nk, :], SAFE_SPAN))).astype(BF16))

    for h in range(nh):
        sl = slice(h * DH_H, (h + 1) * DH_H)
        st = st_scr[h]
        inter = lax.dot_general(qa[:, sl], st.astype(BF16), NT_DIMS, preferred_element_type=F32)
        for i in range(nb):
            rows = slice(i * sub, (i + 1) * sub)
            nk = (i + 1) * sub
            rr = lax.broadcasted_iota(jnp.int32, (sub, nk), 0) + i * sub
            cc = lax.broadcasted_iota(jnp.int32, (sub, nk), 1)
            mm = lax.dot_general(qs[i][:, sl], ks[i][:, sl], NT_DIMS, preferred_element_type=F32)
            blk = jnp.where(cc <= rr, mm, 0.0)
            in_diag = jnp.logical_and(jnp.logical_and(cc >= i * sub, cc <= rr), use_exact)
            blk = jnp.where(in_diag, sc_scr[h, rows, :][:, :nk], blk)
            o_scr[rows, sl] = inter[rows, :] + jnp.dot(blk.astype(BF16), vb[:nk, sl], preferred_element_type=F32)
        st_scr[h] = st * dec_last[:, sl] + lax.dot_general(vb[:, sl], kdec[:, sl], TN_DIMS,
                                                           preferred_element_type=F32)

    gg = g_ref[...]
    y = _rms(o_scr[...], gnorm_ref[...]) * (gg * _sigmoid(gg))
    ohg_ref[...] = y.astype(BF16)

    @pl.when(c == nc - 1)
    def _():
        for h in range(nh):
            s1_ref[h] = st_scr[h].T


def _hgrn(proj, lb_logits, s0, g_hnorm, *, B, T):
    L = min(CHUNK, T)
    nc = T // L
    nh = s0.shape[1]
    dhh = nh * DH_H
    row = lambda b, c: b * nc + c
    kern = functools.partial(_hgrn_kernel, L=L, nh=nh)
    return pl.pallas_call(
        kern,
        grid=(B, nc),
        in_specs=[
            pl.BlockSpec((L, dhh), lambda b, c: (row(b, c), 4)),
            pl.BlockSpec((L, dhh), lambda b, c: (row(b, c), 5)),
            pl.BlockSpec((L, dhh), lambda b, c: (row(b, c), 6)),
            pl.BlockSpec((L, dhh), lambda b, c: (row(b, c), 7)),
            pl.BlockSpec((2, dhh), lambda b, c: (0, 0)),
            pl.BlockSpec((None, nh, DH_H, DH_H), lambda b, c: (b, 0, 0, 0)),
            pl.BlockSpec((1, dhh), lambda b, c: (0, 0)),
        ],
        out_specs=[
            pl.BlockSpec((L, dhh), lambda b, c: (row(b, c), 0)),
            pl.BlockSpec((None, nh, DH_H, DH_H), lambda b, c: (b, 0, 0, 0)),
        ],
        out_shape=[
            jax.ShapeDtypeStruct((B * T, dhh), BF16),
            jax.ShapeDtypeStruct((B, nh, DH_H, DH_H), F32),
        ],
        scratch_shapes=[
            pltpu.VMEM((nh, DH_H, DH_H), F32),
            pltpu.VMEM((L, dhh), F32),
            pltpu.VMEM((L, dhh), F32),
            pltpu.VMEM((L, dhh), F32),
            pltpu.VMEM((nh, L, LANES), F32),
            pltpu.VMEM((L, dhh), F32),
        ],
        compiler_params=_params(("parallel", "arbitrary")),
        name="hgrn",
    )(proj, proj, proj, proj, lb_logits, s0, g_hnorm)


def _merge_kernel(hmg_ref, ohg_ref, ga_ref, gb_ref, x_ref, wa_ref, wb_ref, wo_ref, x1_ref):
    ya = jnp.dot(hmg_ref[...], wa_ref[...], preferred_element_type=F32)
    yb = jnp.dot(ohg_ref[...], wb_ref[...], preferred_element_type=F32)
    u = _sigmoid(ga_ref[...]) * ya + _sigmoid(gb_ref[...]) * yb
    x1_ref[...] = x_ref[...] + jnp.dot(u.astype(BF16), wo_ref[...], preferred_element_type=F32)


def _merge(hmg, ohg, proj, x2d, w_a, w_b, w_o):
    m, d = x2d.shape
    dm = hmg.shape[1]
    tm = min(m, 256)
    const = lambda i: (0, 0)
    return pl.pallas_call(
        _merge_kernel,
        grid=(m // tm,),
        in_specs=[
            pl.BlockSpec((tm, dm), lambda i: (i, 0)),
            pl.BlockSpec((tm, dm), lambda i: (i, 0)),
            pl.BlockSpec((tm, d), lambda i: (i, 4)),
            pl.BlockSpec((tm, d), lambda i: (i, 5)),
            pl.BlockSpec((tm, d), lambda i: (i, 0)),
            pl.BlockSpec((dm, d), const, pipeline_mode=pl.Buffered(1)),
            pl.BlockSpec((dm, d), const, pipeline_mode=pl.Buffered(1)),
            pl.BlockSpec((d, d), const, pipeline_mode=pl.Buffered(1)),
        ],
        out_specs=pl.BlockSpec((tm, d), lambda i: (i, 0)),
        out_shape=jax.ShapeDtypeStruct((m, d), F32),
        compiler_params=_params(("parallel",)),
        name="merge",
    )(hmg, ohg, proj, proj, x2d, w_a, w_b, w_o)


def _ffn_kernel(x1_ref, gffn_ref, w1_ref, w2_ref, gfin_ref, y_ref, h_scr, acc_scr):
    j = pl.program_id(1)

    @pl.when(j == 0)
    def _():
        h_scr[...] = _rms(x1_ref[...], gffn_ref[...]).astype(BF16)
        acc_scr[...] = jnp.zeros_like(acc_scr)

    t = jnp.maximum(jnp.dot(h_scr[...], w1_ref[...], preferred_element_type=F32), 0.0)
    acc_scr[...] += jnp.dot((t * t).astype(BF16), w2_ref[...], preferred_element_type=F32)

    @pl.when(j == pl.num_programs(1) - 1)
    def _():
        y_ref[...] = _rms(x1_ref[...] + acc_scr[...], gfin_ref[...])


def _ffn(x1, g_ffn, w1, w2, g_final):
    m, d = x1.shape
    dff = w1.shape[1]
    tm = min(m, 512)
    tf = 512
    return pl.pallas_call(
        _ffn_kernel,
        grid=(m // tm, dff // tf),
        in_specs=[
            pl.BlockSpec((tm, d), lambda i, j: (i, 0)),
            pl.BlockSpec((1, d), lambda i, j: (0, 0)),
            pl.BlockSpec((d, tf), lambda i, j: (0, j)),
            pl.BlockSpec((tf, d), lambda i, j: (j, 0)),
            pl.BlockSpec((1, d), lambda i, j: (0, 0)),
        ],
        out_specs=pl.BlockSpec((tm, d), lambda i, j: (i, 0)),
        out_shape=jax.ShapeDtypeStruct((m, d), F32),
        scratch_shapes=[pltpu.VMEM((tm, d), BF16), pltpu.VMEM((tm, d), F32)],
        compiler_params=_params(("parallel", "arbitrary")),
        name="ffn",
    )(x1, g_ffn, w1, w2, g_final)


def _group(x, conv_prev, c0, n0, m0, s0, wts):
    B, T, d = x.shape
    x2d = x.reshape(B * T, d)
    proj, gates = _inproj(x2d, wts["g_mix"], wts["w_main"], wts["w_if"], wts["b_if"])
    hmg, new_conv, c1, n1, m1 = _mlstm(proj, gates, conv_prev, c0, n0, m0.reshape(B, 1, -1),
                                       wts["w_conv"], wts["b_conv"], wts["g_mnorm"], B=B, T=T)
    ohg, s1 = _hgrn(proj, wts["lb_logits"], s0, wts["g_hnorm"], B=B, T=T)
    x1 = _merge(hmg, ohg, proj, x2d, wts["w_a"], wts["w_b"], wts["w_o"])
    y = _ffn(x1, wts["g_ffn"], wts["w_ff1"], wts["w_ff2"], wts["g_final"])
    return (y.reshape(B, T, d), new_conv[None], c1[None], n1[None], m1.reshape(1, B, -1), s1[None])


def kernel(x_prompt, x_sample, cache_mlstm_conv, state_mlstm_C, state_mlstm_n, state_mlstm_m, state_hgrn_S,
           g_mix, w_in, b_if, w_conv, b_conv, g_mnorm, g_hnorm, hgrn_lb_logits, w_branch_a, w_branch_b,
           w_out, g_ffn, w_ff1, w_ff2, g_final):
    depth = w_in.shape[0]
    assert depth == 1, "single-layer trunk only"
    d = x_prompt.shape[-1]
    dm = w_branch_a.shape[1]
    dhh = w_branch_b.shape[1]
    nh_h = dhh // DH_H
    dh_m = dm // H_M
    assert hgrn_lb_logits.shape[0] == depth + 1
    assert 2 * dm == d and dhh == dm

    w = w_in[0]
    n_if = 2 * H_M
    p_if = 4 * dm
    w_main = jnp.concatenate([w[:, :p_if], w[:, p_if + n_if:]], axis=1).astype(BF16)
    w_if = jnp.pad(w[:, p_if:p_if + n_if], ((0, 0), (0, LANES - n_if))).astype(BF16)
    bif = jnp.pad(b_if[0].astype(F32), (0, LANES - n_if)).reshape(1, LANES)
    wts = dict(
        g_mix=g_mix[0].reshape(1, d), w_main=w_main, w_if=w_if, b_if=bif,
        w_conv=w_conv[0], b_conv=b_conv[0].reshape(1, -1), g_mnorm=g_mnorm[0].reshape(1, dm),
        lb_logits=hgrn_lb_logits.astype(F32), g_hnorm=g_hnorm[0].reshape(1, dhh),
        w_a=w_branch_a[0].astype(BF16), w_b=w_branch_b[0].astype(BF16), w_o=w_out[0].astype(BF16),
        g_ffn=g_ffn[0].reshape(1, d), w_ff1=w_ff1[0].astype(BF16), w_ff2=w_ff2[0].astype(BF16),
        g_final=g_final.reshape(1, d),
    )

    bp = x_prompt.shape[0]
    zeros = lambda *s: jnp.zeros(s, F32)
    outs_p = _group(x_prompt, zeros(bp, CONV_W - 1, 2 * dm), zeros(bp, H_M, dh_m, dh_m), zeros(bp, H_M, dh_m),
                    zeros(bp, H_M), zeros(bp, nh_h, DH_H, DH_H), wts)
    outs_s = _group(x_sample, cache_mlstm_conv[0], state_mlstm_C[0].astype(F32), state_mlstm_n[0].astype(F32),
                    state_mlstm_m[0].astype(F32), state_hgrn_S[0].astype(F32), wts)
    return (outs_p[0], outs_s[0]) + outs_p[1:] + outs_s[1:]
```

```python
import functools

import jax
import jax.numpy as jnp
from jax import lax
from jax.experimental import pallas as pl
from jax.experimental.pallas import tpu as pltpu

F32 = jnp.float32
BF16 = jnp.bfloat16
HIGHEST = lax.Precision.HIGHEST

EPS = 1e-6
CHUNK = 64
MLSTM_CHUNK = 128
CONV_W = 4
H_M = 4
DH_H = 128
SUB = 32
SAFE_SPAN = 60.0
LANES = 128
CONV_PAD = 8
VMEM_LIMIT = 56 * 1024 * 1024

NT_DIMS = (((1,), (1,)), ((), ()))
TN_DIMS = (((0,), (0,)), ((), ()))


def _rms(x, g):
    return x * lax.rsqrt(jnp.mean(x * x, axis=-1, keepdims=True) + EPS) * g


def _sigmoid(x):
    return 1.0 / (1.0 + jnp.exp(-x))


def _params(sem):
    return pltpu.CompilerParams(dimension_semantics=sem, vmem_limit_bytes=VMEM_LIMIT)


def _inproj_kernel(x_ref, g_ref, wlo_ref, whi_ref, wif_ref, bif_ref, proj_ref, gates_ref, h_scr, *, n_lo):
    j = pl.program_id(1)

    @pl.when(j == 0)
    def _():
        hb = _rms(x_ref[...], g_ref[...]).astype(BF16)
        h_scr[...] = hb
        gates_ref[...] = jnp.dot(hb, wif_ref[...], preferred_element_type=F32) + bif_ref[...]

    @pl.when(j < n_lo)
    def _():
        proj_ref[...] = jnp.dot(h_scr[...], wlo_ref[...], preferred_element_type=F32)

    @pl.when(j >= n_lo)
    def _():
        proj_ref[...] = jnp.dot(h_scr[...], whi_ref[...], preferred_element_type=F32)


def _inproj(x2d, g_mix, w_lo, w_hi, w_if, b_if):
    m, d = x2d.shape
    tm = min(m, 1024)
    tn = 1024
    n_lo = w_lo.shape[1] // tn
    n_hi = w_hi.shape[1] // tn
    n = (n_lo + n_hi) * tn
    return pl.pallas_call(
        functools.partial(_inproj_kernel, n_lo=n_lo),
        grid=(m // tm, n_lo + n_hi),
        in_specs=[
            pl.BlockSpec((tm, d), lambda i, j: (i, 0)),
            pl.BlockSpec((1, d), lambda i, j: (0, 0)),
            pl.BlockSpec((d, tn), lambda i, j: (0, jnp.minimum(j, n_lo - 1))),
            pl.BlockSpec((d, tn), lambda i, j: (0, jnp.maximum(j - n_lo, 0))),
            pl.BlockSpec((d, LANES), lambda i, j: (0, 0)),
            pl.BlockSpec((1, LANES), lambda i, j: (0, 0)),
        ],
        out_specs=[
            pl.BlockSpec((tm, tn), lambda i, j: (i, j)),
            pl.BlockSpec((tm, LANES), lambda i, j: (i, 0)),
        ],
        out_shape=[jax.ShapeDtypeStruct((m, n), F32), jax.ShapeDtypeStruct((m, LANES), F32)],
        scratch_shapes=[pltpu.VMEM((tm, d), BF16)],
        compiler_params=_params(("parallel", "arbitrary")),
        name="inproj",
    )(x2d, g_mix, w_lo, w_hi, w_if, b_if)


def _mlstm_kernel(qk_ref, v_ref, o_ref, gates_ref, convprev_ref, c0_ref, n0_ref, m0_ref,
                  wconv_ref, bconv_ref, gnorm_ref,
                  hmg_ref, newconv_ref, c1_ref, n1_ref, m1_ref,
                  xpad, c_scr, n_scr, m_scr, *, L, dm, dh):
    c = pl.program_id(1)
    nc = pl.num_programs(1)
    hist0 = CONV_PAD - (CONV_W - 1)

    @pl.when(c == 0)
    def _():
        xpad[hist0:CONV_PAD, :] = convprev_ref[...]
        c_scr[...] = c0_ref[...]
        n_scr[...] = n0_ref[...]
        m_scr[...] = m0_ref[...]

    xpad[CONV_PAD:CONV_PAD + L, :] = qk_ref[...]
    conv = bconv_ref[...]
    for i in range(CONV_W):
        conv = conv + xpad[hist0 + i:hist0 + i + L, :] * wconv_ref[i:i + 1, :]
    hist = xpad[CONV_PAD + L - (CONV_W - 1):CONV_PAD + L, :]
    xpad[hist0:CONV_PAD, :] = hist
    qk = conv * _sigmoid(conv)
    q = qk[:, :dm]
    k = qk[:, dm:] * (dh ** -0.5)

    g = gates_ref[...]
    col = lax.broadcasted_iota(jnp.int32, g.shape, 1)
    lf = jnp.minimum(g, 0.0) - jnp.log1p(jnp.exp(-jnp.abs(g)))
    x_cols = jnp.where(col < H_M, g, lf)
    rr = lax.broadcasted_iota(jnp.int32, (L, L), 0)
    cc = lax.broadcasted_iota(jnp.int32, (L, L), 1)
    causal = rr >= cc
    tril = causal.astype(F32)
    cum = jnp.dot(tril, x_cols, preferred_element_type=F32, precision=HIGHEST)
    xc = jnp.where(col < H_M, g, cum)
    eye = (lax.broadcasted_iota(jnp.int32, (8, LANES), 0)
           == lax.broadcasted_iota(jnp.int32, (8, LANES), 1)).astype(F32)
    rows = lax.dot_general(eye, xc, NT_DIMS, preferred_element_type=F32, precision=HIGHEST)

    vv = v_ref[...]
    oo = o_ref[...]
    m_all = m_scr[...]
    for h in range(H_M):
        sl = slice(h * dh, (h + 1) * dh)
        qf = q[:, sl]
        kf = k[:, sl]
        vf = vv[:, sl]
        qb = qf.astype(BF16)
        kb = kf.astype(BF16)
        bcol = cum[:, H_M + h:H_M + h + 1]
        igcol = g[:, h:h + 1]
        brow = rows[H_M + h:H_M + h + 1, :]
        igrow = rows[h:h + 1, :]
        m_prev = m_all[:, h:h + 1]
        dlog = jnp.where(causal, bcol - brow + igrow, -jnp.inf)
        m_inter = bcol + m_prev
        m_t = jnp.maximum(m_inter, jnp.max(dlog, axis=-1, keepdims=True))
        s = lax.dot_general(qb, kb, NT_DIMS, preferred_element_type=F32) * jnp.exp(dlog - m_t)
        sc_inter = jnp.exp(m_inter - m_t)
        c_old = c_scr[h]
        n_old = n_scr[h:h + 1, :]
        num = (sc_inter * lax.dot_general(qb, c_old.astype(BF16), NT_DIMS, preferred_element_type=F32)
               + jnp.dot(s.astype(BF16), vf.astype(BF16), preferred_element_type=F32))
        den = sc_inter * jnp.sum(qf * n_old, axis=-1, keepdims=True) + jnp.sum(s, axis=-1, keepdims=True)
        hout = num / jnp.maximum(jnp.abs(den), jnp.exp(-m_t))
        m_new = m_t[L - 1:L, :]
        b_last = bcol[L - 1:L, :]
        decay = jnp.exp(b_last + m_prev - m_new)
        wgt = jnp.exp(b_last - bcol + igcol - m_new)
        c_scr[h] = decay * c_old + lax.dot_general((vf * wgt).astype(BF16), kb, TN_DIMS,
                                                   preferred_element_type=F32)
        n_scr[h:h + 1, :] = decay * n_old + jnp.sum(wgt * kf, axis=0, keepdims=True)
        m_scr[:, h:h + 1] = m_new
        y = _rms(hout, gnorm_ref[:, sl]) * _sigmoid(oo[:, sl])
        hmg_ref[:, sl] = y.astype(BF16)

    @pl.when(c == nc - 1)
    def _():
        newconv_ref[...] = hist
        c1_ref[...] = c_scr[...]
        n1_ref[...] = n_scr[...]
        m1_ref[...] = m_scr[...]


def _mlstm(proj, gates, conv_prev, c0, n0, m0, w_conv, b_conv, g_mnorm, *, B, T):
    L = MLSTM_CHUNK if T % MLSTM_CHUNK == 0 else min(CHUNK, T)
    nc = T // L
    dm = c0.shape[1] * c0.shape[2]
    dh = c0.shape[2]
    row = lambda b, c: b * nc + c
    kern = functools.partial(_mlstm_kernel, L=L, dm=dm, dh=dh)
    return pl.pallas_call(
        kern,
        grid=(B, nc),
        in_specs=[
            pl.BlockSpec((L, 2 * dm), lambda b, c: (row(b, c), 0)),
            pl.BlockSpec((L, dm), lambda b, c: (row(b, c), 2)),
            pl.BlockSpec((L, dm), lambda b, c: (row(b, c), 3)),
            pl.BlockSpec((L, LANES), lambda b, c: (row(b, c), 0)),
            pl.BlockSpec((None, CONV_W - 1, 2 * dm), lambda b, c: (b, 0, 0)),
            pl.BlockSpec((None, H_M, dh, dh), lambda b, c: (b, 0, 0, 0)),
            pl.BlockSpec((None, H_M, dh), lambda b, c: (b, 0, 0)),
            pl.BlockSpec((None, 1, H_M), lambda b, c: (b, 0, 0)),
            pl.BlockSpec((CONV_W, 2 * dm), lambda b, c: (0, 0)),
            pl.BlockSpec((1, 2 * dm), lambda b, c: (0, 0)),
            pl.BlockSpec((1, dm), lambda b, c: (0, 0)),
        ],
        out_specs=[
            pl.BlockSpec((L, dm), lambda b, c: (row(b, c), 0)),
            pl.BlockSpec((None, CONV_W - 1, 2 * dm), lambda b, c: (b, 0, 0)),
            pl.BlockSpec((None, H_M, dh, dh), lambda b, c: (b, 0, 0, 0)),
            pl.BlockSpec((None, H_M, dh), lambda b, c: (b, 0, 0)),
            pl.BlockSpec((None, 1, H_M), lambda b, c: (b, 0, 0)),
        ],
        out_shape=[
            jax.ShapeDtypeStruct((B * T, dm), BF16),
            jax.ShapeDtypeStruct((B, CONV_W - 1, 2 * dm), F32),
            jax.ShapeDtypeStruct((B, H_M, dh, dh), F32),
            jax.ShapeDtypeStruct((B, H_M, dh), F32),
            jax.ShapeDtypeStruct((B, 1, H_M), F32),
        ],
        scratch_shapes=[
            pltpu.VMEM((CONV_PAD + L, 2 * dm), F32),
            pltpu.VMEM((H_M, dh, dh), F32),
            pltpu.VMEM((H_M, dh), F32),
            pltpu.VMEM((1, H_M), F32),
        ],
        compiler_params=_params(("parallel", "arbitrary")),
        name="mlstm",
    )(proj, proj, proj, gates, conv_prev, c0, n0, m0, w_conv, b_conv, g_mnorm)


def _hgrn_kernel(f_ref, i_ref, q_ref, g_ref, lbl_ref, s0_ref, gnorm_ref,
                 ohg_ref, s1_ref,
                 st_scr, a_scr, q_scr, k_scr, sc_scr, o_scr, *, L, nh):
    c = pl.program_id(1)
    nc = pl.num_programs(1)
    sub = min(SUB, L)
    nb = L // sub

    @pl.when(c == 0)
    def _():
        sc_scr[...] = jnp.zeros_like(sc_scr)
        for h in range(nh):
            st_scr[h] = s0_ref[h].T

    lbl = lbl_ref[...]
    mx = jnp.max(lbl, axis=0, keepdims=True)
    ex = jnp.exp(lbl - mx)
    lb = ex[0:1, :] / jnp.sum(ex, axis=0, keepdims=True)

    f = lb + (1.0 - lb) * _sigmoid(f_ref[...])
    lf = jnp.log(f)
    kk = 1.0 - f
    qq = q_ref[...]
    vv = i_ref[...]

    ridx = lax.broadcasted_iota(jnp.int32, lf.shape, 0)
    a = lf
    sh = 1
    while sh < L:
        a = a + jnp.where(ridx >= sh, pltpu.roll(a, sh, 0), 0.0)
        sh *= 2

    refs = [jnp.zeros_like(a[0:1, :])] + [a[i * sub - 1:i * sub, :] for i in range(1, nb)]
    span = refs[0] - a[sub - 1:sub, :]
    for i in range(1, nb):
        span = jnp.maximum(span, refs[i] - a[(i + 1) * sub - 1:(i + 1) * sub, :])
    use_exact = jnp.logical_not(jnp.max(span) <= SAFE_SPAN)

    @pl.when(use_exact)
    def _():
        a_scr[...] = a
        q_scr[...] = qq
        k_scr[...] = kk
        lane = lax.broadcasted_iota(jnp.int32, (sub, LANES), 1)

        def diag_body(s, carry):
            r0 = pl.multiple_of((s // sub) * sub, sub)
            a_blk = a_scr[pl.ds(r0, sub), :]
            q_blk = q_scr[pl.ds(r0, sub), :]
            a_s = a_scr[pl.ds(s, 1), :]
            k_s = k_scr[pl.ds(s, 1), :]
            p = q_blk * jnp.exp(jnp.minimum(a_blk - a_s, 0.0)) * k_s
            for h in range(nh):
                colv = jnp.sum(p[:, h * DH_H:(h + 1) * DH_H], axis=-1, keepdims=True)
                old = sc_scr[h, pl.ds(r0, sub), :]
                sc_scr[h, pl.ds(r0, sub), :] = jnp.where(lane == s, colv, old)
            return carry

        lax.fori_loop(0, L, diag_body, 0)

    a_last = a[L - 1:L, :]
    qa = (qq * jnp.exp(a)).astype(BF16)
    kdec = (kk * jnp.exp(a_last - a)).astype(BF16)
    vb = vv.astype(BF16)
    dec_last = jnp.exp(a_last)

    qs = []
    ks = []
    for i in range(nb):
        rows = slice(i * sub, (i + 1) * sub)
        nk = (i + 1) * sub
        qs.append(qa[rows, :] if i == 0 else (qq[rows, :] * jnp.exp(a[rows, :] - refs[i])).astype(BF16))
        ks.append((kk[:nk, :] * jnp.exp(jnp.minimum(refs[i] - a[:nk, :], SAFE_SPAN))).astype(BF16))

    for h in range(nh):
        sl = slice(h * DH_H, (h + 1) * DH_H)
        st = st_scr[h]
        inter = lax.dot_general(qa[:, sl], st.astype(BF16), NT_DIMS, preferred_element_type=F32)
        for i in range(nb):
            rows = slice(i * sub, (i + 1) * sub)
            nk = (i + 1) * sub
            rr = lax.broadcasted_iota(jnp.int32, (sub, nk), 0) + i * sub
            cc = lax.broadcasted_iota(jnp.int32, (sub, nk), 1)
            mm = lax.dot_general(qs[i][:, sl], ks[i][:, sl], NT_DIMS, preferred_element_type=F32)
            blk = jnp.where(cc <= rr, mm, 0.0)
            in_diag = jnp.logical_and(jnp.logical_and(cc >= i * sub, cc <= rr), use_exact)
            blk = jnp.where(in_diag, sc_scr[h, rows, :][:, :nk], blk)
            o_scr[rows, sl] = inter[rows, :] + jnp.dot(blk.astype(BF16), vb[:nk, sl], preferred_element_type=F32)
        st_scr[h] = st * dec_last[:, sl] + lax.dot_general(vb[:, sl], kdec[:, sl], TN_DIMS,
                                                           preferred_element_type=F32)

    gg = g_ref[...]
    y = _rms(o_scr[...], gnorm_ref[...]) * (gg * _sigmoid(gg))
    ohg_ref[...] = y.astype(BF16)

    @pl.when(c == nc - 1)
    def _():
        for h in range(nh):
            s1_ref[h] = st_scr[h].T


def _hgrn(proj, lb_logits, s0, g_hnorm, *, B, T):
    L = min(CHUNK, T)
    nc = T // L
    nh = s0.shape[1]
    dhh = nh * DH_H
    row = lambda b, c: b * nc + c
    kern = functools.partial(_hgrn_kernel, L=L, nh=nh)
    return pl.pallas_call(
        kern,
        grid=(B, nc),
        in_specs=[
            pl.BlockSpec((L, dhh), lambda b, c: (row(b, c), 4)),
            pl.BlockSpec((L, dhh), lambda b, c: (row(b, c), 5)),
            pl.BlockSpec((L, dhh), lambda b, c: (row(b, c), 6)),
            pl.BlockSpec((L, dhh), lambda b, c: (row(b, c), 7)),
            pl.BlockSpec((2, dhh), lambda b, c: (0, 0)),
            pl.BlockSpec((None, nh, DH_H, DH_H), lambda b, c: (b, 0, 0, 0)),
            pl.BlockSpec((1, dhh), lambda b, c: (0, 0)),
        ],
        out_specs=[
            pl.BlockSpec((L, dhh), lambda b, c: (row(b, c), 0)),
            pl.BlockSpec((None, nh, DH_H, DH_H), lambda b, c: (b, 0, 0, 0)),
        ],
        out_shape=[
            jax.ShapeDtypeStruct((B * T, dhh), BF16),
            jax.ShapeDtypeStruct((B, nh, DH_H, DH_H), F32),
        ],
        scratch_shapes=[
            pltpu.VMEM((nh, DH_H, DH_H), F32),
            pltpu.VMEM((L, dhh), F32),
            pltpu.VMEM((L, dhh), F32),
            pltpu.VMEM((L, dhh), F32),
            pltpu.VMEM((nh, L, LANES), F32),
            pltpu.VMEM((L, dhh), F32),
        ],
        compiler_params=_params(("parallel", "arbitrary")),
        name="hgrn",
    )(proj, proj, proj, proj, lb_logits, s0, g_hnorm)


def _merge_kernel(hmg_ref, ohg_ref, ga_ref, gb_ref, x_ref, wa_ref, wb_ref, wo_ref, x1_ref):
    ya = jnp.dot(hmg_ref[...], wa_ref[...], preferred_element_type=F32)
    yb = jnp.dot(ohg_ref[...], wb_ref[...], preferred_element_type=F32)
    u = _sigmoid(ga_ref[...]) * ya + _sigmoid(gb_ref[...]) * yb
    x1_ref[...] = x_ref[...] + jnp.dot(u.astype(BF16), wo_ref[...], preferred_element_type=F32)


def _merge(hmg, ohg, proj, x2d, w_a, w_b, w_o):
    m, d = x2d.shape
    dm = hmg.shape[1]
    tm = min(m, 256)
    const = lambda i: (0, 0)
    return pl.pallas_call(
        _merge_kernel,
        grid=(m // tm,),
        in_specs=[
            pl.BlockSpec((tm, dm), lambda i: (i, 0)),
            pl.BlockSpec((tm, dm), lambda i: (i, 0)),
            pl.BlockSpec((tm, d), lambda i: (i, 4)),
            pl.BlockSpec((tm, d), lambda i: (i, 5)),
            pl.BlockSpec((tm, d), lambda i: (i, 0)),
            pl.BlockSpec((dm, d), const, pipeline_mode=pl.Buffered(1)),
            pl.BlockSpec((dm, d), const, pipeline_mode=pl.Buffered(1)),
            pl.BlockSpec((d, d), const, pipeline_mode=pl.Buffered(1)),
        ],
        out_specs=pl.BlockSpec((tm, d), lambda i: (i, 0)),
        out_shape=jax.ShapeDtypeStruct((m, d), F32),
        compiler_params=_params(("parallel",)),
        name="merge",
    )(hmg, ohg, proj, proj, x2d, w_a, w_b, w_o)


def _ffn_kernel(x1_ref, gffn_ref, w1_ref, w2_ref, gfin_ref, y_ref, h_scr, acc_scr):
    j = pl.program_id(1)

    @pl.when(j == 0)
    def _():
        h_scr[...] = _rms(x1_ref[...], gffn_ref[...]).astype(BF16)
        acc_scr[...] = jnp.zeros_like(acc_scr)

    t = jnp.maximum(jnp.dot(h_scr[...], w1_ref[...], preferred_element_type=F32), 0.0)
    acc_scr[...] += jnp.dot((t * t).astype(BF16), w2_ref[...], preferred_element_type=F32)

    @pl.when(j == pl.num_programs(1) - 1)
    def _():
        y_ref[...] = _rms(x1_ref[...] + acc_scr[...], gfin_ref[...])


def _ffn(x1, g_ffn, w1, w2, g_final):
    m, d = x1.shape
    dff = w1.shape[1]
    tm = min(m, 512)
    tf = 512
    return pl.pallas_call(
        _ffn_kernel,
        grid=(m // tm, dff // tf),
        in_specs=[
            pl.BlockSpec((tm, d), lambda i, j: (i, 0)),
            pl.BlockSpec((1, d), lambda i, j: (0, 0)),
            pl.BlockSpec((d, tf), lambda i, j: (0, j)),
            pl.BlockSpec((tf, d), lambda i, j: (j, 0)),
            pl.BlockSpec((1, d), lambda i, j: (0, 0)),
        ],
        out_specs=pl.BlockSpec((tm, d), lambda i, j: (i, 0)),
        out_shape=jax.ShapeDtypeStruct((m, d), F32),
        scratch_shapes=[pltpu.VMEM((tm, d), BF16), pltpu.VMEM((tm, d), F32)],
        compiler_params=_params(("parallel", "arbitrary")),
        name="ffn",
    )(x1, g_ffn, w1, w2, g_final)


def _group(x, conv_prev, c0, n0, m0, s0, wts):
    B, T, d = x.shape
    x2d = x.reshape(B * T, d)
    proj, gates = _inproj(x2d, wts["g_mix"], wts["w_lo"], wts["w_hi"], wts["w_if"], wts["b_if"])
    hmg, new_conv, c1, n1, m1 = _mlstm(proj, gates, conv_prev, c0, n0, m0.reshape(B, 1, -1),
                                       wts["w_conv"], wts["b_conv"], wts["g_mnorm"], B=B, T=T)
    ohg, s1 = _hgrn(proj, wts["lb_logits"], s0, wts["g_hnorm"], B=B, T=T)
    x1 = _merge(hmg, ohg, proj, x2d, wts["w_a"], wts["w_b"], wts["w_o"])
    y = _ffn(x1, wts["g_ffn"], wts["w_ff1"], wts["w_ff2"], wts["g_final"])
    return (y.reshape(B, T, d), new_conv[None], c1[None], n1[None], m1.reshape(1, B, -1), s1[None])


def kernel(x_prompt, x_sample, cache_mlstm_conv, state_mlstm_C, state_mlstm_n, state_mlstm_m, state_hgrn_S,
           g_mix, w_in, b_if, w_conv, b_conv, g_mnorm, g_hnorm, hgrn_lb_logits, w_branch_a, w_branch_b,
           w_out, g_ffn, w_ff1, w_ff2, g_final):
    depth = w_in.shape[0]
    assert depth == 1, "single-layer trunk only"
    d = x_prompt.shape[-1]
    dm = w_branch_a.shape[1]
    dhh = w_branch_b.shape[1]
    nh_h = dhh // DH_H
    dh_m = dm // H_M
    assert hgrn_lb_logits.shape[0] == depth + 1
    assert 2 * dm == d and dhh == dm

    w = w_in[0]
    n_if = 2 * H_M
    p_if = 4 * dm
    w_lo = w[:, :p_if].astype(BF16)
    w_hi = w[:, p_if + n_if:].astype(BF16)
    w_if = jnp.pad(w[:, p_if:p_if + n_if], ((0, 0), (0, LANES - n_if))).astype(BF16)
    bif = jnp.pad(b_if[0].astype(F32), (0, LANES - n_if)).reshape(1, LANES)
    wts = dict(
        g_mix=g_mix[0].reshape(1, d), w_lo=w_lo, w_hi=w_hi, w_if=w_if, b_if=bif,
        w_conv=w_conv[0], b_conv=b_conv[0].reshape(1, -1), g_mnorm=g_mnorm[0].reshape(1, dm),
        lb_logits=hgrn_lb_logits.astype(F32), g_hnorm=g_hnorm[0].reshape(1, dhh),
        w_a=w_branch_a[0].astype(BF16), w_b=w_branch_b[0].astype(BF16), w_o=w_out[0].astype(BF16),
        g_ffn=g_ffn[0].reshape(1, d), w_ff1=w_ff1[0].astype(BF16), w_ff2=w_ff2[0].astype(BF16),
        g_final=g_final.reshape(1, d),
    )

    bp = x_prompt.shape[0]
    zeros = lambda *s: jnp.zeros(s, F32)
    outs_p = _group(x_prompt, zeros(bp, CONV_W - 1, 2 * dm), zeros(bp, H_M, dh_m, dh_m), zeros(bp, H_M, dh_m),
                    zeros(bp, H_M), zeros(bp, nh_h, DH_H, DH_H), wts)
    outs_s = _group(x_sample, cache_mlstm_conv[0], state_mlstm_C[0].astype(F32), state_mlstm_n[0].astype(F32),
                    state_mlstm_m[0].astype(F32), state_hgrn_S[0].astype(F32), wts)
    return (outs_p[0], outs_s[0]) + outs_p[1:] + outs_s[1:]
```

```python
import functools

import jax
import jax.numpy as jnp
from jax import lax
from jax.experimental import pallas as pl
from jax.experimental.pallas import tpu as pltpu

F32 = jnp.float32
BF16 = jnp.bfloat16
HIGHEST = lax.Precision.HIGHEST

EPS = 1e-6
CHUNK = 64
MLSTM_CHUNK = 128
CONV_W = 4
H_M = 4
DH_H = 128
SUB = 32
HGRN_CPS = 4
SAFE_SPAN = 60.0
LANES = 128
CONV_PAD = 8
VMEM_LIMIT = 56 * 1024 * 1024

NT_DIMS = (((1,), (1,)), ((), ()))
TN_DIMS = (((0,), (0,)), ((), ()))


def _rms(x, g):
    return x * lax.rsqrt(jnp.mean(x * x, axis=-1, keepdims=True) + EPS) * g


def _sigmoid(x):
    return 1.0 / (1.0 + jnp.exp(-x))


def _params(sem):
    return pltpu.CompilerParams(dimension_semantics=sem, vmem_limit_bytes=VMEM_LIMIT)


def _inproj_kernel(x_ref, g_ref, wlo_ref, whi_ref, wif_ref, bif_ref, proj_ref, gates_ref, h_scr, *, n_lo):
    j = pl.program_id(1)

    @pl.when(j == 0)
    def _():
        hb = _rms(x_ref[...], g_ref[...]).astype(BF16)
        h_scr[...] = hb
        gates_ref[...] = jnp.dot(hb, wif_ref[...], preferred_element_type=F32) + bif_ref[...]

    @pl.when(j < n_lo)
    def _():
        proj_ref[...] = jnp.dot(h_scr[...], wlo_ref[...], preferred_element_type=F32)

    @pl.when(j >= n_lo)
    def _():
        proj_ref[...] = jnp.dot(h_scr[...], whi_ref[...], preferred_element_type=F32)


def _inproj(x2d, g_mix, w_lo, w_hi, w_if, b_if):
    m, d = x2d.shape
    tm = min(m, 1024)
    tn = 1024
    n_lo = w_lo.shape[1] // tn
    n_hi = w_hi.shape[1] // tn
    n = (n_lo + n_hi) * tn
    return pl.pallas_call(
        functools.partial(_inproj_kernel, n_lo=n_lo),
        grid=(m // tm, n_lo + n_hi),
        in_specs=[
            pl.BlockSpec((tm, d), lambda i, j: (i, 0)),
            pl.BlockSpec((1, d), lambda i, j: (0, 0)),
            pl.BlockSpec((d, tn), lambda i, j: (0, jnp.minimum(j, n_lo - 1))),
            pl.BlockSpec((d, tn), lambda i, j: (0, jnp.maximum(j - n_lo, 0))),
            pl.BlockSpec((d, LANES), lambda i, j: (0, 0)),
            pl.BlockSpec((1, LANES), lambda i, j: (0, 0)),
        ],
        out_specs=[
            pl.BlockSpec((tm, tn), lambda i, j: (i, j)),
            pl.BlockSpec((tm, LANES), lambda i, j: (i, 0)),
        ],
        out_shape=[jax.ShapeDtypeStruct((m, n), F32), jax.ShapeDtypeStruct((m, LANES), F32)],
        scratch_shapes=[pltpu.VMEM((tm, d), BF16)],
        compiler_params=_params(("parallel", "arbitrary")),
        name="inproj",
    )(x2d, g_mix, w_lo, w_hi, w_if, b_if)


def _mlstm_kernel(qk_ref, v_ref, o_ref, gates_ref, convprev_ref, c0_ref, n0_ref, m0_ref,
                  wconv_ref, bconv_ref, gnorm_ref,
                  hmg_ref, newconv_ref, c1_ref, n1_ref, m1_ref,
                  xpad, c_scr, n_scr, m_scr, *, L, dm, dh):
    c = pl.program_id(1)
    nc = pl.num_programs(1)
    hist0 = CONV_PAD - (CONV_W - 1)

    @pl.when(c == 0)
    def _():
        xpad[hist0:CONV_PAD, :] = convprev_ref[...]
        c_scr[...] = c0_ref[...]
        n_scr[...] = n0_ref[...]
        m_scr[...] = m0_ref[...]

    xpad[CONV_PAD:CONV_PAD + L, :] = qk_ref[...]
    conv = bconv_ref[...]
    for i in range(CONV_W):
        conv = conv + xpad[hist0 + i:hist0 + i + L, :] * wconv_ref[i:i + 1, :]
    hist = xpad[CONV_PAD + L - (CONV_W - 1):CONV_PAD + L, :]
    xpad[hist0:CONV_PAD, :] = hist
    qk = conv * _sigmoid(conv)
    q = qk[:, :dm]
    k = qk[:, dm:] * (dh ** -0.5)

    g = gates_ref[...]
    col = lax.broadcasted_iota(jnp.int32, g.shape, 1)
    lf = jnp.minimum(g, 0.0) - jnp.log1p(jnp.exp(-jnp.abs(g)))
    x_cols = jnp.where(col < H_M, g, lf)
    rr = lax.broadcasted_iota(jnp.int32, (L, L), 0)
    cc = lax.broadcasted_iota(jnp.int32, (L, L), 1)
    causal = rr >= cc
    tril = causal.astype(F32)
    cum = jnp.dot(tril, x_cols, preferred_element_type=F32, precision=HIGHEST)
    xc = jnp.where(col < H_M, g, cum)
    eye = (lax.broadcasted_iota(jnp.int32, (8, LANES), 0)
           == lax.broadcasted_iota(jnp.int32, (8, LANES), 1)).astype(F32)
    rows = lax.dot_general(eye, xc, NT_DIMS, preferred_element_type=F32, precision=HIGHEST)

    vv = v_ref[...]
    oo = o_ref[...]
    m_all = m_scr[...]
    for h in range(H_M):
        sl = slice(h * dh, (h + 1) * dh)
        qf = q[:, sl]
        kf = k[:, sl]
        vf = vv[:, sl]
        qb = qf.astype(BF16)
        kb = kf.astype(BF16)
        bcol = cum[:, H_M + h:H_M + h + 1]
        igcol = g[:, h:h + 1]
        brow = rows[H_M + h:H_M + h + 1, :]
        igrow = rows[h:h + 1, :]
        m_prev = m_all[:, h:h + 1]
        dlog = jnp.where(causal, bcol - brow + igrow, -jnp.inf)
        m_inter = bcol + m_prev
        m_t = jnp.maximum(m_inter, jnp.max(dlog, axis=-1, keepdims=True))
        s = lax.dot_general(qb, kb, NT_DIMS, preferred_element_type=F32) * jnp.exp(dlog - m_t)
        sc_inter = jnp.exp(m_inter - m_t)
        c_old = c_scr[h]
        n_old = n_scr[h:h + 1, :]
        num = (sc_inter * lax.dot_general(qb, c_old.astype(BF16), NT_DIMS, preferred_element_type=F32)
               + jnp.dot(s.astype(BF16), vf.astype(BF16), preferred_element_type=F32))
        den = sc_inter * jnp.sum(qf * n_old, axis=-1, keepdims=True) + jnp.sum(s, axis=-1, keepdims=True)
        hout = num / jnp.maximum(jnp.abs(den), jnp.exp(-m_t))
        m_new = m_t[L - 1:L, :]
        b_last = bcol[L - 1:L, :]
        decay = jnp.exp(b_last + m_prev - m_new)
        wgt = jnp.exp(b_last - bcol + igcol - m_new)
        c_scr[h] = decay * c_old + lax.dot_general((vf * wgt).astype(BF16), kb, TN_DIMS,
                                                   preferred_element_type=F32)
        n_scr[h:h + 1, :] = decay * n_old + jnp.sum(wgt * kf, axis=0, keepdims=True)
        m_scr[:, h:h + 1] = m_new
        y = _rms(hout, gnorm_ref[:, sl]) * _sigmoid(oo[:, sl])
        hmg_ref[:, sl] = y.astype(BF16)

    @pl.when(c == nc - 1)
    def _():
        newconv_ref[...] = hist
        c1_ref[...] = c_scr[...]
        n1_ref[...] = n_scr[...]
        m1_ref[...] = m_scr[...]


def _mlstm(proj, gates, conv_prev, c0, n0, m0, w_conv, b_conv, g_mnorm, *, B, T):
    L = MLSTM_CHUNK if T % MLSTM_CHUNK == 0 else min(CHUNK, T)
    nc = T // L
    dm = c0.shape[1] * c0.shape[2]
    dh = c0.shape[2]
    row = lambda b, c: b * nc + c
    kern = functools.partial(_mlstm_kernel, L=L, dm=dm, dh=dh)
    return pl.pallas_call(
        kern,
        grid=(B, nc),
        in_specs=[
            pl.BlockSpec((L, 2 * dm), lambda b, c: (row(b, c), 0)),
            pl.BlockSpec((L, dm), lambda b, c: (row(b, c), 2)),
            pl.BlockSpec((L, dm), lambda b, c: (row(b, c), 3)),
            pl.BlockSpec((L, LANES), lambda b, c: (row(b, c), 0)),
            pl.BlockSpec((None, CONV_W - 1, 2 * dm), lambda b, c: (b, 0, 0)),
            pl.BlockSpec((None, H_M, dh, dh), lambda b, c: (b, 0, 0, 0)),
            pl.BlockSpec((None, H_M, dh), lambda b, c: (b, 0, 0)),
            pl.BlockSpec((None, 1, H_M), lambda b, c: (b, 0, 0)),
            pl.BlockSpec((CONV_W, 2 * dm), lambda b, c: (0, 0)),
            pl.BlockSpec((1, 2 * dm), lambda b, c: (0, 0)),
            pl.BlockSpec((1, dm), lambda b, c: (0, 0)),
        ],
        out_specs=[
            pl.BlockSpec((L, dm), lambda b, c: (row(b, c), 0)),
            pl.BlockSpec((None, CONV_W - 1, 2 * dm), lambda b, c: (b, 0, 0)),
            pl.BlockSpec((None, H_M, dh, dh), lambda b, c: (b, 0, 0, 0)),
            pl.BlockSpec((None, H_M, dh), lambda b, c: (b, 0, 0)),
            pl.BlockSpec((None, 1, H_M), lambda b, c: (b, 0, 0)),
        ],
        out_shape=[
            jax.ShapeDtypeStruct((B * T, dm), BF16),
            jax.ShapeDtypeStruct((B, CONV_W - 1, 2 * dm), F32),
            jax.ShapeDtypeStruct((B, H_M, dh, dh), F32),
            jax.ShapeDtypeStruct((B, H_M, dh), F32),
            jax.ShapeDtypeStruct((B, 1, H_M), F32),
        ],
        scratch_shapes=[
            pltpu.VMEM((CONV_PAD + L, 2 * dm), F32),
            pltpu.VMEM((H_M, dh, dh), F32),
            pltpu.VMEM((H_M, dh), F32),
            pltpu.VMEM((1, H_M), F32),
        ],
        compiler_params=_params(("parallel", "arbitrary")),
        name="mlstm",
    )(proj, proj, proj, gates, conv_prev, c0, n0, m0, w_conv, b_conv, g_mnorm)


def _hgrn_factors(qq, kk, a, qa, sub, nb):
    L = a.shape[0]
    refs = [jnp.zeros_like(a[0:1, :])] + [a[i * sub - 1:i * sub, :] for i in range(1, nb)]
    qs = []
    ks = []
    for i in range(nb):
        rows = slice(i * sub, (i + 1) * sub)
        nk = (i + 1) * sub
        q_i = qa[rows, :] if i == 0 else (qq[rows, :] * jnp.exp(a[rows, :] - refs[i])).astype(BF16)
        k_i = (kk[:nk, :] * jnp.exp(jnp.minimum(refs[i] - a[:nk, :], SAFE_SPAN))).astype(BF16)
        zq = lambda n: jnp.zeros((n, q_i.shape[1]), BF16)
        qs.append(jnp.concatenate([z for z in (zq(i * sub), q_i, zq(L - nk)) if z.shape[0]], axis=0))
        ks.append(jnp.concatenate([z for z in (k_i, zq(L - nk)) if z.shape[0]], axis=0))
    return refs, qs, ks


def _hgrn_scores(qs, ks, sl, L):
    lhs = jnp.concatenate([q[:, sl] for q in qs], axis=1) if len(qs) > 1 else qs[0][:, sl]
    rhs = jnp.concatenate([k[:, sl] for k in ks], axis=1) if len(ks) > 1 else ks[0][:, sl]
    return lax.dot_general(lhs, rhs, NT_DIMS, preferred_element_type=F32)


def _hgrn_kernel(f_ref, i_ref, q_ref, g_ref, lbl_ref, s0_ref, gnorm_ref,
                 ohg_ref, s1_ref,
                 st_scr, a_scr, k_scr, sc_scr, o_scr, in_scr, *, L, nh, cps):
    c = pl.program_id(1)
    nc = pl.num_programs(1)
    sub = min(SUB, L)
    nb = L // sub

    @pl.when(c == 0)
    def _():
        for h in range(nh):
            st_scr[h] = s0_ref[h].T

    lbl = lbl_ref[...]
    mx = jnp.max(lbl, axis=0, keepdims=True)
    ex = jnp.exp(lbl - mx)
    lb = ex[0:1, :] / jnp.sum(ex, axis=0, keepdims=True)
    gnorm = gnorm_ref[...]
    rr = lax.broadcasted_iota(jnp.int32, (L, L), 0)
    cc = lax.broadcasted_iota(jnp.int32, (L, L), 1)
    causal = cc <= rr

    def finish(rows):
        gg = g_ref[rows, :]
        ohg_ref[rows, :] = (_rms(o_scr[rows, :], gnorm) * (gg * _sigmoid(gg))).astype(BF16)

    span = None
    for j in range(cps):
        cr = slice(j * L, (j + 1) * L)
        f = lb + (1.0 - lb) * _sigmoid(f_ref[cr, :])
        lf = jnp.log(f)
        kk = 1.0 - f
        qq = q_ref[cr, :]
        vb = i_ref[cr, :].astype(BF16)

        ridx = lax.broadcasted_iota(jnp.int32, lf.shape, 0)
        a = lf
        sh = 1
        while sh < L:
            a = a + jnp.where(ridx >= sh, pltpu.roll(a, sh, 0), 0.0)
            sh *= 2
        a_scr[cr, :] = a
        k_scr[cr, :] = kk

        a_last = a[L - 1:L, :]
        qa = (qq * jnp.exp(a)).astype(BF16)
        kdec = (kk * jnp.exp(a_last - a)).astype(BF16)
        dec_last = jnp.exp(a_last)
        refs, qs, ks = _hgrn_factors(qq, kk, a, qa, sub, nb)
        for i in range(nb):
            sp = refs[i] - a[(i + 1) * sub - 1:(i + 1) * sub, :]
            span = sp if span is None else jnp.maximum(span, sp)

        heads = [slice(h * DH_H, (h + 1) * DH_H) for h in range(nh)]
        st_old = [st_scr[h] for h in range(nh)]
        inter = [lax.dot_general(qa[:, sl], st_old[h].astype(BF16), NT_DIMS, preferred_element_type=F32)
                 for h, sl in enumerate(heads)]
        raw = [_hgrn_scores(qs, ks, sl, L) for sl in heads]
        upd = [lax.dot_general(vb[:, sl], kdec[:, sl], TN_DIMS, preferred_element_type=F32) for sl in heads]
        for h, sl in enumerate(heads):
            st_scr[h] = st_old[h] * dec_last[:, sl] + upd[h]
            in_scr[cr, sl] = inter[h]
        blk = [jnp.where(causal, r, 0.0).astype(BF16) for r in raw]
        for h, sl in enumerate(heads):
            o_scr[cr, sl] = inter[h] + jnp.dot(blk[h], vb[:, sl], preferred_element_type=F32)
        finish(cr)

    use_exact = jnp.logical_not(jnp.max(span) <= SAFE_SPAN)

    @pl.when(use_exact)
    def _():
        sc_scr[...] = jnp.zeros_like(sc_scr)
        lane = lax.broadcasted_iota(jnp.int32, (sub, LANES), 1)
        in_block = jnp.logical_and(causal, (cc // sub) == (rr // sub))

        def fix_chunk(j, carry):
            base = pl.multiple_of(j * L, L)

            def diag_body(s, carry2):
                r0 = pl.multiple_of((s // sub) * sub, sub)
                a_blk = a_scr[pl.ds(base + r0, sub), :]
                q_blk = q_ref[pl.ds(base + r0, sub), :]
                a_s = a_scr[pl.ds(base + s, 1), :]
                k_s = k_scr[pl.ds(base + s, 1), :]
                p = q_blk * jnp.exp(jnp.minimum(a_blk - a_s, 0.0)) * k_s
                for h in range(nh):
                    colv = jnp.sum(p[:, h * DH_H:(h + 1) * DH_H], axis=-1, keepdims=True)
                    old = sc_scr[h, pl.ds(r0, sub), :]
                    sc_scr[h, pl.ds(r0, sub), :] = jnp.where(lane == s, colv, old)
                return carry2

            lax.fori_loop(0, L, diag_body, 0)

            cr = pl.ds(base, L)
            a = a_scr[cr, :]
            kk = k_scr[cr, :]
            qq = q_ref[cr, :]
            vb = i_ref[cr, :].astype(BF16)
            qa = (qq * jnp.exp(a)).astype(BF16)
            _, qs, ks = _hgrn_factors(qq, kk, a, qa, sub, nb)
            for h in range(nh):
                sl = slice(h * DH_H, (h + 1) * DH_H)
                blk = jnp.where(causal, _hgrn_scores(qs, ks, sl, L), 0.0)
                blk = jnp.where(in_block, sc_scr[h][:, :L], blk).astype(BF16)
                o_scr[cr, sl] = in_scr[cr, sl] + jnp.dot(blk, vb[:, sl], preferred_element_type=F32)
            finish(cr)
            return carry

        lax.fori_loop(0, cps, fix_chunk, 0)

    @pl.when(c == nc - 1)
    def _():
        for h in range(nh):
            s1_ref[h] = st_scr[h].T


def _hgrn(proj, lb_logits, s0, g_hnorm, *, B, T):
    L = min(CHUNK, T)
    nc = T // L
    cps = HGRN_CPS if nc % HGRN_CPS == 0 else 1
    ns = nc // cps
    nh = s0.shape[1]
    dhh = nh * DH_H
    row = lambda b, c: b * ns + c
    kern = functools.partial(_hgrn_kernel, L=L, nh=nh, cps=cps)
    return pl.pallas_call(
        kern,
        grid=(B, ns),
        in_specs=[
            pl.BlockSpec((cps * L, dhh), lambda b, c: (row(b, c), 4)),
            pl.BlockSpec((cps * L, dhh), lambda b, c: (row(b, c), 5)),
            pl.BlockSpec((cps * L, dhh), lambda b, c: (row(b, c), 6)),
            pl.BlockSpec((cps * L, dhh), lambda b, c: (row(b, c), 7)),
            pl.BlockSpec((2, dhh), lambda b, c: (0, 0)),
            pl.BlockSpec((None, nh, DH_H, DH_H), lambda b, c: (b, 0, 0, 0)),
            pl.BlockSpec((1, dhh), lambda b, c: (0, 0)),
        ],
        out_specs=[
            pl.BlockSpec((cps * L, dhh), lambda b, c: (row(b, c), 0)),
            pl.BlockSpec((None, nh, DH_H, DH_H), lambda b, c: (b, 0, 0, 0)),
        ],
        out_shape=[
            jax.ShapeDtypeStruct((B * T, dhh), BF16),
            jax.ShapeDtypeStruct((B, nh, DH_H, DH_H), F32),
        ],
        scratch_shapes=[
            pltpu.VMEM((nh, DH_H, DH_H), F32),
            pltpu.VMEM((cps * L, dhh), F32),
            pltpu.VMEM((cps * L, dhh), F32),
            pltpu.VMEM((nh, L, LANES), F32),
            pltpu.VMEM((cps * L, dhh), F32),
            pltpu.VMEM((cps * L, dhh), F32),
        ],
        compiler_params=_params(("parallel", "arbitrary")),
        name="hgrn",
    )(proj, proj, proj, proj, lb_logits, s0, g_hnorm)


def _merge_kernel(hmg_ref, ohg_ref, ga_ref, gb_ref, x_ref, wa_ref, wb_ref, wo_ref, x1_ref):
    ya = jnp.dot(hmg_ref[...], wa_ref[...], preferred_element_type=F32)
    yb = jnp.dot(ohg_ref[...], wb_ref[...], preferred_element_type=F32)
    u = _sigmoid(ga_ref[...]) * ya + _sigmoid(gb_ref[...]) * yb
    x1_ref[...] = x_ref[...] + jnp.dot(u.astype(BF16), wo_ref[...], preferred_element_type=F32)


def _merge(hmg, ohg, proj, x2d, w_a, w_b, w_o):
    m, d = x2d.shape
    dm = hmg.shape[1]
    tm = min(m, 256)
    const = lambda i: (0, 0)
    return pl.pallas_call(
        _merge_kernel,
        grid=(m // tm,),
        in_specs=[
            pl.BlockSpec((tm, dm), lambda i: (i, 0)),
            pl.BlockSpec((tm, dm), lambda i: (i, 0)),
            pl.BlockSpec((tm, d), lambda i: (i, 4)),
            pl.BlockSpec((tm, d), lambda i: (i, 5)),
            pl.BlockSpec((tm, d), lambda i: (i, 0)),
            pl.BlockSpec((dm, d), const, pipeline_mode=pl.Buffered(1)),
            pl.BlockSpec((dm, d), const, pipeline_mode=pl.Buffered(1)),
            pl.BlockSpec((d, d), const, pipeline_mode=pl.Buffered(1)),
        ],
        out_specs=pl.BlockSpec((tm, d), lambda i: (i, 0)),
        out_shape=jax.ShapeDtypeStruct((m, d), F32),
        compiler_params=_params(("parallel",)),
        name="merge",
    )(hmg, ohg, proj, proj, x2d, w_a, w_b, w_o)


def _ffn_kernel(x1_ref, gffn_ref, w1_ref, w2_ref, gfin_ref, y_ref, h_scr, acc_scr):
    j = pl.program_id(1)

    @pl.when(j == 0)
    def _():
        h_scr[...] = _rms(x1_ref[...], gffn_ref[...]).astype(BF16)
        acc_scr[...] = jnp.zeros_like(acc_scr)

    t = jnp.maximum(jnp.dot(h_scr[...], w1_ref[...], preferred_element_type=F32), 0.0)
    acc_scr[...] += jnp.dot((t * t).astype(BF16), w2_ref[...], preferred_element_type=F32)

    @pl.when(j == pl.num_programs(1) - 1)
    def _():
        y_ref[...] = _rms(x1_ref[...] + acc_scr[...], gfin_ref[...])


def _ffn(x1, g_ffn, w1, w2, g_final):
    m, d = x1.shape
    dff = w1.shape[1]
    tm = min(m, 512)
    tf = 512
    return pl.pallas_call(
        _ffn_kernel,
        grid=(m // tm, dff // tf),
        in_specs=[
            pl.BlockSpec((tm, d), lambda i, j: (i, 0)),
            pl.BlockSpec((1, d), lambda i, j: (0, 0)),
            pl.BlockSpec((d, tf), lambda i, j: (0, j)),
            pl.BlockSpec((tf, d), lambda i, j: (j, 0)),
            pl.BlockSpec((1, d), lambda i, j: (0, 0)),
        ],
        out_specs=pl.BlockSpec((tm, d), lambda i, j: (i, 0)),
        out_shape=jax.ShapeDtypeStruct((m, d), F32),
        scratch_shapes=[pltpu.VMEM((tm, d), BF16), pltpu.VMEM((tm, d), F32)],
        compiler_params=_params(("parallel", "arbitrary")),
        name="ffn",
    )(x1, g_ffn, w1, w2, g_final)


def _group(x, conv_prev, c0, n0, m0, s0, wts):
    B, T, d = x.shape
    x2d = x.reshape(B * T, d)
    proj, gates = _inproj(x2d, wts["g_mix"], wts["w_lo"], wts["w_hi"], wts["w_if"], wts["b_if"])
    hmg, new_conv, c1, n1, m1 = _mlstm(proj, gates, conv_prev, c0, n0, m0.reshape(B, 1, -1),
                                       wts["w_conv"], wts["b_conv"], wts["g_mnorm"], B=B, T=T)
    ohg, s1 = _hgrn(proj, wts["lb_logits"], s0, wts["g_hnorm"], B=B, T=T)
    x1 = _merge(hmg, ohg, proj, x2d, wts["w_a"], wts["w_b"], wts["w_o"])
    y = _ffn(x1, wts["g_ffn"], wts["w_ff1"], wts["w_ff2"], wts["g_final"])
    return (y.reshape(B, T, d), new_conv[None], c1[None], n1[None], m1.reshape(1, B, -1), s1[None])


def kernel(x_prompt, x_sample, cache_mlstm_conv, state_mlstm_C, state_mlstm_n, state_mlstm_m, state_hgrn_S,
           g_mix, w_in, b_if, w_conv, b_conv, g_mnorm, g_hnorm, hgrn_lb_logits, w_branch_a, w_branch_b,
           w_out, g_ffn, w_ff1, w_ff2, g_final):
    depth = w_in.shape[0]
    assert depth == 1, "single-layer trunk only"
    d = x_prompt.shape[-1]
    dm = w_branch_a.shape[1]
    dhh = w_branch_b.shape[1]
    nh_h = dhh // DH_H
    dh_m = dm // H_M
    assert hgrn_lb_logits.shape[0] == depth + 1
    assert 2 * dm == d and dhh == dm

    w = w_in[0]
    n_if = 2 * H_M
    p_if = 4 * dm
    w_lo = w[:, :p_if].astype(BF16)
    w_hi = w[:, p_if + n_if:].astype(BF16)
    w_if = jnp.pad(w[:, p_if:p_if + n_if], ((0, 0), (0, LANES - n_if))).astype(BF16)
    bif = jnp.pad(b_if[0].astype(F32), (0, LANES - n_if)).reshape(1, LANES)
    wts = dict(
        g_mix=g_mix[0].reshape(1, d), w_lo=w_lo, w_hi=w_hi, w_if=w_if, b_if=bif,
        w_conv=w_conv[0], b_conv=b_conv[0].reshape(1, -1), g_mnorm=g_mnorm[0].reshape(1, dm),
        lb_logits=hgrn_lb_logits.astype(F32), g_hnorm=g_hnorm[0].reshape(1, dhh),
        w_a=w_branch_a[0].astype(BF16), w_b=w_branch_b[0].astype(BF16), w_o=w_out[0].astype(BF16),
        g_ffn=g_ffn[0].reshape(1, d), w_ff1=w_ff1[0].astype(BF16), w_ff2=w_ff2[0].astype(BF16),
        g_final=g_final.reshape(1, d),
    )

    bp = x_prompt.shape[0]
    zeros = lambda *s: jnp.zeros(s, F32)
    outs_p = _group(x_prompt, zeros(bp, CONV_W - 1, 2 * dm), zeros(bp, H_M, dh_m, dh_m), zeros(bp, H_M, dh_m),
                    zeros(bp, H_M), zeros(bp, nh_h, DH_H, DH_H), wts)
    outs_s = _group(x_sample, cache_mlstm_conv[0], state_mlstm_C[0].astype(F32), state_mlstm_n[0].astype(F32),
                    state_mlstm_m[0].astype(F32), state_hgrn_S[0].astype(F32), wts)
    return (outs_p[0], outs_s[0]) + outs_p[1:] + outs_s[1:]
```

```python
import functools

import jax
import jax.numpy as jnp
from jax import lax
from jax.experimental import pallas as pl
from jax.experimental.pallas import tpu as pltpu

F32 = jnp.float32
BF16 = jnp.bfloat16
HIGHEST = lax.Precision.HIGHEST

EPS = 1e-6
CHUNK = 64
MLSTM_CHUNK = 128
MLSTM_CPS = 2
CONV_W = 4
H_M = 4
DH_H = 128
SUB = 32
HGRN_CPS = 4
SAFE_SPAN = 60.0
LANES = 128
CONV_PAD = 8
FFN_PIECE = 256
VMEM_LIMIT = 56 * 1024 * 1024

NT_DIMS = (((1,), (1,)), ((), ()))
TN_DIMS = (((0,), (0,)), ((), ()))


def _rms(x, g):
    return x * lax.rsqrt(jnp.mean(x * x, axis=-1, keepdims=True) + EPS) * g


def _sigmoid(x):
    return 1.0 / (1.0 + jnp.exp(-x))


def _params(sem):
    return pltpu.CompilerParams(dimension_semantics=sem, vmem_limit_bytes=VMEM_LIMIT)


def _inproj_lo_kernel(x_ref, g_ref, w_ref, wif_ref, bif_ref, proj_ref, gates_ref, h_ref):
    @pl.when(pl.program_id(1) == 0)
    def _():
        hb = _rms(x_ref[...], g_ref[...]).astype(BF16)
        h_ref[...] = hb
        gates_ref[...] = jnp.dot(hb, wif_ref[...], preferred_element_type=F32) + bif_ref[...]

    proj_ref[...] = jnp.dot(h_ref[...], w_ref[...], preferred_element_type=F32)


def _matmul_kernel(h_ref, w_ref, o_ref):
    o_ref[...] = jnp.dot(h_ref[...], w_ref[...], preferred_element_type=F32)


def _inproj(x2d, g_mix, w_lo, w_hi, w_if, b_if):
    m, d = x2d.shape
    tm = min(m, 1024)
    tn = 1024
    proj_lo, gates, h = pl.pallas_call(
        _inproj_lo_kernel,
        grid=(m // tm, w_lo.shape[1] // tn),
        in_specs=[
            pl.BlockSpec((tm, d), lambda i, j: (i, 0)),
            pl.BlockSpec((1, d), lambda i, j: (0, 0)),
            pl.BlockSpec((d, tn), lambda i, j: (0, j)),
            pl.BlockSpec((d, LANES), lambda i, j: (0, 0)),
            pl.BlockSpec((1, LANES), lambda i, j: (0, 0)),
        ],
        out_specs=[
            pl.BlockSpec((tm, tn), lambda i, j: (i, j)),
            pl.BlockSpec((tm, LANES), lambda i, j: (i, 0)),
            pl.BlockSpec((tm, d), lambda i, j: (i, 0)),
        ],
        out_shape=[jax.ShapeDtypeStruct((m, w_lo.shape[1]), F32), jax.ShapeDtypeStruct((m, LANES), F32),
                   jax.ShapeDtypeStruct((m, d), BF16)],
        compiler_params=_params(("parallel", "arbitrary")),
        name="inproj_lo",
    )(x2d, g_mix, w_lo, w_if, b_if)
    proj_hi = pl.pallas_call(
        _matmul_kernel,
        grid=(m // tm, w_hi.shape[1] // tn),
        in_specs=[
            pl.BlockSpec((tm, d), lambda i, j: (i, 0)),
            pl.BlockSpec((d, tn), lambda i, j: (0, j)),
        ],
        out_specs=pl.BlockSpec((tm, tn), lambda i, j: (i, j)),
        out_shape=jax.ShapeDtypeStruct((m, w_hi.shape[1]), F32),
        compiler_params=_params(("parallel", "arbitrary")),
        name="inproj_hi",
    )(h, w_hi)
    return proj_lo, proj_hi, gates


def _mlstm_kernel(qk_ref, v_ref, o_ref, gates_ref, convprev_ref, c0_ref, n0_ref, m0_ref,
                  wconv_ref, bconv_ref, gnorm_ref,
                  hmg_ref, newconv_ref, c1_ref, n1_ref, m1_ref,
                  xpad, c_scr, n_scr, m_scr, *, L, dm, dh, cps):
    c = pl.program_id(1)
    nc = pl.num_programs(1)
    hist0 = CONV_PAD - (CONV_W - 1)

    @pl.when(c == 0)
    def _():
        xpad[0:hist0, :] = jnp.zeros((hist0, xpad.shape[1]), F32)
        xpad[hist0:CONV_PAD, :] = convprev_ref[...]
        c_scr[...] = c0_ref[...]
        n_scr[...] = n0_ref[...]
        m_scr[...] = m0_ref[...]

    rr = lax.broadcasted_iota(jnp.int32, (L, L), 0)
    cc = lax.broadcasted_iota(jnp.int32, (L, L), 1)
    causal = rr >= cc
    tril = causal.astype(F32)
    eye = (lax.broadcasted_iota(jnp.int32, (8, LANES), 0)
           == lax.broadcasted_iota(jnp.int32, (8, LANES), 1)).astype(F32)
    heads = [slice(h * dh, (h + 1) * dh) for h in range(H_M)]
    k_scale = dh ** -0.5

    for j in range(cps):
        cr = slice(j * L, (j + 1) * L)
        x0 = CONV_PAD + j * L
        xpad[x0:x0 + L, :] = qk_ref[cr, :]
        win = xpad[x0 - CONV_PAD:x0 + L, :]
        conv = bconv_ref[...]
        for i in range(CONV_W):
            d = CONV_W - 1 - i
            tap = pltpu.roll(win, d, 0) if d else win
            conv = conv + tap[CONV_PAD:, :] * wconv_ref[i:i + 1, :]
        qk = conv * _sigmoid(conv)
        q = qk[:, :dm]
        k = qk[:, dm:]
        qb = q.astype(BF16)
        kb = k.astype(BF16)
        vv = v_ref[cr, :]
        vb = vv.astype(BF16)

        g = gates_ref[cr, :]
        col = lax.broadcasted_iota(jnp.int32, g.shape, 1)
        lf = jnp.minimum(g, 0.0) - jnp.log1p(jnp.exp(-jnp.abs(g)))
        x_cols = jnp.where(col < H_M, g, lf)
        cum = jnp.dot(tril, x_cols, preferred_element_type=F32, precision=HIGHEST)
        xc = jnp.where(col < H_M, g, cum)
        rows = lax.dot_general(eye, xc, NT_DIMS, preferred_element_type=F32, precision=HIGHEST)

        m_all = m_scr[...]
        e_s, sc_inter, inv_floor, decay, wgt, m_new = [], [], [], [], [], []
        for h in range(H_M):
            bcol = cum[:, H_M + h:H_M + h + 1]
            igcol = g[:, h:h + 1]
            brow = rows[H_M + h:H_M + h + 1, :]
            igrow = rows[h:h + 1, :]
            m_prev = m_all[:, h:h + 1]
            dlog = jnp.where(causal, bcol - brow + igrow, -jnp.inf)
            m_inter = bcol + m_prev
            m_t = jnp.maximum(m_inter, jnp.max(dlog, axis=-1, keepdims=True))
            e_s.append(jnp.exp(dlog - m_t) * k_scale)
            sc_inter.append(jnp.exp(m_inter - m_t))
            inv_floor.append(jnp.exp(-m_t))
            mn = m_t[L - 1:L, :]
            b_last = bcol[L - 1:L, :]
            m_new.append(mn)
            decay.append(jnp.exp(b_last + m_prev - mn))
            wgt.append(jnp.exp(b_last - bcol + igcol - mn) * k_scale)

        c_old = [c_scr[h] for h in range(H_M)]
        s_raw = [lax.dot_general(qb[:, sl], kb[:, sl], NT_DIMS, preferred_element_type=F32) for sl in heads]
        qc = [lax.dot_general(qb[:, sl], c_old[h].astype(BF16), NT_DIMS, preferred_element_type=F32)
              for h, sl in enumerate(heads)]
        upd = [lax.dot_general((vv[:, sl] * wgt[h]).astype(BF16), kb[:, sl], TN_DIMS, preferred_element_type=F32)
               for h, sl in enumerate(heads)]
        qn = []
        for h, sl in enumerate(heads):
            n_old = n_scr[h:h + 1, :]
            qn.append(jnp.sum(q[:, sl] * n_old, axis=-1, keepdims=True))
            c_scr[h] = decay[h] * c_old[h] + upd[h]
            n_scr[h:h + 1, :] = decay[h] * n_old + jnp.sum(wgt[h] * k[:, sl], axis=0, keepdims=True)
            m_scr[:, h:h + 1] = m_new[h]
        s = [s_raw[h] * e_s[h] for h in range(H_M)]
        pv = [jnp.dot(s[h].astype(BF16), vb[:, sl], preferred_element_type=F32) for h, sl in enumerate(heads)]
        oo = o_ref[cr, :]
        for h, sl in enumerate(heads):
            num = sc_inter[h] * qc[h] + pv[h]
            den = sc_inter[h] * qn[h] + jnp.sum(s[h], axis=-1, keepdims=True)
            hout = num / jnp.maximum(jnp.abs(den), inv_floor[h])
            y = _rms(hout, gnorm_ref[:, sl]) * _sigmoid(oo[:, sl])
            hmg_ref[cr, sl] = y.astype(BF16)

    x_end = CONV_PAD + cps * L
    hist = xpad[x_end - (CONV_W - 1):x_end, :]
    xpad[hist0:CONV_PAD, :] = hist

    @pl.when(c == nc - 1)
    def _():
        newconv_ref[...] = hist
        c1_ref[...] = c_scr[...]
        n1_ref[...] = n_scr[...]
        m1_ref[...] = m_scr[...]


def _mlstm(proj, gates, conv_prev, c0, n0, m0, w_conv, b_conv, g_mnorm, *, B, T):
    L = MLSTM_CHUNK if T % MLSTM_CHUNK == 0 else min(CHUNK, T)
    nc = T // L
    dm = c0.shape[1] * c0.shape[2]
    dh = c0.shape[2]
    cps = MLSTM_CPS if nc % MLSTM_CPS == 0 else 1
    ns = nc // cps
    row = lambda b, c: b * ns + c
    kern = functools.partial(_mlstm_kernel, L=L, dm=dm, dh=dh, cps=cps)
    return pl.pallas_call(
        kern,
        grid=(B, ns),
        in_specs=[
            pl.BlockSpec((cps * L, 2 * dm), lambda b, c: (row(b, c), 0)),
            pl.BlockSpec((cps * L, dm), lambda b, c: (row(b, c), 2)),
            pl.BlockSpec((cps * L, dm), lambda b, c: (row(b, c), 3)),
            pl.BlockSpec((cps * L, LANES), lambda b, c: (row(b, c), 0)),
            pl.BlockSpec((None, CONV_W - 1, 2 * dm), lambda b, c: (b, 0, 0)),
            pl.BlockSpec((None, H_M, dh, dh), lambda b, c: (b, 0, 0, 0)),
            pl.BlockSpec((None, H_M, dh), lambda b, c: (b, 0, 0)),
            pl.BlockSpec((None, 1, H_M), lambda b, c: (b, 0, 0)),
            pl.BlockSpec((CONV_W, 2 * dm), lambda b, c: (0, 0)),
            pl.BlockSpec((1, 2 * dm), lambda b, c: (0, 0)),
            pl.BlockSpec((1, dm), lambda b, c: (0, 0)),
        ],
        out_specs=[
            pl.BlockSpec((cps * L, dm), lambda b, c: (row(b, c), 0)),
            pl.BlockSpec((None, CONV_W - 1, 2 * dm), lambda b, c: (b, 0, 0)),
            pl.BlockSpec((None, H_M, dh, dh), lambda b, c: (b, 0, 0, 0)),
            pl.BlockSpec((None, H_M, dh), lambda b, c: (b, 0, 0)),
            pl.BlockSpec((None, 1, H_M), lambda b, c: (b, 0, 0)),
        ],
        out_shape=[
            jax.ShapeDtypeStruct((B * T, dm), BF16),
            jax.ShapeDtypeStruct((B, CONV_W - 1, 2 * dm), F32),
            jax.ShapeDtypeStruct((B, H_M, dh, dh), F32),
            jax.ShapeDtypeStruct((B, H_M, dh), F32),
            jax.ShapeDtypeStruct((B, 1, H_M), F32),
        ],
        scratch_shapes=[
            pltpu.VMEM((CONV_PAD + cps * L, 2 * dm), F32),
            pltpu.VMEM((H_M, dh, dh), F32),
            pltpu.VMEM((H_M, dh), F32),
            pltpu.VMEM((1, H_M), F32),
        ],
        compiler_params=_params(("parallel", "arbitrary")),
        name="mlstm",
    )(proj, proj, proj, gates, conv_prev, c0, n0, m0, w_conv, b_conv, g_mnorm)


def _hgrn_factors(qq, kk, a, qa, sub, nb):
    L = a.shape[0]
    refs = [jnp.zeros_like(a[0:1, :])] + [a[i * sub - 1:i * sub, :] for i in range(1, nb)]
    qs = []
    ks = []
    for i in range(nb):
        rows = slice(i * sub, (i + 1) * sub)
        nk = (i + 1) * sub
        q_i = qa[rows, :] if i == 0 else (qq[rows, :] * jnp.exp(a[rows, :] - refs[i])).astype(BF16)
        k_i = (kk[:nk, :] * jnp.exp(jnp.minimum(refs[i] - a[:nk, :], SAFE_SPAN))).astype(BF16)
        zq = lambda n: jnp.zeros((n, q_i.shape[1]), BF16)
        qs.append(jnp.concatenate([z for z in (zq(i * sub), q_i, zq(L - nk)) if z.shape[0]], axis=0))
        ks.append(jnp.concatenate([z for z in (k_i, zq(L - nk)) if z.shape[0]], axis=0))
    return refs, qs, ks


def _hgrn_scores(qs, ks, sl, L):
    lhs = jnp.concatenate([q[:, sl] for q in qs], axis=1) if len(qs) > 1 else qs[0][:, sl]
    rhs = jnp.concatenate([k[:, sl] for k in ks], axis=1) if len(ks) > 1 else ks[0][:, sl]
    return lax.dot_general(lhs, rhs, NT_DIMS, preferred_element_type=F32)


def _hgrn_kernel(f_ref, i_ref, q_ref, g_ref, lbl_ref, s0_ref, gnorm_ref,
                 ohg_ref, s1_ref,
                 st_scr, a_scr, k_scr, sc_scr, o_scr, in_scr, *, L, nh, cps):
    c = pl.program_id(1)
    nc = pl.num_programs(1)
    sub = min(SUB, L)
    nb = L // sub

    @pl.when(c == 0)
    def _():
        for h in range(nh):
            st_scr[h] = s0_ref[h].T

    lbl = lbl_ref[...]
    mx = jnp.max(lbl, axis=0, keepdims=True)
    ex = jnp.exp(lbl - mx)
    lb = ex[0:1, :] / jnp.sum(ex, axis=0, keepdims=True)
    gnorm = gnorm_ref[...]
    rr = lax.broadcasted_iota(jnp.int32, (L, L), 0)
    cc = lax.broadcasted_iota(jnp.int32, (L, L), 1)
    causal = cc <= rr

    def finish(rows):
        gg = g_ref[rows, :]
        ohg_ref[rows, :] = (_rms(o_scr[rows, :], gnorm) * (gg * _sigmoid(gg))).astype(BF16)

    span = None
    for j in range(cps):
        cr = slice(j * L, (j + 1) * L)
        f = lb + (1.0 - lb) * _sigmoid(f_ref[cr, :])
        lf = jnp.log(f)
        kk = 1.0 - f
        qq = q_ref[cr, :]
        vb = i_ref[cr, :].astype(BF16)

        ridx = lax.broadcasted_iota(jnp.int32, lf.shape, 0)
        a = lf
        sh = 1
        while sh < L:
            a = a + jnp.where(ridx >= sh, pltpu.roll(a, sh, 0), 0.0)
            sh *= 2
        a_scr[cr, :] = a
        k_scr[cr, :] = kk

        a_last = a[L - 1:L, :]
        qa = (qq * jnp.exp(a)).astype(BF16)
        kdec = (kk * jnp.exp(a_last - a)).astype(BF16)
        dec_last = jnp.exp(a_last)
        refs, qs, ks = _hgrn_factors(qq, kk, a, qa, sub, nb)
        for i in range(nb):
            sp = refs[i] - a[(i + 1) * sub - 1:(i + 1) * sub, :]
            span = sp if span is None else jnp.maximum(span, sp)

        heads = [slice(h * DH_H, (h + 1) * DH_H) for h in range(nh)]
        st_old = [st_scr[h] for h in range(nh)]
        inter = [lax.dot_general(qa[:, sl], st_old[h].astype(BF16), NT_DIMS, preferred_element_type=F32)
                 for h, sl in enumerate(heads)]
        raw = [_hgrn_scores(qs, ks, sl, L) for sl in heads]
        upd = [lax.dot_general(vb[:, sl], kdec[:, sl], TN_DIMS, preferred_element_type=F32) for sl in heads]
        for h, sl in enumerate(heads):
            st_scr[h] = st_old[h] * dec_last[:, sl] + upd[h]
            in_scr[cr, sl] = inter[h]
        blk = [jnp.where(causal, r, 0.0).astype(BF16) for r in raw]
        for h, sl in enumerate(heads):
            o_scr[cr, sl] = inter[h] + jnp.dot(blk[h], vb[:, sl], preferred_element_type=F32)
        finish(cr)

    use_exact = jnp.logical_not(jnp.max(span) <= SAFE_SPAN)

    @pl.when(use_exact)
    def _():
        sc_scr[...] = jnp.zeros_like(sc_scr)
        lane = lax.broadcasted_iota(jnp.int32, (sub, LANES), 1)
        in_block = jnp.logical_and(causal, (cc // sub) == (rr // sub))

        def fix_chunk(j, carry):
            base = pl.multiple_of(j * L, L)

            def diag_body(s, carry2):
                r0 = pl.multiple_of((s // sub) * sub, sub)
                a_blk = a_scr[pl.ds(base + r0, sub), :]
                q_blk = q_ref[pl.ds(base + r0, sub), :]
                a_s = a_scr[pl.ds(base + s, 1), :]
                k_s = k_scr[pl.ds(base + s, 1), :]
                p = q_blk * jnp.exp(jnp.minimum(a_blk - a_s, 0.0)) * k_s
                for h in range(nh):
                    colv = jnp.sum(p[:, h * DH_H:(h + 1) * DH_H], axis=-1, keepdims=True)
                    old = sc_scr[h, pl.ds(r0, sub), :]
                    sc_scr[h, pl.ds(r0, sub), :] = jnp.where(lane == s, colv, old)
                return carry2

            lax.fori_loop(0, L, diag_body, 0)

            cr = pl.ds(base, L)
            a = a_scr[cr, :]
            kk = k_scr[cr, :]
            qq = q_ref[cr, :]
            vb = i_ref[cr, :].astype(BF16)
            qa = (qq * jnp.exp(a)).astype(BF16)
            _, qs, ks = _hgrn_factors(qq, kk, a, qa, sub, nb)
            for h in range(nh):
                sl = slice(h * DH_H, (h + 1) * DH_H)
                blk = jnp.where(causal, _hgrn_scores(qs, ks, sl, L), 0.0)
                blk = jnp.where(in_block, sc_scr[h][:, :L], blk).astype(BF16)
                o_scr[cr, sl] = in_scr[cr, sl] + jnp.dot(blk, vb[:, sl], preferred_element_type=F32)
            finish(cr)
            return carry

        lax.fori_loop(0, cps, fix_chunk, 0)

    @pl.when(c == nc - 1)
    def _():
        for h in range(nh):
            s1_ref[h] = st_scr[h].T


def _hgrn(proj, lb_logits, s0, g_hnorm, *, B, T):
    L = min(CHUNK, T)
    nc = T // L
    cps = HGRN_CPS if nc % HGRN_CPS == 0 else 1
    ns = nc // cps
    nh = s0.shape[1]
    dhh = nh * DH_H
    row = lambda b, c: b * ns + c
    kern = functools.partial(_hgrn_kernel, L=L, nh=nh, cps=cps)
    return pl.pallas_call(
        kern,
        grid=(B, ns),
        in_specs=[
            pl.BlockSpec((cps * L, dhh), lambda b, c: (row(b, c), 0)),
            pl.BlockSpec((cps * L, dhh), lambda b, c: (row(b, c), 1)),
            pl.BlockSpec((cps * L, dhh), lambda b, c: (row(b, c), 2)),
            pl.BlockSpec((cps * L, dhh), lambda b, c: (row(b, c), 3)),
            pl.BlockSpec((2, dhh), lambda b, c: (0, 0)),
            pl.BlockSpec((None, nh, DH_H, DH_H), lambda b, c: (b, 0, 0, 0)),
            pl.BlockSpec((1, dhh), lambda b, c: (0, 0)),
        ],
        out_specs=[
            pl.BlockSpec((cps * L, dhh), lambda b, c: (row(b, c), 0)),
            pl.BlockSpec((None, nh, DH_H, DH_H), lambda b, c: (b, 0, 0, 0)),
        ],
        out_shape=[
            jax.ShapeDtypeStruct((B * T, dhh), BF16),
            jax.ShapeDtypeStruct((B, nh, DH_H, DH_H), F32),
        ],
        scratch_shapes=[
            pltpu.VMEM((nh, DH_H, DH_H), F32),
            pltpu.VMEM((cps * L, dhh), F32),
            pltpu.VMEM((cps * L, dhh), F32),
            pltpu.VMEM((nh, L, LANES), F32),
            pltpu.VMEM((cps * L, dhh), F32),
            pltpu.VMEM((cps * L, dhh), F32),
        ],
        compiler_params=_params(("parallel", "arbitrary")),
        name="hgrn",
    )(proj, proj, proj, proj, lb_logits, s0, g_hnorm)


def _merge_kernel(hmg_ref, ohg_ref, ga_ref, gb_ref, x_ref, wa_ref, wb_ref, wo_ref, x1_ref):
    ya = jnp.dot(hmg_ref[...], wa_ref[...], preferred_element_type=F32)
    yb = jnp.dot(ohg_ref[...], wb_ref[...], preferred_element_type=F32)
    u = _sigmoid(ga_ref[...]) * ya + _sigmoid(gb_ref[...]) * yb
    x1_ref[...] = x_ref[...] + jnp.dot(u.astype(BF16), wo_ref[...], preferred_element_type=F32)


def _merge(hmg, ohg, proj, x2d, w_a, w_b, w_o):
    m, d = x2d.shape
    dm = hmg.shape[1]
    tm = min(m, 256)
    const = lambda i: (0, 0)
    return pl.pallas_call(
        _merge_kernel,
        grid=(m // tm,),
        in_specs=[
            pl.BlockSpec((tm, dm), lambda i: (i, 0)),
            pl.BlockSpec((tm, dm), lambda i: (i, 0)),
            pl.BlockSpec((tm, d), lambda i: (i, 2)),
            pl.BlockSpec((tm, d), lambda i: (i, 3)),
            pl.BlockSpec((tm, d), lambda i: (i, 0)),
            pl.BlockSpec((dm, d), const, pipeline_mode=pl.Buffered(1)),
            pl.BlockSpec((dm, d), const, pipeline_mode=pl.Buffered(1)),
            pl.BlockSpec((d, d), const, pipeline_mode=pl.Buffered(1)),
        ],
        out_specs=pl.BlockSpec((tm, d), lambda i: (i, 0)),
        out_shape=jax.ShapeDtypeStruct((m, d), F32),
        compiler_params=_params(("parallel",)),
        name="merge",
    )(hmg, ohg, proj, proj, x2d, w_a, w_b, w_o)


def _ffn_kernel(x1_ref, gffn_ref, w1_ref, w2_ref, gfin_ref, y_ref, h_scr, acc_scr):
    j = pl.program_id(1)

    @pl.when(j == 0)
    def _():
        h_scr[...] = _rms(x1_ref[...], gffn_ref[...]).astype(BF16)
        acc_scr[...] = jnp.zeros_like(acc_scr)

    h = h_scr[...]
    tf = w1_ref.shape[1]
    pieces = [slice(p, p + FFN_PIECE) for p in range(0, tf, FFN_PIECE)]
    ts = []
    for ps in pieces:
        t = jnp.maximum(jnp.dot(h, w1_ref[:, ps], preferred_element_type=F32), 0.0)
        ts.append((t * t).astype(BF16))
    acc = acc_scr[...]
    for ps, t in zip(pieces, ts):
        acc = acc + jnp.dot(t, w2_ref[ps, :], preferred_element_type=F32)
    acc_scr[...] = acc

    @pl.when(j == pl.num_programs(1) - 1)
    def _():
        y_ref[...] = _rms(x1_ref[...] + acc_scr[...], gfin_ref[...])


def _ffn(x1, g_ffn, w1, w2, g_final):
    m, d = x1.shape
    dff = w1.shape[1]
    tm = min(m, 512)
    tf = 1024
    return pl.pallas_call(
        _ffn_kernel,
        grid=(m // tm, dff // tf),
        in_specs=[
            pl.BlockSpec((tm, d), lambda i, j: (i, 0)),
            pl.BlockSpec((1, d), lambda i, j: (0, 0)),
            pl.BlockSpec((d, tf), lambda i, j: (0, j)),
            pl.BlockSpec((tf, d), lambda i, j: (j, 0)),
            pl.BlockSpec((1, d), lambda i, j: (0, 0)),
        ],
        out_specs=pl.BlockSpec((tm, d), lambda i, j: (i, 0)),
        out_shape=jax.ShapeDtypeStruct((m, d), F32),
        scratch_shapes=[pltpu.VMEM((tm, d), BF16), pltpu.VMEM((tm, d), F32)],
        compiler_params=_params(("parallel", "arbitrary")),
        name="ffn",
    )(x1, g_ffn, w1, w2, g_final)


def _group(x, conv_prev, c0, n0, m0, s0, wts):
    B, T, d = x.shape
    x2d = x.reshape(B * T, d)
    proj_lo, proj_hi, gates = _inproj(x2d, wts["g_mix"], wts["w_lo"], wts["w_hi"], wts["w_if"], wts["b_if"])
    hmg, new_conv, c1, n1, m1 = _mlstm(proj_lo, gates, conv_prev, c0, n0, m0.reshape(B, 1, -1),
                                       wts["w_conv"], wts["b_conv"], wts["g_mnorm"], B=B, T=T)
    ohg, s1 = _hgrn(proj_hi, wts["lb_logits"], s0, wts["g_hnorm"], B=B, T=T)
    x1 = _merge(hmg, ohg, proj_hi, x2d, wts["w_a"], wts["w_b"], wts["w_o"])
    y = _ffn(x1, wts["g_ffn"], wts["w_ff1"], wts["w_ff2"], wts["g_final"])
    return (y.reshape(B, T, d), new_conv[None], c1[None], n1[None], m1.reshape(1, B, -1), s1[None])


def kernel(x_prompt, x_sample, cache_mlstm_conv, state_mlstm_C, state_mlstm_n, state_mlstm_m, state_hgrn_S,
           g_mix, w_in, b_if, w_conv, b_conv, g_mnorm, g_hnorm, hgrn_lb_logits, w_branch_a, w_branch_b,
           w_out, g_ffn, w_ff1, w_ff2, g_final):
    depth = w_in.shape[0]
    assert depth == 1, "single-layer trunk only"
    d = x_prompt.shape[-1]
    dm = w_branch_a.shape[1]
    dhh = w_branch_b.shape[1]
    nh_h = dhh // DH_H
    dh_m = dm // H_M
    assert hgrn_lb_logits.shape[0] == depth + 1
    assert 2 * dm == d and dhh == dm

    w = w_in[0]
    n_if = 2 * H_M
    p_if = 4 * dm
    w_lo = w[:, :p_if].astype(BF16)
    w_hi = w[:, p_if + n_if:].astype(BF16)
    w_if = jnp.pad(w[:, p_if:p_if + n_if], ((0, 0), (0, LANES - n_if))).astype(BF16)
    bif = jnp.pad(b_if[0].astype(F32), (0, LANES - n_if)).reshape(1, LANES)
    wts = dict(
        g_mix=g_mix[0].reshape(1, d), w_lo=w_lo, w_hi=w_hi, w_if=w_if, b_if=bif,
        w_conv=w_conv[0], b_conv=b_conv[0].reshape(1, -1), g_mnorm=g_mnorm[0].reshape(1, dm),
        lb_logits=hgrn_lb_logits.astype(F32), g_hnorm=g_hnorm[0].reshape(1, dhh),
        w_a=w_branch_a[0].astype(BF16), w_b=w_branch_b[0].astype(BF16), w_o=w_out[0].astype(BF16),
        g_ffn=g_ffn[0].reshape(1, d), w_ff1=w_ff1[0].astype(BF16), w_ff2=w_ff2[0].astype(BF16),
        g_final=g_final.reshape(1, d),
    )

    bp = x_prompt.shape[0]
    zeros = lambda *s: jnp.zeros(s, F32)
    outs_p = _group(x_prompt, zeros(bp, CONV_W - 1, 2 * dm), zeros(bp, H_M, dh_m, dh_m), zeros(bp, H_M, dh_m),
                    zeros(bp, H_M), zeros(bp, nh_h, DH_H, DH_H), wts)
    outs_s = _group(x_sample, cache_mlstm_conv[0], state_mlstm_C[0].astype(F32), state_mlstm_n[0].astype(F32),
                    state_mlstm_m[0].astype(F32), state_hgrn_S[0].astype(F32), wts)
    return (outs_p[0], outs_s[0]) + outs_p[1:] + outs_s[1:]
```

```python
import functools

import jax
import jax.numpy as jnp
from jax import lax
from jax.experimental import pallas as pl
from jax.experimental.pallas import tpu as pltpu

F32 = jnp.float32
BF16 = jnp.bfloat16
HIGHEST = lax.Precision.HIGHEST

EPS = 1e-6
CHUNK = 64
MLSTM_CHUNK = 128
MLSTM_CPS = 2
CONV_W = 4
H_M = 4
DH_H = 128
SUB = 32
HGRN_CPS = 4
SAFE_SPAN = 60.0
LANES = 128
CONV_PAD = 8
FFN_PIECE = 256
VMEM_LIMIT = 56 * 1024 * 1024

NT_DIMS = (((1,), (1,)), ((), ()))
TN_DIMS = (((0,), (0,)), ((), ()))


def _rms(x, g):
    return x * lax.rsqrt(jnp.mean(x * x, axis=-1, keepdims=True) + EPS) * g


def _sigmoid(x):
    return 0.5 * jnp.tanh(0.5 * x) + 0.5


def _params(sem):
    return pltpu.CompilerParams(dimension_semantics=sem, vmem_limit_bytes=VMEM_LIMIT)


def _shift_cast_kernel(a_ref, b_ref, o_ref, *, shift):
    cat = jnp.concatenate([a_ref[...], b_ref[...]], axis=1)
    o_ref[...] = pltpu.roll(cat, cat.shape[1] - shift, 1)[:, :o_ref.shape[1]].astype(BF16)


def _shifted_columns_bf16(w, start, width):
    d = w.shape[0]
    tr, tn = 256, 1024
    base, shift = divmod(start, LANES)
    assert width % tn == 0 and d % tr == 0 and (base * LANES) % tn == 0 and 0 < shift < LANES
    jb = base * LANES // tn
    return pl.pallas_call(
        functools.partial(_shift_cast_kernel, shift=shift),
        grid=(d // tr, width // tn),
        in_specs=[
            pl.BlockSpec((tr, tn), lambda r, j: (r, jb + j)),
            pl.BlockSpec((tr, LANES), lambda r, j: (r, (jb + j + 1) * (tn // LANES))),
        ],
        out_specs=pl.BlockSpec((tr, tn), lambda r, j: (r, j)),
        out_shape=jax.ShapeDtypeStruct((d, width), BF16),
        compiler_params=_params(("parallel", "parallel")),
        name="shift_cast",
    )(w, w)


def _inproj_lo_kernel(x_ref, g_ref, w_ref, wif_ref, bif_ref, proj_ref, gates_ref, h_ref):
    @pl.when(pl.program_id(1) == 0)
    def _():
        hb = _rms(x_ref[...], g_ref[...]).astype(BF16)
        h_ref[...] = hb
        gates_ref[...] = jnp.dot(hb, wif_ref[...], preferred_element_type=F32) + bif_ref[...]

    proj_ref[...] = jnp.dot(h_ref[...], w_ref[...], preferred_element_type=F32)


def _matmul_kernel(h_ref, w_ref, o_ref):
    o_ref[...] = jnp.dot(h_ref[...], w_ref[...], preferred_element_type=F32)


def _inproj(x2d, g_mix, w_lo, w_hi, w_if, b_if):
    m, d = x2d.shape
    tm = min(m, 1024)
    tn = 1024
    proj_lo, gates, h = pl.pallas_call(
        _inproj_lo_kernel,
        grid=(m // tm, w_lo.shape[1] // tn),
        in_specs=[
            pl.BlockSpec((tm, d), lambda i, j: (i, 0)),
            pl.BlockSpec((1, d), lambda i, j: (0, 0)),
            pl.BlockSpec((d, tn), lambda i, j: (0, j)),
            pl.BlockSpec((d, LANES), lambda i, j: (0, 0)),
            pl.BlockSpec((1, LANES), lambda i, j: (0, 0)),
        ],
        out_specs=[
            pl.BlockSpec((tm, tn), lambda i, j: (i, j)),
            pl.BlockSpec((tm, LANES), lambda i, j: (i, 0)),
            pl.BlockSpec((tm, d), lambda i, j: (i, 0)),
        ],
        out_shape=[jax.ShapeDtypeStruct((m, w_lo.shape[1]), F32), jax.ShapeDtypeStruct((m, LANES), F32),
                   jax.ShapeDtypeStruct((m, d), BF16)],
        compiler_params=_params(("parallel", "arbitrary")),
        name="inproj_lo",
    )(x2d, g_mix, w_lo, w_if, b_if)
    proj_hi = pl.pallas_call(
        _matmul_kernel,
        grid=(m // tm, w_hi.shape[1] // tn),
        in_specs=[
            pl.BlockSpec((tm, d), lambda i, j: (i, 0)),
            pl.BlockSpec((d, tn), lambda i, j: (0, j)),
        ],
        out_specs=pl.BlockSpec((tm, tn), lambda i, j: (i, j)),
        out_shape=jax.ShapeDtypeStruct((m, w_hi.shape[1]), F32),
        compiler_params=_params(("parallel", "arbitrary")),
        name="inproj_hi",
    )(h, w_hi)
    return proj_lo, proj_hi, gates


def _mlstm_kernel(qk_ref, v_ref, o_ref, gates_ref, convprev_ref, c0_ref, n0_ref, m0_ref,
                  wconv_ref, bconv_ref, gnorm_ref,
                  hmg_ref, newconv_ref, c1_ref, n1_ref, m1_ref,
                  xpad, c_scr, n_scr, m_scr, *, L, dm, dh, cps):
    c = pl.program_id(1)
    nc = pl.num_programs(1)
    hist0 = CONV_PAD - (CONV_W - 1)

    @pl.when(c == 0)
    def _():
        xpad[0:hist0, :] = jnp.zeros((hist0, xpad.shape[1]), F32)
        xpad[hist0:CONV_PAD, :] = convprev_ref[...]
        c_scr[...] = c0_ref[...]
        n_scr[...] = n0_ref[...]
        m_scr[...] = m0_ref[...]

    rr = lax.broadcasted_iota(jnp.int32, (L, L), 0)
    cc = lax.broadcasted_iota(jnp.int32, (L, L), 1)
    causal = rr >= cc
    tril = causal.astype(F32)
    eye = (lax.broadcasted_iota(jnp.int32, (8, LANES), 0)
           == lax.broadcasted_iota(jnp.int32, (8, LANES), 1)).astype(F32)
    heads = [slice(h * dh, (h + 1) * dh) for h in range(H_M)]
    k_scale = dh ** -0.5

    for j in range(cps):
        cr = slice(j * L, (j + 1) * L)
        x0 = CONV_PAD + j * L
        xpad[x0:x0 + L, :] = qk_ref[cr, :]
        win = xpad[x0 - CONV_PAD:x0 + L, :]
        conv = bconv_ref[...]
        for i in range(CONV_W):
            d = CONV_W - 1 - i
            tap = pltpu.roll(win, d, 0) if d else win
            conv = conv + tap[CONV_PAD:, :] * wconv_ref[i:i + 1, :]
        qk = conv * _sigmoid(conv)
        q = qk[:, :dm]
        k = qk[:, dm:]
        qb = q.astype(BF16)
        kb = k.astype(BF16)
        vv = v_ref[cr, :]
        vb = vv.astype(BF16)

        g = gates_ref[cr, :]
        col = lax.broadcasted_iota(jnp.int32, g.shape, 1)
        lf = jnp.minimum(g, 0.0) - jnp.log1p(jnp.exp(-jnp.abs(g)))
        x_cols = jnp.where(col < H_M, g, lf)
        cum = jnp.dot(tril, x_cols, preferred_element_type=F32, precision=HIGHEST)
        xc = jnp.where(col < H_M, g, cum)
        rows = lax.dot_general(eye, xc, NT_DIMS, preferred_element_type=F32, precision=HIGHEST)

        m_all = m_scr[...]
        e_s, sc_inter, inv_floor, decay, wgt, m_new = [], [], [], [], [], []
        for h in range(H_M):
            bcol = cum[:, H_M + h:H_M + h + 1]
            igcol = g[:, h:h + 1]
            brow = rows[H_M + h:H_M + h + 1, :]
            igrow = rows[h:h + 1, :]
            m_prev = m_all[:, h:h + 1]
            dlog = jnp.where(causal, bcol - brow + igrow, -jnp.inf)
            m_inter = bcol + m_prev
            m_t = jnp.maximum(m_inter, jnp.max(dlog, axis=-1, keepdims=True))
            e_s.append(jnp.exp(dlog - m_t) * k_scale)
            sc_inter.append(jnp.exp(m_inter - m_t))
            inv_floor.append(jnp.exp(-m_t))
            mn = m_t[L - 1:L, :]
            b_last = bcol[L - 1:L, :]
            m_new.append(mn)
            decay.append(jnp.exp(b_last + m_prev - mn))
            wgt.append(jnp.exp(b_last - bcol + igcol - mn) * k_scale)

        qn = []
        for h, sl in enumerate(heads):
            n_old = n_scr[h:h + 1, :]
            qn.append(jnp.sum(q[:, sl] * n_old, axis=-1, keepdims=True))
            n_scr[h:h + 1, :] = decay[h] * n_old + jnp.sum(wgt[h] * k[:, sl], axis=0, keepdims=True)
            m_scr[:, h:h + 1] = m_new[h]

        c_old = [c_scr[h] for h in range(H_M)]
        s_raw = [lax.dot_general(qb[:, sl], kb[:, sl], NT_DIMS, preferred_element_type=F32) for sl in heads]
        qc = [lax.dot_general(qb[:, sl], c_old[h].astype(BF16), NT_DIMS, preferred_element_type=F32)
              for h, sl in enumerate(heads)]
        upd = [lax.dot_general((vv[:, sl] * wgt[h]).astype(BF16), kb[:, sl], TN_DIMS, preferred_element_type=F32)
               for h, sl in enumerate(heads)]
        for h in range(H_M):
            c_scr[h] = decay[h] * c_old[h] + upd[h]
        s = [s_raw[h] * e_s[h] for h in range(H_M)]
        pv = [jnp.dot(s[h].astype(BF16), vb[:, sl], preferred_element_type=F32) for h, sl in enumerate(heads)]
        oo = o_ref[cr, :]
        for h, sl in enumerate(heads):
            num = sc_inter[h] * qc[h] + pv[h]
            den = sc_inter[h] * qn[h] + jnp.sum(s[h], axis=-1, keepdims=True)
            hout = num / jnp.maximum(jnp.abs(den), inv_floor[h])
            y = _rms(hout, gnorm_ref[:, sl]) * _sigmoid(oo[:, sl])
            hmg_ref[cr, sl] = y.astype(BF16)

    x_end = CONV_PAD + cps * L
    hist = xpad[x_end - (CONV_W - 1):x_end, :]
    xpad[hist0:CONV_PAD, :] = hist

    @pl.when(c == nc - 1)
    def _():
        newconv_ref[...] = hist
        c1_ref[...] = c_scr[...]
        n1_ref[...] = n_scr[...]
        m1_ref[...] = m_scr[...]


def _mlstm(proj, gates, conv_prev, c0, n0, m0, w_conv, b_conv, g_mnorm, *, B, T):
    L = MLSTM_CHUNK if T % MLSTM_CHUNK == 0 else min(CHUNK, T)
    nc = T // L
    dm = c0.shape[1] * c0.shape[2]
    dh = c0.shape[2]
    cps = MLSTM_CPS if nc % MLSTM_CPS == 0 else 1
    ns = nc // cps
    row = lambda b, c: b * ns + c
    kern = functools.partial(_mlstm_kernel, L=L, dm=dm, dh=dh, cps=cps)
    return pl.pallas_call(
        kern,
        grid=(B, ns),
        in_specs=[
            pl.BlockSpec((cps * L, 2 * dm), lambda b, c: (row(b, c), 0)),
            pl.BlockSpec((cps * L, dm), lambda b, c: (row(b, c), 2)),
            pl.BlockSpec((cps * L, dm), lambda b, c: (row(b, c), 3)),
            pl.BlockSpec((cps * L, LANES), lambda b, c: (row(b, c), 0)),
            pl.BlockSpec((None, CONV_W - 1, 2 * dm), lambda b, c: (b, 0, 0)),
            pl.BlockSpec((None, H_M, dh, dh), lambda b, c: (b, 0, 0, 0)),
            pl.BlockSpec((None, H_M, dh), lambda b, c: (b, 0, 0)),
            pl.BlockSpec((None, 1, H_M), lambda b, c: (b, 0, 0)),
            pl.BlockSpec((CONV_W, 2 * dm), lambda b, c: (0, 0)),
            pl.BlockSpec((1, 2 * dm), lambda b, c: (0, 0)),
            pl.BlockSpec((1, dm), lambda b, c: (0, 0)),
        ],
        out_specs=[
            pl.BlockSpec((cps * L, dm), lambda b, c: (row(b, c), 0)),
            pl.BlockSpec((None, CONV_W - 1, 2 * dm), lambda b, c: (b, 0, 0)),
            pl.BlockSpec((None, H_M, dh, dh), lambda b, c: (b, 0, 0, 0)),
            pl.BlockSpec((None, H_M, dh), lambda b, c: (b, 0, 0)),
            pl.BlockSpec((None, 1, H_M), lambda b, c: (b, 0, 0)),
        ],
        out_shape=[
            jax.ShapeDtypeStruct((B * T, dm), BF16),
            jax.ShapeDtypeStruct((B, CONV_W - 1, 2 * dm), F32),
            jax.ShapeDtypeStruct((B, H_M, dh, dh), F32),
            jax.ShapeDtypeStruct((B, H_M, dh), F32),
            jax.ShapeDtypeStruct((B, 1, H_M), F32),
        ],
        scratch_shapes=[
            pltpu.VMEM((CONV_PAD + cps * L, 2 * dm), F32),
            pltpu.VMEM((H_M, dh, dh), F32),
            pltpu.VMEM((H_M, dh), F32),
            pltpu.VMEM((1, H_M), F32),
        ],
        compiler_params=_params(("parallel", "arbitrary")),
        name="mlstm",
    )(proj, proj, proj, gates, conv_prev, c0, n0, m0, w_conv, b_conv, g_mnorm)


def _hgrn_factors(qq, kk, a, qa, sub, nb):
    L = a.shape[0]
    refs = [jnp.zeros_like(a[0:1, :])] + [a[i * sub - 1:i * sub, :] for i in range(1, nb)]
    qs = []
    ks = []
    for i in range(nb):
        rows = slice(i * sub, (i + 1) * sub)
        nk = (i + 1) * sub
        q_i = qa[rows, :] if i == 0 else (qq[rows, :] * jnp.exp(a[rows, :] - refs[i])).astype(BF16)
        k_i = (kk[:nk, :] * jnp.exp(jnp.minimum(refs[i] - a[:nk, :], SAFE_SPAN))).astype(BF16)
        zq = lambda n: jnp.zeros((n, q_i.shape[1]), BF16)
        qs.append(jnp.concatenate([z for z in (zq(i * sub), q_i, zq(L - nk)) if z.shape[0]], axis=0))
        ks.append(jnp.concatenate([z for z in (k_i, zq(L - nk)) if z.shape[0]], axis=0))
    return refs, qs, ks


def _hgrn_scores(qs, ks, sl, L):
    lhs = jnp.concatenate([q[:, sl] for q in qs], axis=1) if len(qs) > 1 else qs[0][:, sl]
    rhs = jnp.concatenate([k[:, sl] for k in ks], axis=1) if len(ks) > 1 else ks[0][:, sl]
    return lax.dot_general(lhs, rhs, NT_DIMS, preferred_element_type=F32)


def _hgrn_kernel(f_ref, i_ref, q_ref, g_ref, lbl_ref, s0_ref, gnorm_ref,
                 ohg_ref, s1_ref,
                 st_scr, a_scr, k_scr, sc_scr, o_scr, in_scr, *, L, nh, cps):
    c = pl.program_id(1)
    nc = pl.num_programs(1)
    sub = min(SUB, L)
    nb = L // sub

    @pl.when(c == 0)
    def _():
        for h in range(nh):
            st_scr[h] = s0_ref[h].T

    lbl = lbl_ref[...]
    mx = jnp.max(lbl, axis=0, keepdims=True)
    ex = jnp.exp(lbl - mx)
    lb = ex[0:1, :] / jnp.sum(ex, axis=0, keepdims=True)
    gnorm = gnorm_ref[...]
    rr = lax.broadcasted_iota(jnp.int32, (L, L), 0)
    cc = lax.broadcasted_iota(jnp.int32, (L, L), 1)
    causal = cc <= rr

    def finish(rows):
        gg = g_ref[rows, :]
        ohg_ref[rows, :] = (_rms(o_scr[rows, :], gnorm) * (gg * _sigmoid(gg))).astype(BF16)

    span = None
    for j in range(cps):
        cr = slice(j * L, (j + 1) * L)
        f = lb + (1.0 - lb) * _sigmoid(f_ref[cr, :])
        lf = jnp.log(f)
        kk = 1.0 - f
        qq = q_ref[cr, :]
        vb = i_ref[cr, :].astype(BF16)

        ridx = lax.broadcasted_iota(jnp.int32, lf.shape, 0)
        a = lf
        sh = 1
        while sh < L:
            a = a + jnp.where(ridx >= sh, pltpu.roll(a, sh, 0), 0.0)
            sh *= 2
        a_scr[cr, :] = a
        k_scr[cr, :] = kk

        a_last = a[L - 1:L, :]
        qa = (qq * jnp.exp(a)).astype(BF16)
        kdec = (kk * jnp.exp(a_last - a)).astype(BF16)
        dec_last = jnp.exp(a_last)
        refs, qs, ks = _hgrn_factors(qq, kk, a, qa, sub, nb)
        for i in range(nb):
            sp = refs[i] - a[(i + 1) * sub - 1:(i + 1) * sub, :]
            span = sp if span is None else jnp.maximum(span, sp)

        heads = [slice(h * DH_H, (h + 1) * DH_H) for h in range(nh)]
        st_old = [st_scr[h] for h in range(nh)]
        inter = [lax.dot_general(qa[:, sl], st_old[h].astype(BF16), NT_DIMS, preferred_element_type=F32)
                 for h, sl in enumerate(heads)]
        raw = [_hgrn_scores(qs, ks, sl, L) for sl in heads]
        upd = [lax.dot_general(vb[:, sl], kdec[:, sl], TN_DIMS, preferred_element_type=F32) for sl in heads]
        for h, sl in enumerate(heads):
            st_scr[h] = st_old[h] * dec_last[:, sl] + upd[h]
            in_scr[cr, sl] = inter[h]
        blk = [jnp.where(causal, r, 0.0).astype(BF16) for r in raw]
        for h, sl in enumerate(heads):
            o_scr[cr, sl] = inter[h] + jnp.dot(blk[h], vb[:, sl], preferred_element_type=F32)
        finish(cr)

    use_exact = jnp.logical_not(jnp.max(span) <= SAFE_SPAN)

    @pl.when(use_exact)
    def _():
        sc_scr[...] = jnp.zeros_like(sc_scr)
        lane = lax.broadcasted_iota(jnp.int32, (sub, LANES), 1)
        in_block = jnp.logical_and(causal, (cc // sub) == (rr // sub))

        def fix_chunk(j, carry):
            base = pl.multiple_of(j * L, L)

            def diag_body(s, carry2):
                r0 = pl.multiple_of((s // sub) * sub, sub)
                a_blk = a_scr[pl.ds(base + r0, sub), :]
                q_blk = q_ref[pl.ds(base + r0, sub), :]
                a_s = a_scr[pl.ds(base + s, 1), :]
                k_s = k_scr[pl.ds(base + s, 1), :]
                p = q_blk * jnp.exp(jnp.minimum(a_blk - a_s, 0.0)) * k_s
                for h in range(nh):
                    colv = jnp.sum(p[:, h * DH_H:(h + 1) * DH_H], axis=-1, keepdims=True)
                    old = sc_scr[h, pl.ds(r0, sub), :]
                    sc_scr[h, pl.ds(r0, sub), :] = jnp.where(lane == s, colv, old)
                return carry2

            lax.fori_loop(0, L, diag_body, 0)

            cr = pl.ds(base, L)
            a = a_scr[cr, :]
            kk = k_scr[cr, :]
            qq = q_ref[cr, :]
            vb = i_ref[cr, :].astype(BF16)
            qa = (qq * jnp.exp(a)).astype(BF16)
            _, qs, ks = _hgrn_factors(qq, kk, a, qa, sub, nb)
            for h in range(nh):
                sl = slice(h * DH_H, (h + 1) * DH_H)
                blk = jnp.where(causal, _hgrn_scores(qs, ks, sl, L), 0.0)
                blk = jnp.where(in_block, sc_scr[h][:, :L], blk).astype(BF16)
                o_scr[cr, sl] = in_scr[cr, sl] + jnp.dot(blk, vb[:, sl], preferred_element_type=F32)
            finish(cr)
            return carry

        lax.fori_loop(0, cps, fix_chunk, 0)

    @pl.when(c == nc - 1)
    def _():
        for h in range(nh):
            s1_ref[h] = st_scr[h].T


def _hgrn(proj, lb_logits, s0, g_hnorm, *, B, T):
    L = min(CHUNK, T)
    nc = T // L
    cps = HGRN_CPS if nc % HGRN_CPS == 0 else 1
    ns = nc // cps
    nh = s0.shape[1]
    dhh = nh * DH_H
    row = lambda b, c: b * ns + c
    kern = functools.partial(_hgrn_kernel, L=L, nh=nh, cps=cps)
    return pl.pallas_call(
        kern,
        grid=(B, ns),
        in_specs=[
            pl.BlockSpec((cps * L, dhh), lambda b, c: (row(b, c), 0)),
            pl.BlockSpec((cps * L, dhh), lambda b, c: (row(b, c), 1)),
            pl.BlockSpec((cps * L, dhh), lambda b, c: (row(b, c), 2)),
            pl.BlockSpec((cps * L, dhh), lambda b, c: (row(b, c), 3)),
            pl.BlockSpec((2, dhh), lambda b, c: (0, 0)),
            pl.BlockSpec((None, nh, DH_H, DH_H), lambda b, c: (b, 0, 0, 0)),
            pl.BlockSpec((1, dhh), lambda b, c: (0, 0)),
        ],
        out_specs=[
            pl.BlockSpec((cps * L, dhh), lambda b, c: (row(b, c), 0)),
            pl.BlockSpec((None, nh, DH_H, DH_H), lambda b, c: (b, 0, 0, 0)),
        ],
        out_shape=[
            jax.ShapeDtypeStruct((B * T, dhh), BF16),
            jax.ShapeDtypeStruct((B, nh, DH_H, DH_H), F32),
        ],
        scratch_shapes=[
            pltpu.VMEM((nh, DH_H, DH_H), F32),
            pltpu.VMEM((cps * L, dhh), F32),
            pltpu.VMEM((cps * L, dhh), F32),
            pltpu.VMEM((nh, L, LANES), F32),
            pltpu.VMEM((cps * L, dhh), F32),
            pltpu.VMEM((cps * L, dhh), F32),
        ],
        compiler_params=_params(("parallel", "arbitrary")),
        name="hgrn",
    )(proj, proj, proj, proj, lb_logits, s0, g_hnorm)


def _merge_kernel(hmg_ref, ohg_ref, ga_ref, gb_ref, x_ref, wa_ref, wb_ref, wo_ref, gffn_ref, x1_ref, h2_ref):
    ya = jnp.dot(hmg_ref[...], wa_ref[...], preferred_element_type=F32)
    yb = jnp.dot(ohg_ref[...], wb_ref[...], preferred_element_type=F32)
    u = _sigmoid(ga_ref[...]) * ya + _sigmoid(gb_ref[...]) * yb
    x1 = x_ref[...] + jnp.dot(u.astype(BF16), wo_ref[...], preferred_element_type=F32)
    x1_ref[...] = x1
    h2_ref[...] = _rms(x1, gffn_ref[...]).astype(BF16)


def _merge(hmg, ohg, proj, x2d, w_a, w_b, w_o, g_ffn):
    m, d = x2d.shape
    dm = hmg.shape[1]
    tm = min(m, 256)
    const = lambda i: (0, 0)
    return pl.pallas_call(
        _merge_kernel,
        grid=(m // tm,),
        in_specs=[
            pl.BlockSpec((tm, dm), lambda i: (i, 0)),
            pl.BlockSpec((tm, dm), lambda i: (i, 0)),
            pl.BlockSpec((tm, d), lambda i: (i, 2)),
            pl.BlockSpec((tm, d), lambda i: (i, 3)),
            pl.BlockSpec((tm, d), lambda i: (i, 0)),
            pl.BlockSpec((dm, d), const, pipeline_mode=pl.Buffered(1)),
            pl.BlockSpec((dm, d), const, pipeline_mode=pl.Buffered(1)),
            pl.BlockSpec((d, d), const, pipeline_mode=pl.Buffered(1)),
            pl.BlockSpec((1, d), const),
        ],
        out_specs=[pl.BlockSpec((tm, d), lambda i: (i, 0)), pl.BlockSpec((tm, d), lambda i: (i, 0))],
        out_shape=[jax.ShapeDtypeStruct((m, d), F32), jax.ShapeDtypeStruct((m, d), BF16)],
        compiler_params=_params(("parallel",)),
        name="merge",
    )(hmg, ohg, proj, proj, x2d, w_a, w_b, w_o, g_ffn)


def _ffn_kernel(x1_ref, h2_ref, w1_ref, w2_ref, gfin_ref, y_ref, acc_scr):
    j = pl.program_id(1)

    h = h2_ref[...]
    tf = w1_ref.shape[1]
    pieces = [slice(p, p + FFN_PIECE) for p in range(0, tf, FFN_PIECE)]
    ts = []
    for ps in pieces:
        t = jnp.maximum(jnp.dot(h, w1_ref[:, ps], preferred_element_type=F32), 0.0)
        ts.append((t * t).astype(BF16))
    acc = jnp.where(j > 0, acc_scr[...], 0.0)
    for ps, t in zip(pieces, ts):
        acc = acc + jnp.dot(t, w2_ref[ps, :], preferred_element_type=F32)
    acc_scr[...] = acc

    @pl.when(j == pl.num_programs(1) - 1)
    def _():
        y_ref[...] = _rms(x1_ref[...] + acc_scr[...], gfin_ref[...])


def _ffn(x1, h2, w1, w2, g_final):
    m, d = x1.shape
    dff = w1.shape[1]
    tm = min(m, 512)
    tf = 1024
    return pl.pallas_call(
        _ffn_kernel,
        grid=(m // tm, dff // tf),
        in_specs=[
            pl.BlockSpec((tm, d), lambda i, j: (i, 0)),
            pl.BlockSpec((tm, d), lambda i, j: (i, 0)),
            pl.BlockSpec((d, tf), lambda i, j: (0, j)),
            pl.BlockSpec((tf, d), lambda i, j: (j, 0)),
            pl.BlockSpec((1, d), lambda i, j: (0, 0)),
        ],
        out_specs=pl.BlockSpec((tm, d), lambda i, j: (i, 0)),
        out_shape=jax.ShapeDtypeStruct((m, d), F32),
        scratch_shapes=[pltpu.VMEM((tm, d), F32)],
        compiler_params=_params(("parallel", "arbitrary")),
        name="ffn",
    )(x1, h2, w1, w2, g_final)


def _group(x, conv_prev, c0, n0, m0, s0, wts):
    B, T, d = x.shape
    x2d = x.reshape(B * T, d)
    proj_lo, proj_hi, gates = _inproj(x2d, wts["g_mix"], wts["w_lo"], wts["w_hi"], wts["w_if"], wts["b_if"])
    hmg, new_conv, c1, n1, m1 = _mlstm(proj_lo, gates, conv_prev, c0, n0, m0.reshape(B, 1, -1),
                                       wts["w_conv"], wts["b_conv"], wts["g_mnorm"], B=B, T=T)
    ohg, s1 = _hgrn(proj_hi, wts["lb_logits"], s0, wts["g_hnorm"], B=B, T=T)
    x1, h2 = _merge(hmg, ohg, proj_hi, x2d, wts["w_a"], wts["w_b"], wts["w_o"], wts["g_ffn"])
    y = _ffn(x1, h2, wts["w_ff1"], wts["w_ff2"], wts["g_final"])
    return (y.reshape(B, T, d), new_conv[None], c1[None], n1[None], m1.reshape(1, B, -1), s1[None])


def kernel(x_prompt, x_sample, cache_mlstm_conv, state_mlstm_C, state_mlstm_n, state_mlstm_m, state_hgrn_S,
           g_mix, w_in, b_if, w_conv, b_conv, g_mnorm, g_hnorm, hgrn_lb_logits, w_branch_a, w_branch_b,
           w_out, g_ffn, w_ff1, w_ff2, g_final):
    depth = w_in.shape[0]
    assert depth == 1, "single-layer trunk only"
    d = x_prompt.shape[-1]
    dm = w_branch_a.shape[1]
    dhh = w_branch_b.shape[1]
    nh_h = dhh // DH_H
    dh_m = dm // H_M
    assert hgrn_lb_logits.shape[0] == depth + 1
    assert 2 * dm == d and dhh == dm

    w = w_in[0]
    n_if = 2 * H_M
    p_if = 4 * dm
    w_lo = w[:, :p_if].astype(BF16)
    w_hi = _shifted_columns_bf16(w, p_if + n_if, w.shape[1] - p_if - n_if)
    w_if = jnp.pad(w[:, p_if:p_if + n_if], ((0, 0), (0, LANES - n_if))).astype(BF16)
    bif = jnp.pad(b_if[0].astype(F32), (0, LANES - n_if)).reshape(1, LANES)
    wts = dict(
        g_mix=g_mix[0].reshape(1, d), w_lo=w_lo, w_hi=w_hi, w_if=w_if, b_if=bif,
        w_conv=w_conv[0], b_conv=b_conv[0].reshape(1, -1), g_mnorm=g_mnorm[0].reshape(1, dm),
        lb_logits=hgrn_lb_logits.astype(F32), g_hnorm=g_hnorm[0].reshape(1, dhh),
        w_a=w_branch_a[0].astype(BF16), w_b=w_branch_b[0].astype(BF16), w_o=w_out[0].astype(BF16),
        g_ffn=g_ffn[0].reshape(1, d), w_ff1=w_ff1[0].astype(BF16), w_ff2=w_ff2[0].astype(BF16),
        g_final=g_final.reshape(1, d),
    )

    bp = x_prompt.shape[0]
    zeros = lambda *s: jnp.zeros(s, F32)
    outs_p = _group(x_prompt, zeros(bp, CONV_W - 1, 2 * dm), zeros(bp, H_M, dh_m, dh_m), zeros(bp, H_M, dh_m),
                    zeros(bp, H_M), zeros(bp, nh_h, DH_H, DH_H), wts)
    outs_s = _group(x_sample, cache_mlstm_conv[0], state_mlstm_C[0].astype(F32), state_mlstm_n[0].astype(F32),
                    state_mlstm_m[0].astype(F32), state_hgrn_S[0].astype(F32), wts)
    return (outs_p[0], outs_s[0]) + outs_p[1:] + outs_s[1:]
```

```python
import functools

import jax
import jax.numpy as jnp
from jax import lax
from jax.experimental import pallas as pl
from jax.experimental.pallas import tpu as pltpu

F32 = jnp.float32
BF16 = jnp.bfloat16
HIGHEST = lax.Precision.HIGHEST

EPS = 1e-6
CHUNK = 64
MLSTM_CHUNK = 128
MLSTM_CPS = 2
CONV_W = 4
H_M = 4
DH_H = 128
SUB = 32
HGRN_CPS = 4
SAFE_SPAN = 60.0
LANES = 128
CONV_PAD = 8
FFN_PIECE = 256
VMEM_LIMIT = 56 * 1024 * 1024

NT_DIMS = (((1,), (1,)), ((), ()))
TN_DIMS = (((0,), (0,)), ((), ()))


def _rms(x, g):
    return x * lax.rsqrt(jnp.mean(x * x, axis=-1, keepdims=True) + EPS) * g


def _sigmoid(x):
    return 0.5 * jnp.tanh(0.5 * x) + 0.5


def _params(sem):
    return pltpu.CompilerParams(dimension_semantics=sem, vmem_limit_bytes=VMEM_LIMIT)


def _norm_kernel(x_ref, g_ref, wif_ref, bif_ref, h_ref, gates_ref):
    hb = _rms(x_ref[...], g_ref[...]).astype(BF16)
    h_ref[...] = hb
    gates_ref[...] = jnp.dot(hb, wif_ref[...], preferred_element_type=F32) + bif_ref[...]


def _matmul_kernel(h_ref, w_ref, o_ref):
    o_ref[...] = jnp.dot(h_ref[...], w_ref[...], preferred_element_type=F32)


def _matmul(h, w, name):
    m, d = h.shape
    tm = min(m, 1024)
    tn = 1024
    return pl.pallas_call(
        _matmul_kernel,
        grid=(m // tm, w.shape[1] // tn),
        in_specs=[
            pl.BlockSpec((tm, d), lambda i, j: (i, 0)),
            pl.BlockSpec((d, tn), lambda i, j: (0, j)),
        ],
        out_specs=pl.BlockSpec((tm, tn), lambda i, j: (i, j)),
        out_shape=jax.ShapeDtypeStruct((m, w.shape[1]), F32),
        compiler_params=_params(("parallel", "arbitrary")),
        name=name,
    )(h, w)


def _inproj(x2d, g_mix, w_lo, w_hi, w_if, b_if):
    m, d = x2d.shape
    tm = min(m, 512)
    h, gates = pl.pallas_call(
        _norm_kernel,
        grid=(m // tm,),
        in_specs=[
            pl.BlockSpec((tm, d), lambda i: (i, 0)),
            pl.BlockSpec((1, d), lambda i: (0, 0)),
            pl.BlockSpec((d, LANES), lambda i: (0, 0)),
            pl.BlockSpec((1, LANES), lambda i: (0, 0)),
        ],
        out_specs=[pl.BlockSpec((tm, d), lambda i: (i, 0)), pl.BlockSpec((tm, LANES), lambda i: (i, 0))],
        out_shape=[jax.ShapeDtypeStruct((m, d), BF16), jax.ShapeDtypeStruct((m, LANES), F32)],
        compiler_params=_params(("parallel",)),
        name="norm",
    )(x2d, g_mix, w_if, b_if)
    return _matmul(h, w_lo, "inproj_lo"), _matmul(h, w_hi, "inproj_hi"), gates


def _mlstm_kernel(qk_ref, v_ref, o_ref, gates_ref, convprev_ref, c0_ref, n0_ref, m0_ref,
                  wconv_ref, bconv_ref, gnorm_ref,
                  hmg_ref, newconv_ref, c1_ref, n1_ref, m1_ref,
                  xpad, c_scr, n_scr, m_scr, *, L, dm, dh, cps):
    c = pl.program_id(1)
    nc = pl.num_programs(1)
    hist0 = CONV_PAD - (CONV_W - 1)

    @pl.when(c == 0)
    def _():
        xpad[0:hist0, :] = jnp.zeros((hist0, xpad.shape[1]), F32)
        xpad[hist0:CONV_PAD, :] = convprev_ref[...]
        c_scr[...] = c0_ref[...]
        n_scr[...] = n0_ref[...]
        m_scr[...] = m0_ref[...]

    rr = lax.broadcasted_iota(jnp.int32, (L, L), 0)
    cc = lax.broadcasted_iota(jnp.int32, (L, L), 1)
    causal = rr >= cc
    tril = causal.astype(F32)
    eye = (lax.broadcasted_iota(jnp.int32, (8, LANES), 0)
           == lax.broadcasted_iota(jnp.int32, (8, LANES), 1)).astype(F32)
    heads = [slice(h * dh, (h + 1) * dh) for h in range(H_M)]
    k_scale = dh ** -0.5

    for j in range(cps):
        cr = slice(j * L, (j + 1) * L)
        x0 = CONV_PAD + j * L
        xpad[x0:x0 + L, :] = qk_ref[cr, :]
        win = xpad[x0 - CONV_PAD:x0 + L, :]
        conv = bconv_ref[...]
        for i in range(CONV_W):
            d = CONV_W - 1 - i
            tap = pltpu.roll(win, d, 0) if d else win
            conv = conv + tap[CONV_PAD:, :] * wconv_ref[i:i + 1, :]
        qk = conv * _sigmoid(conv)
        q = qk[:, :dm]
        k = qk[:, dm:]
        qb = q.astype(BF16)
        kb = k.astype(BF16)
        vv = v_ref[cr, :]
        vb = vv.astype(BF16)

        g = gates_ref[cr, :]
        col = lax.broadcasted_iota(jnp.int32, g.shape, 1)
        lf = jnp.minimum(g, 0.0) - jnp.log1p(jnp.exp(-jnp.abs(g)))
        x_cols = jnp.where(col < H_M, g, lf)
        cum = jnp.dot(tril, x_cols, preferred_element_type=F32, precision=HIGHEST)
        xc = jnp.where(col < H_M, g, cum)
        rows = lax.dot_general(eye, xc, NT_DIMS, preferred_element_type=F32, precision=HIGHEST)

        m_all = m_scr[...]
        e_s, sc_inter, inv_floor, decay, wgt, m_new = [], [], [], [], [], []
        for h in range(H_M):
            bcol = cum[:, H_M + h:H_M + h + 1]
            igcol = g[:, h:h + 1]
            brow = rows[H_M + h:H_M + h + 1, :]
            igrow = rows[h:h + 1, :]
            m_prev = m_all[:, h:h + 1]
            dlog = jnp.where(causal, bcol - brow + igrow, -jnp.inf)
            m_inter = bcol + m_prev
            m_t = jnp.maximum(m_inter, jnp.max(dlog, axis=-1, keepdims=True))
            e_s.append(jnp.exp(dlog - m_t) * k_scale)
            sc_inter.append(jnp.exp(m_inter - m_t))
            inv_floor.append(jnp.exp(-m_t))
            mn = m_t[L - 1:L, :]
            b_last = bcol[L - 1:L, :]
            m_new.append(mn)
            decay.append(jnp.exp(b_last + m_prev - mn))
            wgt.append(jnp.exp(b_last - bcol + igcol - mn) * k_scale)

        qn = []
        for h, sl in enumerate(heads):
            n_old = n_scr[h:h + 1, :]
            qn.append(jnp.sum(q[:, sl] * n_old, axis=-1, keepdims=True))
            n_scr[h:h + 1, :] = decay[h] * n_old + jnp.sum(wgt[h] * k[:, sl], axis=0, keepdims=True)
            m_scr[:, h:h + 1] = m_new[h]

        c_old = [c_scr[h] for h in range(H_M)]
        s_raw = [lax.dot_general(qb[:, sl], kb[:, sl], NT_DIMS, preferred_element_type=F32) for sl in heads]
        qc = [lax.dot_general(qb[:, sl], c_old[h].astype(BF16), NT_DIMS, preferred_element_type=F32)
              for h, sl in enumerate(heads)]
        upd = [lax.dot_general((vv[:, sl] * wgt[h]).astype(BF16), kb[:, sl], TN_DIMS, preferred_element_type=F32)
               for h, sl in enumerate(heads)]
        for h in range(H_M):
            c_scr[h] = decay[h] * c_old[h] + upd[h]
        s = [s_raw[h] * e_s[h] for h in range(H_M)]
        pv = [jnp.dot(s[h].astype(BF16), vb[:, sl], preferred_element_type=F32) for h, sl in enumerate(heads)]
        oo = o_ref[cr, :]
        for h, sl in enumerate(heads):
            num = sc_inter[h] * qc[h] + pv[h]
            den = sc_inter[h] * qn[h] + jnp.sum(s[h], axis=-1, keepdims=True)
            hout = num / jnp.maximum(jnp.abs(den), inv_floor[h])
            y = _rms(hout, gnorm_ref[:, sl]) * _sigmoid(oo[:, sl])
            hmg_ref[cr, sl] = y.astype(BF16)

    x_end = CONV_PAD + cps * L
    hist = xpad[x_end - (CONV_W - 1):x_end, :]
    xpad[hist0:CONV_PAD, :] = hist

    @pl.when(c == nc - 1)
    def _():
        newconv_ref[...] = hist
        c1_ref[...] = c_scr[...]
        n1_ref[...] = n_scr[...]
        m1_ref[...] = m_scr[...]


def _mlstm(proj, gates, conv_prev, c0, n0, m0, w_conv, b_conv, g_mnorm, *, B, T):
    L = MLSTM_CHUNK if T % MLSTM_CHUNK == 0 else min(CHUNK, T)
    nc = T // L
    dm = c0.shape[1] * c0.shape[2]
    dh = c0.shape[2]
    cps = MLSTM_CPS if nc % MLSTM_CPS == 0 else 1
    ns = nc // cps
    row = lambda b, c: b * ns + c
    kern = functools.partial(_mlstm_kernel, L=L, dm=dm, dh=dh, cps=cps)
    return pl.pallas_call(
        kern,
        grid=(B, ns),
        in_specs=[
            pl.BlockSpec((cps * L, 2 * dm), lambda b, c: (row(b, c), 0)),
            pl.BlockSpec((cps * L, dm), lambda b, c: (row(b, c), 2)),
            pl.BlockSpec((cps * L, dm), lambda b, c: (row(b, c), 3)),
            pl.BlockSpec((cps * L, LANES), lambda b, c: (row(b, c), 0)),
            pl.BlockSpec((None, CONV_W - 1, 2 * dm), lambda b, c: (b, 0, 0)),
            pl.BlockSpec((None, H_M, dh, dh), lambda b, c: (b, 0, 0, 0)),
            pl.BlockSpec((None, H_M, dh), lambda b, c: (b, 0, 0)),
            pl.BlockSpec((None, 1, H_M), lambda b, c: (b, 0, 0)),
            pl.BlockSpec((CONV_W, 2 * dm), lambda b, c: (0, 0)),
            pl.BlockSpec((1, 2 * dm), lambda b, c: (0, 0)),
            pl.BlockSpec((1, dm), lambda b, c: (0, 0)),
        ],
        out_specs=[
            pl.BlockSpec((cps * L, dm), lambda b, c: (row(b, c), 0)),
            pl.BlockSpec((None, CONV_W - 1, 2 * dm), lambda b, c: (b, 0, 0)),
            pl.BlockSpec((None, H_M, dh, dh), lambda b, c: (b, 0, 0, 0)),
            pl.BlockSpec((None, H_M, dh), lambda b, c: (b, 0, 0)),
            pl.BlockSpec((None, 1, H_M), lambda b, c: (b, 0, 0)),
        ],
        out_shape=[
            jax.ShapeDtypeStruct((B * T, dm), BF16),
            jax.ShapeDtypeStruct((B, CONV_W - 1, 2 * dm), F32),
            jax.ShapeDtypeStruct((B, H_M, dh, dh), F32),
            jax.ShapeDtypeStruct((B, H_M, dh), F32),
            jax.ShapeDtypeStruct((B, 1, H_M), F32),
        ],
        scratch_shapes=[
            pltpu.VMEM((CONV_PAD + cps * L, 2 * dm), F32),
            pltpu.VMEM((H_M, dh, dh), F32),
            pltpu.VMEM((H_M, dh), F32),
            pltpu.VMEM((1, H_M), F32),
        ],
        compiler_params=_params(("parallel", "arbitrary")),
        name="mlstm",
    )(proj, proj, proj, gates, conv_prev, c0, n0, m0, w_conv, b_conv, g_mnorm)


def _hgrn_factors(qq, kk, a, qa, sub, nb):
    L = a.shape[0]
    refs = [jnp.zeros_like(a[0:1, :])] + [a[i * sub - 1:i * sub, :] for i in range(1, nb)]
    qs = []
    ks = []
    for i in range(nb):
        rows = slice(i * sub, (i + 1) * sub)
        nk = (i + 1) * sub
        q_i = qa[rows, :] if i == 0 else (qq[rows, :] * jnp.exp(a[rows, :] - refs[i])).astype(BF16)
        k_i = (kk[:nk, :] * jnp.exp(jnp.minimum(refs[i] - a[:nk, :], SAFE_SPAN))).astype(BF16)
        zq = lambda n: jnp.zeros((n, q_i.shape[1]), BF16)
        qs.append(jnp.concatenate([z for z in (zq(i * sub), q_i, zq(L - nk)) if z.shape[0]], axis=0))
        ks.append(jnp.concatenate([z for z in (k_i, zq(L - nk)) if z.shape[0]], axis=0))
    return refs, qs, ks


def _hgrn_scores(qs, ks, sl, L):
    lhs = jnp.concatenate([q[:, sl] for q in qs], axis=1) if len(qs) > 1 else qs[0][:, sl]
    rhs = jnp.concatenate([k[:, sl] for k in ks], axis=1) if len(ks) > 1 else ks[0][:, sl]
    return lax.dot_general(lhs, rhs, NT_DIMS, preferred_element_type=F32)


def _hgrn_kernel(f_ref, i_ref, q_ref, g_ref, lbl_ref, s0_ref, gnorm_ref,
                 ohg_ref, s1_ref,
                 st_scr, a_scr, k_scr, sc_scr, o_scr, in_scr, *, L, nh, cps):
    c = pl.program_id(1)
    nc = pl.num_programs(1)
    sub = min(SUB, L)
    nb = L // sub

    @pl.when(c == 0)
    def _():
        for h in range(nh):
            st_scr[h] = s0_ref[h].T

    lbl = lbl_ref[...]
    mx = jnp.max(lbl, axis=0, keepdims=True)
    ex = jnp.exp(lbl - mx)
    lb = ex[0:1, :] / jnp.sum(ex, axis=0, keepdims=True)
    gnorm = gnorm_ref[...]
    rr = lax.broadcasted_iota(jnp.int32, (L, L), 0)
    cc = lax.broadcasted_iota(jnp.int32, (L, L), 1)
    causal = cc <= rr

    def finish(rows):
        gg = g_ref[rows, :]
        ohg_ref[rows, :] = (_rms(o_scr[rows, :], gnorm) * (gg * _sigmoid(gg))).astype(BF16)

    span = None
    for j in range(cps):
        cr = slice(j * L, (j + 1) * L)
        f = lb + (1.0 - lb) * _sigmoid(f_ref[cr, :])
        lf = jnp.log(f)
        kk = 1.0 - f
        qq = q_ref[cr, :]
        vb = i_ref[cr, :].astype(BF16)

        ridx = lax.broadcasted_iota(jnp.int32, lf.shape, 0)
        a = lf
        sh = 1
        while sh < L:
            a = a + jnp.where(ridx >= sh, pltpu.roll(a, sh, 0), 0.0)
            sh *= 2
        a_scr[cr, :] = a
        k_scr[cr, :] = kk

        a_last = a[L - 1:L, :]
        qa = (qq * jnp.exp(a)).astype(BF16)
        kdec = (kk * jnp.exp(a_last - a)).astype(BF16)
        dec_last = jnp.exp(a_last)
        refs, qs, ks = _hgrn_factors(qq, kk, a, qa, sub, nb)
        for i in range(nb):
            sp = refs[i] - a[(i + 1) * sub - 1:(i + 1) * sub, :]
            span = sp if span is None else jnp.maximum(span, sp)

        heads = [slice(h * DH_H, (h + 1) * DH_H) for h in range(nh)]
        st_old = [st_scr[h] for h in range(nh)]
        inter = [lax.dot_general(qa[:, sl], st_old[h].astype(BF16), NT_DIMS, preferred_element_type=F32)
                 for h, sl in enumerate(heads)]
        raw = [_hgrn_scores(qs, ks, sl, L) for sl in heads]
        upd = [lax.dot_general(vb[:, sl], kdec[:, sl], TN_DIMS, preferred_element_type=F32) for sl in heads]
        for h, sl in enumerate(heads):
            st_scr[h] = st_old[h] * dec_last[:, sl] + upd[h]
            in_scr[cr, sl] = inter[h]
        blk = [jnp.where(causal, r, 0.0).astype(BF16) for r in raw]
        for h, sl in enumerate(heads):
            o_scr[cr, sl] = inter[h] + jnp.dot(blk[h], vb[:, sl], preferred_element_type=F32)
        finish(cr)

    use_exact = jnp.logical_not(jnp.max(span) <= SAFE_SPAN)

    @pl.when(use_exact)
    def _():
        sc_scr[...] = jnp.zeros_like(sc_scr)
        lane = lax.broadcasted_iota(jnp.int32, (sub, LANES), 1)
        in_block = jnp.logical_and(causal, (cc // sub) == (rr // sub))

        def fix_chunk(j, carry):
            base = pl.multiple_of(j * L, L)

            def diag_body(s, carry2):
                r0 = pl.multiple_of((s // sub) * sub, sub)
                a_blk = a_scr[pl.ds(base + r0, sub), :]
                q_blk = q_ref[pl.ds(base + r0, sub), :]
                a_s = a_scr[pl.ds(base + s, 1), :]
                k_s = k_scr[pl.ds(base + s, 1), :]
                p = q_blk * jnp.exp(jnp.minimum(a_blk - a_s, 0.0)) * k_s
                for h in range(nh):
                    colv = jnp.sum(p[:, h * DH_H:(h + 1) * DH_H], axis=-1, keepdims=True)
                    old = sc_scr[h, pl.ds(r0, sub), :]
                    sc_scr[h, pl.ds(r0, sub), :] = jnp.where(lane == s, colv, old)
                return carry2

            lax.fori_loop(0, L, diag_body, 0)

            cr = pl.ds(base, L)
            a = a_scr[cr, :]
            kk = k_scr[cr, :]
            qq = q_ref[cr, :]
            vb = i_ref[cr, :].astype(BF16)
            qa = (qq * jnp.exp(a)).astype(BF16)
            _, qs, ks = _hgrn_factors(qq, kk, a, qa, sub, nb)
            for h in range(nh):
                sl = slice(h * DH_H, (h + 1) * DH_H)
                blk = jnp.where(causal, _hgrn_scores(qs, ks, sl, L), 0.0)
                blk = jnp.where(in_block, sc_scr[h][:, :L], blk).astype(BF16)
                o_scr[cr, sl] = in_scr[cr, sl] + jnp.dot(blk, vb[:, sl], preferred_element_type=F32)
            finish(cr)
            return carry

        lax.fori_loop(0, cps, fix_chunk, 0)

    @pl.when(c == nc - 1)
    def _():
        for h in range(nh):
            s1_ref[h] = st_scr[h].T


def _hgrn(proj, lb_logits, s0, g_hnorm, *, B, T):
    L = min(CHUNK, T)
    nc = T // L
    cps = HGRN_CPS if nc % HGRN_CPS == 0 else 1
    ns = nc // cps
    nh = s0.shape[1]
    dhh = nh * DH_H
    row = lambda b, c: b * ns + c
    kern = functools.partial(_hgrn_kernel, L=L, nh=nh, cps=cps)
    return pl.pallas_call(
        kern,
        grid=(B, ns),
        in_specs=[
            pl.BlockSpec((cps * L, dhh), lambda b, c: (row(b, c), 0)),
            pl.BlockSpec((cps * L, dhh), lambda b, c: (row(b, c), 1)),
            pl.BlockSpec((cps * L, dhh), lambda b, c: (row(b, c), 2)),
            pl.BlockSpec((cps * L, dhh), lambda b, c: (row(b, c), 3)),
            pl.BlockSpec((2, dhh), lambda b, c: (0, 0)),
            pl.BlockSpec((None, nh, DH_H, DH_H), lambda b, c: (b, 0, 0, 0)),
            pl.BlockSpec((1, dhh), lambda b, c: (0, 0)),
        ],
        out_specs=[
            pl.BlockSpec((cps * L, dhh), lambda b, c: (row(b, c), 0)),
            pl.BlockSpec((None, nh, DH_H, DH_H), lambda b, c: (b, 0, 0, 0)),
        ],
        out_shape=[
            jax.ShapeDtypeStruct((B * T, dhh), BF16),
            jax.ShapeDtypeStruct((B, nh, DH_H, DH_H), F32),
        ],
        scratch_shapes=[
            pltpu.VMEM((nh, DH_H, DH_H), F32),
            pltpu.VMEM((cps * L, dhh), F32),
            pltpu.VMEM((cps * L, dhh), F32),
            pltpu.VMEM((nh, L, LANES), F32),
            pltpu.VMEM((cps * L, dhh), F32),
            pltpu.VMEM((cps * L, dhh), F32),
        ],
        compiler_params=_params(("parallel", "arbitrary")),
        name="hgrn",
    )(proj, proj, proj, proj, lb_logits, s0, g_hnorm)


def _merge_kernel(hmg_ref, ohg_ref, ga_ref, gb_ref, x_ref, wa_ref, wb_ref, wo_ref, gffn_ref, x1_ref, h2_ref):
    ya = jnp.dot(hmg_ref[...], wa_ref[...], preferred_element_type=F32)
    yb = jnp.dot(ohg_ref[...], wb_ref[...], preferred_element_type=F32)
    u = _sigmoid(ga_ref[...]) * ya + _sigmoid(gb_ref[...]) * yb
    x1 = x_ref[...] + jnp.dot(u.astype(BF16), wo_ref[...], preferred_element_type=F32)
    x1_ref[...] = x1
    h2_ref[...] = _rms(x1, gffn_ref[...]).astype(BF16)


def _merge(hmg, ohg, proj, x2d, w_a, w_b, w_o, g_ffn):
    m, d = x2d.shape
    dm = hmg.shape[1]
    tm = min(m, 256)
    const = lambda i: (0, 0)
    return pl.pallas_call(
        _merge_kernel,
        grid=(m // tm,),
        in_specs=[
            pl.BlockSpec((tm, dm), lambda i: (i, 0)),
            pl.BlockSpec((tm, dm), lambda i: (i, 0)),
            pl.BlockSpec((tm, d), lambda i: (i, 2)),
            pl.BlockSpec((tm, d), lambda i: (i, 3)),
            pl.BlockSpec((tm, d), lambda i: (i, 0)),
            pl.BlockSpec((dm, d), const, pipeline_mode=pl.Buffered(1)),
            pl.BlockSpec((dm, d), const, pipeline_mode=pl.Buffered(1)),
            pl.BlockSpec((d, d), const, pipeline_mode=pl.Buffered(1)),
            pl.BlockSpec((1, d), const),
        ],
        out_specs=[pl.BlockSpec((tm, d), lambda i: (i, 0)), pl.BlockSpec((tm, d), lambda i: (i, 0))],
        out_shape=[jax.ShapeDtypeStruct((m, d), F32), jax.ShapeDtypeStruct((m, d), BF16)],
        compiler_params=_params(("parallel",)),
        name="merge",
    )(hmg, ohg, proj, proj, x2d, w_a, w_b, w_o, g_ffn)


def _ffn_kernel(x1_ref, h2_ref, w1_ref, w2_ref, gfin_ref, y_ref, acc_scr):
    j = pl.program_id(1)

    h = h2_ref[...]
    tf = w1_ref.shape[1]
    pieces = [slice(p, p + FFN_PIECE) for p in range(0, tf, FFN_PIECE)]
    ts = []
    for ps in pieces:
        t = jnp.maximum(jnp.dot(h, w1_ref[:, ps], preferred_element_type=F32), 0.0)
        ts.append((t * t).astype(BF16))
    acc = jnp.where(j > 0, acc_scr[...], 0.0)
    for ps, t in zip(pieces, ts):
        acc = acc + jnp.dot(t, w2_ref[ps, :], preferred_element_type=F32)
    acc_scr[...] = acc

    @pl.when(j == pl.num_programs(1) - 1)
    def _():
        y_ref[...] = _rms(x1_ref[...] + acc_scr[...], gfin_ref[...])


def _ffn(x1, h2, w1, w2, g_final):
    m, d = x1.shape
    dff = w1.shape[1]
    tm = min(m, 512)
    tf = 1024
    return pl.pallas_call(
        _ffn_kernel,
        grid=(m // tm, dff // tf),
        in_specs=[
            pl.BlockSpec((tm, d), lambda i, j: (i, 0)),
            pl.BlockSpec((tm, d), lambda i, j: (i, 0)),
            pl.BlockSpec((d, tf), lambda i, j: (0, j)),
            pl.BlockSpec((tf, d), lambda i, j: (j, 0)),
            pl.BlockSpec((1, d), lambda i, j: (0, 0)),
        ],
        out_specs=pl.BlockSpec((tm, d), lambda i, j: (i, 0)),
        out_shape=jax.ShapeDtypeStruct((m, d), F32),
        scratch_shapes=[pltpu.VMEM((tm, d), F32)],
        compiler_params=_params(("parallel", "arbitrary")),
        name="ffn",
    )(x1, h2, w1, w2, g_final)


def _group(x, conv_prev, c0, n0, m0, s0, wts):
    B, T, d = x.shape
    x2d = x.reshape(B * T, d)
    proj_lo, proj_hi, gates = _inproj(x2d, wts["g_mix"], wts["w_lo"], wts["w_hi"], wts["w_if"], wts["b_if"])
    hmg, new_conv, c1, n1, m1 = _mlstm(proj_lo, gates, conv_prev, c0, n0, m0.reshape(B, 1, -1),
                                       wts["w_conv"], wts["b_conv"], wts["g_mnorm"], B=B, T=T)
    ohg, s1 = _hgrn(proj_hi, wts["lb_logits"], s0, wts["g_hnorm"], B=B, T=T)
    x1, h2 = _merge(hmg, ohg, proj_hi, x2d, wts["w_a"], wts["w_b"], wts["w_o"], wts["g_ffn"])
    y = _ffn(x1, h2, wts["w_ff1"], wts["w_ff2"], wts["g_final"])
    return (y.reshape(B, T, d), new_conv[None], c1[None], n1[None], m1.reshape(1, B, -1), s1[None])


def kernel(x_prompt, x_sample, cache_mlstm_conv, state_mlstm_C, state_mlstm_n, state_mlstm_m, state_hgrn_S,
           g_mix, w_in, b_if, w_conv, b_conv, g_mnorm, g_hnorm, hgrn_lb_logits, w_branch_a, w_branch_b,
           w_out, g_ffn, w_ff1, w_ff2, g_final):
    depth = w_in.shape[0]
    assert depth == 1, "single-layer trunk only"
    d = x_prompt.shape[-1]
    dm = w_branch_a.shape[1]
    dhh = w_branch_b.shape[1]
    nh_h = dhh // DH_H
    dh_m = dm // H_M
    assert hgrn_lb_logits.shape[0] == depth + 1
    assert 2 * dm == d and dhh == dm

    w = w_in[0]
    n_if = 2 * H_M
    p_if = 4 * dm
    w_lo = w[:, :p_if].astype(BF16)
    w_hi = w[:, p_if + n_if:].astype(BF16)
    w_if = jnp.pad(w[:, p_if:p_if + n_if], ((0, 0), (0, LANES - n_if))).astype(BF16)
    bif = jnp.pad(b_if[0].astype(F32), (0, LANES - n_if)).reshape(1, LANES)
    wts = dict(
        g_mix=g_mix[0].reshape(1, d), w_lo=w_lo, w_hi=w_hi, w_if=w_if, b_if=bif,
        w_conv=w_conv[0], b_conv=b_conv[0].reshape(1, -1), g_mnorm=g_mnorm[0].reshape(1, dm),
        lb_logits=hgrn_lb_logits.astype(F32), g_hnorm=g_hnorm[0].reshape(1, dhh),
        w_a=w_branch_a[0].astype(BF16), w_b=w_branch_b[0].astype(BF16), w_o=w_out[0].astype(BF16),
        g_ffn=g_ffn[0].reshape(1, d), w_ff1=w_ff1[0].astype(BF16), w_ff2=w_ff2[0].astype(BF16),
        g_final=g_final.reshape(1, d),
    )

    bp = x_prompt.shape[0]
    zeros = lambda *s: jnp.zeros(s, F32)
    outs_p = _group(x_prompt, zeros(bp, CONV_W - 1, 2 * dm), zeros(bp, H_M, dh_m, dh_m), zeros(bp, H_M, dh_m),
                    zeros(bp, H_M), zeros(bp, nh_h, DH_H, DH_H), wts)
    outs_s = _group(x_sample, cache_mlstm_conv[0], state_mlstm_C[0].astype(F32), state_mlstm_n[0].astype(F32),
                    state_mlstm_m[0].astype(F32), state_hgrn_S[0].astype(F32), wts)
    return (outs_p[0], outs_s[0]) + outs_p[1:] + outs_s[1:]
```

```python
import functools

import jax
import jax.numpy as jnp
from jax import lax
from jax.experimental import pallas as pl
from jax.experimental.pallas import tpu as pltpu

F32 = jnp.float32
BF16 = jnp.bfloat16
HIGHEST = lax.Precision.HIGHEST

EPS = 1e-6
CHUNK = 64
MLSTM_CHUNK = 128
MLSTM_CPS = 2
CONV_W = 4
H_M = 4
DH_H = 128
SUB = 32
HGRN_CPS = 4
SAFE_SPAN = 60.0
LANES = 128
CONV_PAD = 8
FFN_PIECE = 256
VMEM_LIMIT = 56 * 1024 * 1024

NT_DIMS = (((1,), (1,)), ((), ()))
TN_DIMS = (((0,), (0,)), ((), ()))


def _rms(x, g):
    return x * lax.rsqrt(jnp.mean(x * x, axis=-1, keepdims=True) + EPS) * g


def _sigmoid(x):
    return 0.5 * jnp.tanh(0.5 * x) + 0.5


def _params(sem):
    return pltpu.CompilerParams(dimension_semantics=sem, vmem_limit_bytes=VMEM_LIMIT)


def _norm_kernel(x_ref, g_ref, wif_ref, bif_ref, h_ref, gates_ref):
    hb = _rms(x_ref[...], g_ref[...]).astype(BF16)
    h_ref[...] = hb
    gates_ref[...] = jnp.dot(hb, wif_ref[...], preferred_element_type=F32) + bif_ref[...]


def _matmul_kernel(h_ref, w_ref, o_ref):
    o_ref[...] = jnp.dot(h_ref[...], w_ref[...], preferred_element_type=F32)


def _matmul(h, w, n, name):
    m, d = h.shape
    tm = min(m, 1024)
    tn = 1024
    return pl.pallas_call(
        _matmul_kernel,
        grid=(m // tm, n // tn),
        in_specs=[
            pl.BlockSpec((tm, d), lambda i, j: (i, 0)),
            pl.BlockSpec((d, tn), lambda i, j: (0, j)),
        ],
        out_specs=pl.BlockSpec((tm, tn), lambda i, j: (i, j)),
        out_shape=jax.ShapeDtypeStruct((m, n), F32),
        compiler_params=_params(("parallel", "arbitrary")),
        name=name,
    )(h, w)


def _inproj(x2d, g_mix, w_all, n_lo, w_hi, w_if, b_if):
    m, d = x2d.shape
    tm = min(m, 512)
    h, gates = pl.pallas_call(
        _norm_kernel,
        grid=(m // tm,),
        in_specs=[
            pl.BlockSpec((tm, d), lambda i: (i, 0)),
            pl.BlockSpec((1, d), lambda i: (0, 0)),
            pl.BlockSpec((d, LANES), lambda i: (0, 0)),
            pl.BlockSpec((1, LANES), lambda i: (0, 0)),
        ],
        out_specs=[pl.BlockSpec((tm, d), lambda i: (i, 0)), pl.BlockSpec((tm, LANES), lambda i: (i, 0))],
        out_shape=[jax.ShapeDtypeStruct((m, d), BF16), jax.ShapeDtypeStruct((m, LANES), F32)],
        compiler_params=_params(("parallel",)),
        name="norm",
    )(x2d, g_mix, w_if, b_if)
    return _matmul(h, w_all, n_lo, "inproj_lo"), _matmul(h, w_hi, w_hi.shape[1], "inproj_hi"), gates


def _mlstm_kernel(qk_ref, v_ref, o_ref, gates_ref, convprev_ref, c0_ref, n0_ref, m0_ref,
                  wconv_ref, bconv_ref, gnorm_ref,
                  hmg_ref, newconv_ref, c1_ref, n1_ref, m1_ref,
                  xpad, qk_scr, c_scr, n_scr, m_scr, *, L, dm, dh, cps):
    c = pl.program_id(1)
    nc = pl.num_programs(1)
    hist0 = CONV_PAD - (CONV_W - 1)
    groups = [slice(g * LANES, (g + 1) * LANES) for g in range(2 * dm // LANES)]

    @pl.when(c == 0)
    def _():
        for g, gl in enumerate(groups):
            xpad[g, hist0:CONV_PAD, :] = convprev_ref[:, gl]
        c_scr[...] = c0_ref[...]
        n_scr[...] = n0_ref[...]
        m_scr[...] = m0_ref[...]

    rr = lax.broadcasted_iota(jnp.int32, (L, L), 0)
    cc = lax.broadcasted_iota(jnp.int32, (L, L), 1)
    causal = rr >= cc
    tril = causal.astype(F32)
    eye = (lax.broadcasted_iota(jnp.int32, (8, LANES), 0)
           == lax.broadcasted_iota(jnp.int32, (8, LANES), 1)).astype(F32)
    heads = [slice(h * dh, (h + 1) * dh) for h in range(H_M)]
    k_scale = dh ** -0.5

    for j in range(cps):
        cr = slice(j * L, (j + 1) * L)
        x0 = CONV_PAD + j * L
        ns = L // 8
        for g, gl in enumerate(groups):
            xpad[g, x0:x0 + L, :] = qk_ref[cr, gl]
            slabs = {s: xpad[g, pl.ds(x0 + s, ns, stride=8), :] for s in range(-(CONV_W - 1), 8)}
            for r in range(8):
                conv = bconv_ref[:, gl]
                for i in range(CONV_W):
                    conv = conv + slabs[r - (CONV_W - 1) + i] * wconv_ref[i:i + 1, gl]
                qk_scr[g, pl.ds(r, ns, stride=8), :] = conv * _sigmoid(conv)
        qk = jnp.concatenate([qk_scr[g] for g in range(len(groups))], axis=1)
        q = qk[:, :dm]
        k = qk[:, dm:]
        qb = q.astype(BF16)
        kb = k.astype(BF16)
        vv = v_ref[cr, :]
        vb = vv.astype(BF16)

        g = gates_ref[cr, :]
        col = lax.broadcasted_iota(jnp.int32, g.shape, 1)
        lf = jnp.minimum(g, 0.0) - jnp.log1p(jnp.exp(-jnp.abs(g)))
        x_cols = jnp.where(col < H_M, g, lf)
        cum = jnp.dot(tril, x_cols, preferred_element_type=F32, precision=HIGHEST)
        xc = jnp.where(col < H_M, g, cum)
        rows = lax.dot_general(eye, xc, NT_DIMS, preferred_element_type=F32, precision=HIGHEST)

        m_all = m_scr[...]
        e_s, sc_inter, inv_floor, decay, wgt, m_new = [], [], [], [], [], []
        for h in range(H_M):
            bcol = cum[:, H_M + h:H_M + h + 1]
            igcol = g[:, h:h + 1]
            brow = rows[H_M + h:H_M + h + 1, :]
            igrow = rows[h:h + 1, :]
            m_prev = m_all[:, h:h + 1]
            dlog = jnp.where(causal, bcol - brow + igrow, -jnp.inf)
            m_inter = bcol + m_prev
            m_t = jnp.maximum(m_inter, jnp.max(dlog, axis=-1, keepdims=True))
            e_s.append(jnp.exp(dlog - m_t) * k_scale)
            sc_inter.append(jnp.exp(m_inter - m_t))
            inv_floor.append(jnp.exp(-m_t))
            mn = m_t[L - 1:L, :]
            b_last = bcol[L - 1:L, :]
            m_new.append(mn)
            decay.append(jnp.exp(b_last + m_prev - mn))
            wgt.append(jnp.exp(b_last - bcol + igcol - mn) * k_scale)

        qn = []
        for h, sl in enumerate(heads):
            n_old = n_scr[h:h + 1, :]
            qn.append(jnp.sum(q[:, sl] * n_old, axis=-1, keepdims=True))
            n_scr[h:h + 1, :] = decay[h] * n_old + jnp.sum(wgt[h] * k[:, sl], axis=0, keepdims=True)
            m_scr[:, h:h + 1] = m_new[h]

        c_old = [c_scr[h] for h in range(H_M)]
        s_raw = [lax.dot_general(qb[:, sl], kb[:, sl], NT_DIMS, preferred_element_type=F32) for sl in heads]
        qc = [lax.dot_general(qb[:, sl], c_old[h].astype(BF16), NT_DIMS, preferred_element_type=F32)
              for h, sl in enumerate(heads)]
        upd = [lax.dot_general((vv[:, sl] * wgt[h]).astype(BF16), kb[:, sl], TN_DIMS, preferred_element_type=F32)
               for h, sl in enumerate(heads)]
        for h in range(H_M):
            c_scr[h] = decay[h] * c_old[h] + upd[h]
        s = [s_raw[h] * e_s[h] for h in range(H_M)]
        pv = [jnp.dot(s[h].astype(BF16), vb[:, sl], preferred_element_type=F32) for h, sl in enumerate(heads)]
        oo = o_ref[cr, :]
        for h, sl in enumerate(heads):
            num = sc_inter[h] * qc[h] + pv[h]
            den = sc_inter[h] * qn[h] + jnp.sum(s[h], axis=-1, keepdims=True)
            hout = num / jnp.maximum(jnp.abs(den), inv_floor[h])
            y = _rms(hout, gnorm_ref[:, sl]) * _sigmoid(oo[:, sl])
            hmg_ref[cr, sl] = y.astype(BF16)

    x_end = CONV_PAD + cps * L
    hist_g = [xpad[g, x_end - (CONV_W - 1):x_end, :] for g in range(len(groups))]
    for g in range(len(groups)):
        xpad[g, hist0:CONV_PAD, :] = hist_g[g]
    hist = jnp.concatenate(hist_g, axis=1)

    @pl.when(c == nc - 1)
    def _():
        newconv_ref[...] = hist
        c1_ref[...] = c_scr[...]
        n1_ref[...] = n_scr[...]
        m1_ref[...] = m_scr[...]


def _mlstm(proj, gates, conv_prev, c0, n0, m0, w_conv, b_conv, g_mnorm, *, B, T):
    L = MLSTM_CHUNK if T % MLSTM_CHUNK == 0 else min(CHUNK, T)
    nc = T // L
    dm = c0.shape[1] * c0.shape[2]
    dh = c0.shape[2]
    cps = MLSTM_CPS if nc % MLSTM_CPS == 0 else 1
    ns = nc // cps
    row = lambda b, c: b * ns + c
    kern = functools.partial(_mlstm_kernel, L=L, dm=dm, dh=dh, cps=cps)
    return pl.pallas_call(
        kern,
        grid=(B, ns),
        in_specs=[
            pl.BlockSpec((cps * L, 2 * dm), lambda b, c: (row(b, c), 0)),
            pl.BlockSpec((cps * L, dm), lambda b, c: (row(b, c), 2)),
            pl.BlockSpec((cps * L, dm), lambda b, c: (row(b, c), 3)),
            pl.BlockSpec((cps * L, LANES), lambda b, c: (row(b, c), 0)),
            pl.BlockSpec((None, CONV_W - 1, 2 * dm), lambda b, c: (b, 0, 0)),
            pl.BlockSpec((None, H_M, dh, dh), lambda b, c: (b, 0, 0, 0)),
            pl.BlockSpec((None, H_M, dh), lambda b, c: (b, 0, 0)),
            pl.BlockSpec((None, 1, H_M), lambda b, c: (b, 0, 0)),
            pl.BlockSpec((CONV_W, 2 * dm), lambda b, c: (0, 0)),
            pl.BlockSpec((1, 2 * dm), lambda b, c: (0, 0)),
            pl.BlockSpec((1, dm), lambda b, c: (0, 0)),
        ],
        out_specs=[
            pl.BlockSpec((cps * L, dm), lambda b, c: (row(b, c), 0)),
            pl.BlockSpec((None, CONV_W - 1, 2 * dm), lambda b, c: (b, 0, 0)),
            pl.BlockSpec((None, H_M, dh, dh), lambda b, c: (b, 0, 0, 0)),
            pl.BlockSpec((None, H_M, dh), lambda b, c: (b, 0, 0)),
            pl.BlockSpec((None, 1, H_M), lambda b, c: (b, 0, 0)),
        ],
        out_shape=[
            jax.ShapeDtypeStruct((B * T, dm), BF16),
            jax.ShapeDtypeStruct((B, CONV_W - 1, 2 * dm), F32),
            jax.ShapeDtypeStruct((B, H_M, dh, dh), F32),
            jax.ShapeDtypeStruct((B, H_M, dh), F32),
            jax.ShapeDtypeStruct((B, 1, H_M), F32),
        ],
        scratch_shapes=[
            pltpu.VMEM((2 * dm // LANES, CONV_PAD + cps * L, LANES), F32),
            pltpu.VMEM((2 * dm // LANES, L, LANES), F32),
            pltpu.VMEM((H_M, dh, dh), F32),
            pltpu.VMEM((H_M, dh), F32),
            pltpu.VMEM((1, H_M), F32),
        ],
        compiler_params=_params(("parallel", "arbitrary")),
        name="mlstm",
    )(proj, proj, proj, gates, conv_prev, c0, n0, m0, w_conv, b_conv, g_mnorm)


def _hgrn_factors(qq, kk, a, qa, sub, nb):
    L = a.shape[0]
    refs = [jnp.zeros_like(a[0:1, :])] + [a[i * sub - 1:i * sub, :] for i in range(1, nb)]
    qs = []
    ks = []
    for i in range(nb):
        rows = slice(i * sub, (i + 1) * sub)
        nk = (i + 1) * sub
        q_i = qa[rows, :] if i == 0 else (qq[rows, :] * jnp.exp(a[rows, :] - refs[i])).astype(BF16)
        k_i = (kk[:nk, :] * jnp.exp(jnp.minimum(refs[i] - a[:nk, :], SAFE_SPAN))).astype(BF16)
        zq = lambda n: jnp.zeros((n, q_i.shape[1]), BF16)
        qs.append(jnp.concatenate([z for z in (zq(i * sub), q_i, zq(L - nk)) if z.shape[0]], axis=0))
        ks.append(jnp.concatenate([z for z in (k_i, zq(L - nk)) if z.shape[0]], axis=0))
    return refs, qs, ks


def _hgrn_scores(qs, ks, sl, L):
    lhs = jnp.concatenate([q[:, sl] for q in qs], axis=1) if len(qs) > 1 else qs[0][:, sl]
    rhs = jnp.concatenate([k[:, sl] for k in ks], axis=1) if len(ks) > 1 else ks[0][:, sl]
    return lax.dot_general(lhs, rhs, NT_DIMS, preferred_element_type=F32)


def _hgrn_kernel(f_ref, i_ref, q_ref, g_ref, lbl_ref, s0_ref, gnorm_ref,
                 ohg_ref, s1_ref,
                 st_scr, a_scr, k_scr, sc_scr, o_scr, in_scr, *, L, nh, cps):
    c = pl.program_id(1)
    nc = pl.num_programs(1)
    sub = min(SUB, L)
    nb = L // sub

    @pl.when(c == 0)
    def _():
        for h in range(nh):
            st_scr[h] = s0_ref[h].T

    lbl = lbl_ref[...]
    mx = jnp.max(lbl, axis=0, keepdims=True)
    ex = jnp.exp(lbl - mx)
    lb = ex[0:1, :] / jnp.sum(ex, axis=0, keepdims=True)
    gnorm = gnorm_ref[...]
    rr = lax.broadcasted_iota(jnp.int32, (L, L), 0)
    cc = lax.broadcasted_iota(jnp.int32, (L, L), 1)
    causal = cc <= rr
    tril = causal.astype(BF16)
    tril3 = jnp.concatenate([tril, tril, tril], axis=1)

    def finish(rows):
        gg = g_ref[rows, :]
        ohg_ref[rows, :] = (_rms(o_scr[rows, :], gnorm) * (gg * _sigmoid(gg))).astype(BF16)

    span = None
    for j in range(cps):
        cr = slice(j * L, (j + 1) * L)
        f = lb + (1.0 - lb) * _sigmoid(f_ref[cr, :])
        lf = jnp.log(f)
        kk = 1.0 - f
        qq = q_ref[cr, :]
        vb = i_ref[cr, :].astype(BF16)

        hi = lf.astype(BF16)
        r1 = lf - hi.astype(F32)
        mid = r1.astype(BF16)
        lo = (r1 - mid.astype(F32)).astype(BF16)
        a = jnp.dot(tril3, jnp.concatenate([hi, mid, lo], axis=0), preferred_element_type=F32)
        a_scr[cr, :] = a
        k_scr[cr, :] = kk

        a_last = a[L - 1:L, :]
        qa = (qq * jnp.exp(a)).astype(BF16)
        kdec = (kk * jnp.exp(a_last - a)).astype(BF16)
        dec_last = jnp.exp(a_last)
        refs, qs, ks = _hgrn_factors(qq, kk, a, qa, sub, nb)
        for i in range(nb):
            sp = refs[i] - a[(i + 1) * sub - 1:(i + 1) * sub, :]
            span = sp if span is None else jnp.maximum(span, sp)

        heads = [slice(h * DH_H, (h + 1) * DH_H) for h in range(nh)]
        st_old = [st_scr[h] for h in range(nh)]
        inter = [lax.dot_general(qa[:, sl], st_old[h].astype(BF16), NT_DIMS, preferred_element_type=F32)
                 for h, sl in enumerate(heads)]
        raw = [_hgrn_scores(qs, ks, sl, L) for sl in heads]
        upd = [lax.dot_general(vb[:, sl], kdec[:, sl], TN_DIMS, preferred_element_type=F32) for sl in heads]
        for h, sl in enumerate(heads):
            st_scr[h] = st_old[h] * dec_last[:, sl] + upd[h]
            in_scr[cr, sl] = inter[h]
        blk = [jnp.where(causal, r, 0.0).astype(BF16) for r in raw]
        for h, sl in enumerate(heads):
            o_scr[cr, sl] = inter[h] + jnp.dot(blk[h], vb[:, sl], preferred_element_type=F32)
        finish(cr)

    use_exact = jnp.logical_not(jnp.max(span) <= SAFE_SPAN)

    @pl.when(use_exact)
    def _():
        sc_scr[...] = jnp.zeros_like(sc_scr)
        lane = lax.broadcasted_iota(jnp.int32, (sub, LANES), 1)
        in_block = jnp.logical_and(causal, (cc // sub) == (rr // sub))

        def fix_chunk(j, carry):
            base = pl.multiple_of(j * L, L)

            def diag_body(s, carry2):
                r0 = pl.multiple_of((s // sub) * sub, sub)
                a_blk = a_scr[pl.ds(base + r0, sub), :]
                q_blk = q_ref[pl.ds(base + r0, sub), :]
                a_s = a_scr[pl.ds(base + s, 1), :]
                k_s = k_scr[pl.ds(base + s, 1), :]
                p = q_blk * jnp.exp(jnp.minimum(a_blk - a_s, 0.0)) * k_s
                for h in range(nh):
                    colv = jnp.sum(p[:, h * DH_H:(h + 1) * DH_H], axis=-1, keepdims=True)
                    old = sc_scr[h, pl.ds(r0, sub), :]
                    sc_scr[h, pl.ds(r0, sub), :] = jnp.where(lane == s, colv, old)
                return carry2

            lax.fori_loop(0, L, diag_body, 0)

            cr = pl.ds(base, L)
            a = a_scr[cr, :]
            kk = k_scr[cr, :]
            qq = q_ref[cr, :]
            vb = i_ref[cr, :].astype(BF16)
            qa = (qq * jnp.exp(a)).astype(BF16)
            _, qs, ks = _hgrn_factors(qq, kk, a, qa, sub, nb)
            for h in range(nh):
                sl = slice(h * DH_H, (h + 1) * DH_H)
                blk = jnp.where(causal, _hgrn_scores(qs, ks, sl, L), 0.0)
                blk = jnp.where(in_block, sc_scr[h][:, :L], blk).astype(BF16)
                o_scr[cr, sl] = in_scr[cr, sl] + jnp.dot(blk, vb[:, sl], preferred_element_type=F32)
            finish(cr)
            return carry

        lax.fori_loop(0, cps, fix_chunk, 0)

    @pl.when(c == nc - 1)
    def _():
        for h in range(nh):
            s1_ref[h] = st_scr[h].T


def _hgrn(proj, lb_logits, s0, g_hnorm, *, B, T):
    L = min(CHUNK, T)
    nc = T // L
    cps = HGRN_CPS if nc % HGRN_CPS == 0 else 1
    ns = nc // cps
    nh = s0.shape[1]
    dhh = nh * DH_H
    row = lambda b, c: b * ns + c
    kern = functools.partial(_hgrn_kernel, L=L, nh=nh, cps=cps)
    return pl.pallas_call(
        kern,
        grid=(B, ns),
        in_specs=[
            pl.BlockSpec((cps * L, dhh), lambda b, c: (row(b, c), 0)),
            pl.BlockSpec((cps * L, dhh), lambda b, c: (row(b, c), 1)),
            pl.BlockSpec((cps * L, dhh), lambda b, c: (row(b, c), 2)),
            pl.BlockSpec((cps * L, dhh), lambda b, c: (row(b, c), 3)),
            pl.BlockSpec((2, dhh), lambda b, c: (0, 0)),
            pl.BlockSpec((None, nh, DH_H, DH_H), lambda b, c: (b, 0, 0, 0)),
            pl.BlockSpec((1, dhh), lambda b, c: (0, 0)),
        ],
        out_specs=[
            pl.BlockSpec((cps * L, dhh), lambda b, c: (row(b, c), 0)),
            pl.BlockSpec((None, nh, DH_H, DH_H), lambda b, c: (b, 0, 0, 0)),
        ],
        out_shape=[
            jax.ShapeDtypeStruct((B * T, dhh), BF16),
            jax.ShapeDtypeStruct((B, nh, DH_H, DH_H), F32),
        ],
        scratch_shapes=[
            pltpu.VMEM((nh, DH_H, DH_H), F32),
            pltpu.VMEM((cps * L, dhh), F32),
            pltpu.VMEM((cps * L, dhh), F32),
            pltpu.VMEM((nh, L, LANES), F32),
            pltpu.VMEM((cps * L, dhh), F32),
            pltpu.VMEM((cps * L, dhh), F32),
        ],
        compiler_params=_params(("parallel", "arbitrary")),
        name="hgrn",
    )(proj, proj, proj, proj, lb_logits, s0, g_hnorm)


def _merge_kernel(hmg_ref, ohg_ref, ga_ref, gb_ref, x_ref, wa_ref, wb_ref, wo_ref, gffn_ref, x1_ref, h2_ref):
    ya = jnp.dot(hmg_ref[...], wa_ref[...], preferred_element_type=F32)
    yb = jnp.dot(ohg_ref[...], wb_ref[...], preferred_element_type=F32)
    u = _sigmoid(ga_ref[...]) * ya + _sigmoid(gb_ref[...]) * yb
    x1 = x_ref[...] + jnp.dot(u.astype(BF16), wo_ref[...], preferred_element_type=F32)
    x1_ref[...] = x1
    h2_ref[...] = _rms(x1, gffn_ref[...]).astype(BF16)


def _merge(hmg, ohg, proj, x2d, w_a, w_b, w_o, g_ffn):
    m, d = x2d.shape
    dm = hmg.shape[1]
    tm = min(m, 256)
    const = lambda i: (0, 0)
    return pl.pallas_call(
        _merge_kernel,
        grid=(m // tm,),
        in_specs=[
            pl.BlockSpec((tm, dm), lambda i: (i, 0)),
            pl.BlockSpec((tm, dm), lambda i: (i, 0)),
            pl.BlockSpec((tm, d), lambda i: (i, 2)),
            pl.BlockSpec((tm, d), lambda i: (i, 3)),
            pl.BlockSpec((tm, d), lambda i: (i, 0)),
            pl.BlockSpec((dm, d), const, pipeline_mode=pl.Buffered(1)),
            pl.BlockSpec((dm, d), const, pipeline_mode=pl.Buffered(1)),
            pl.BlockSpec((d, d), const, pipeline_mode=pl.Buffered(1)),
            pl.BlockSpec((1, d), const),
        ],
        out_specs=[pl.BlockSpec((tm, d), lambda i: (i, 0)), pl.BlockSpec((tm, d), lambda i: (i, 0))],
        out_shape=[jax.ShapeDtypeStruct((m, d), F32), jax.ShapeDtypeStruct((m, d), BF16)],
        compiler_params=_params(("parallel",)),
        name="merge",
    )(hmg, ohg, proj, proj, x2d, w_a, w_b, w_o, g_ffn)


def _ffn_kernel(x1_ref, h2_ref, w1_ref, w2_ref, gfin_ref, y_ref, acc_scr):
    j = pl.program_id(1)

    h = h2_ref[...]
    tf = w1_ref.shape[1]
    pieces = [slice(p, p + FFN_PIECE) for p in range(0, tf, FFN_PIECE)]
    ts = []
    for ps in pieces:
        t = jnp.maximum(jnp.dot(h, w1_ref[:, ps], preferred_element_type=F32), 0.0)
        ts.append((t * t).astype(BF16))
    acc = jnp.where(j > 0, acc_scr[...], 0.0)
    for ps, t in zip(pieces, ts):
        acc = acc + jnp.dot(t, w2_ref[ps, :], preferred_element_type=F32)
    acc_scr[...] = acc

    @pl.when(j == pl.num_programs(1) - 1)
    def _():
        y_ref[...] = _rms(x1_ref[...] + acc_scr[...], gfin_ref[...])


def _ffn(x1, h2, w1, w2, g_final):
    m, d = x1.shape
    dff = w1.shape[1]
    tm = min(m, 512)
    tf = 1024
    return pl.pallas_call(
        _ffn_kernel,
        grid=(m // tm, dff // tf),
        in_specs=[
            pl.BlockSpec((tm, d), lambda i, j: (i, 0)),
            pl.BlockSpec((tm, d), lambda i, j: (i, 0)),
            pl.BlockSpec((d, tf), lambda i, j: (0, j)),
            pl.BlockSpec((tf, d), lambda i, j: (j, 0)),
            pl.BlockSpec((1, d), lambda i, j: (0, 0)),
        ],
        out_specs=pl.BlockSpec((tm, d), lambda i, j: (i, 0)),
        out_shape=jax.ShapeDtypeStruct((m, d), F32),
        scratch_shapes=[pltpu.VMEM((tm, d), F32)],
        compiler_params=_params(("parallel", "arbitrary")),
        name="ffn",
    )(x1, h2, w1, w2, g_final)


def _group(x, conv_prev, c0, n0, m0, s0, wts):
    B, T, d = x.shape
    x2d = x.reshape(B * T, d)
    proj_lo, proj_hi, gates = _inproj(x2d, wts["g_mix"], wts["w_all"], wts["n_lo"], wts["w_hi"], wts["w_if"], wts["b_if"])
    hmg, new_conv, c1, n1, m1 = _mlstm(proj_lo, gates, conv_prev, c0, n0, m0.reshape(B, 1, -1),
                                       wts["w_conv"], wts["b_conv"], wts["g_mnorm"], B=B, T=T)
    ohg, s1 = _hgrn(proj_hi, wts["lb_logits"], s0, wts["g_hnorm"], B=B, T=T)
    x1, h2 = _merge(hmg, ohg, proj_hi, x2d, wts["w_a"], wts["w_b"], wts["w_o"], wts["g_ffn"])
    y = _ffn(x1, h2, wts["w_ff1"], wts["w_ff2"], wts["g_final"])
    return (y.reshape(B, T, d), new_conv[None], c1[None], n1[None], m1.reshape(1, B, -1), s1[None])


def kernel(x_prompt, x_sample, cache_mlstm_conv, state_mlstm_C, state_mlstm_n, state_mlstm_m, state_hgrn_S,
           g_mix, w_in, b_if, w_conv, b_conv, g_mnorm, g_hnorm, hgrn_lb_logits, w_branch_a, w_branch_b,
           w_out, g_ffn, w_ff1, w_ff2, g_final):
    depth = w_in.shape[0]
    assert depth == 1, "single-layer trunk only"
    d = x_prompt.shape[-1]
    dm = w_branch_a.shape[1]
    dhh = w_branch_b.shape[1]
    nh_h = dhh // DH_H
    dh_m = dm // H_M
    assert hgrn_lb_logits.shape[0] == depth + 1
    assert 2 * dm == d and dhh == dm

    w = w_in[0]
    n_if = 2 * H_M
    p_if = 4 * dm
    w_all = w.astype(BF16)
    w_hi = w_all[:, p_if + n_if:]
    w_if = jnp.pad(w_all[:, p_if:p_if + n_if], ((0, 0), (0, LANES - n_if)))
    bif = jnp.pad(b_if[0].astype(F32), (0, LANES - n_if)).reshape(1, LANES)
    wts = dict(
        g_mix=g_mix[0].reshape(1, d), w_all=w_all, n_lo=p_if, w_hi=w_hi, w_if=w_if, b_if=bif,
        w_conv=w_conv[0], b_conv=b_conv[0].reshape(1, -1), g_mnorm=g_mnorm[0].reshape(1, dm),
        lb_logits=hgrn_lb_logits.astype(F32), g_hnorm=g_hnorm[0].reshape(1, dhh),
        w_a=w_branch_a[0].astype(BF16), w_b=w_branch_b[0].astype(BF16), w_o=w_out[0].astype(BF16),
        g_ffn=g_ffn[0].reshape(1, d), w_ff1=w_ff1[0].astype(BF16), w_ff2=w_ff2[0].astype(BF16),
        g_final=g_final.reshape(1, d),
    )

    bp = x_prompt.shape[0]
    zeros = lambda *s: jnp.zeros(s, F32)
    outs_p = _group(x_prompt, zeros(bp, CONV_W - 1, 2 * dm), zeros(bp, H_M, dh_m, dh_m), zeros(bp, H_M, dh_m),
                    zeros(bp, H_M), zeros(bp, nh_h, DH_H, DH_H), wts)
    outs_s = _group(x_sample, cache_mlstm_conv[0], state_mlstm_C[0].astype(F32), state_mlstm_n[0].astype(F32),
                    state_mlstm_m[0].astype(F32), state_hgrn_S[0].astype(F32), wts)
    return (outs_p[0], outs_s[0]) + outs_p[1:] + outs_s[1:]
```

```python
import functools

import jax
import jax.numpy as jnp
from jax import lax
from jax.experimental import pallas as pl
from jax.experimental.pallas import tpu as pltpu

F32 = jnp.float32
BF16 = jnp.bfloat16
HIGHEST = lax.Precision.HIGHEST

EPS = 1e-6
CHUNK = 64
MLSTM_CHUNK = 128
MLSTM_CPS = 2
CONV_W = 4
H_M = 4
DH_H = 128
SUB = 32
HGRN_CPS = 4
SAFE_SPAN = 60.0
LANES = 128
CONV_PAD = 8
FFN_PIECE = 256
VMEM_LIMIT = 56 * 1024 * 1024

NT_DIMS = (((1,), (1,)), ((), ()))
TN_DIMS = (((0,), (0,)), ((), ()))


def _rms(x, g):
    return x * lax.rsqrt(jnp.mean(x * x, axis=-1, keepdims=True) + EPS) * g


def _sigmoid(x):
    return 0.5 * jnp.tanh(0.5 * x) + 0.5


def _params(sem):
    return pltpu.CompilerParams(dimension_semantics=sem, vmem_limit_bytes=VMEM_LIMIT)


def _norm_kernel(x_ref, g_ref, wif_ref, bif_ref, h_ref, gates_ref):
    hb = _rms(x_ref[...], g_ref[...]).astype(BF16)
    h_ref[...] = hb
    gates_ref[...] = jnp.dot(hb, wif_ref[...], preferred_element_type=F32) + bif_ref[...]


def _matmul_kernel(h_ref, w_ref, o_ref):
    o_ref[...] = jnp.dot(h_ref[...], w_ref[...], preferred_element_type=F32)


def _matmul(h, w, n, name):
    m, d = h.shape
    tm = min(m, 1024)
    tn = 2048
    return pl.pallas_call(
        _matmul_kernel,
        grid=(m // tm, n // tn),
        in_specs=[
            pl.BlockSpec((tm, d), lambda i, j: (i, 0)),
            pl.BlockSpec((d, tn), lambda i, j: (0, j)),
        ],
        out_specs=pl.BlockSpec((tm, tn), lambda i, j: (i, j)),
        out_shape=jax.ShapeDtypeStruct((m, n), F32),
        compiler_params=_params(("parallel", "arbitrary")),
        name=name,
    )(h, w)


def _inproj(x2d, g_mix, w_all, n_lo, w_hi, w_if, b_if):
    m, d = x2d.shape
    tm = min(m, 512)
    h, gates = pl.pallas_call(
        _norm_kernel,
        grid=(m // tm,),
        in_specs=[
            pl.BlockSpec((tm, d), lambda i: (i, 0)),
            pl.BlockSpec((1, d), lambda i: (0, 0)),
            pl.BlockSpec((d, LANES), lambda i: (0, 0)),
            pl.BlockSpec((1, LANES), lambda i: (0, 0)),
        ],
        out_specs=[pl.BlockSpec((tm, d), lambda i: (i, 0)), pl.BlockSpec((tm, LANES), lambda i: (i, 0))],
        out_shape=[jax.ShapeDtypeStruct((m, d), BF16), jax.ShapeDtypeStruct((m, LANES), F32)],
        compiler_params=_params(("parallel",)),
        name="norm",
    )(x2d, g_mix, w_if, b_if)
    return _matmul(h, w_all, n_lo, "inproj_lo"), _matmul(h, w_hi, w_hi.shape[1], "inproj_hi"), gates


def _mlstm_kernel(qk_ref, v_ref, o_ref, gates_ref, convprev_ref, c0_ref, n0_ref, m0_ref,
                  wconv_ref, bconv_ref, gnorm_ref,
                  hmg_ref, newconv_ref, c1_ref, n1_ref, m1_ref,
                  xpad, qk_scr, c_scr, n_scr, m_scr, *, L, dm, dh, cps):
    c = pl.program_id(1)
    nc = pl.num_programs(1)
    hist0 = CONV_PAD - (CONV_W - 1)
    groups = [slice(g * LANES, (g + 1) * LANES) for g in range(2 * dm // LANES)]

    @pl.when(c == 0)
    def _():
        for g, gl in enumerate(groups):
            xpad[g, hist0:CONV_PAD, :] = convprev_ref[:, gl]
        c_scr[...] = c0_ref[...]
        n_scr[...] = n0_ref[...]
        m_scr[...] = m0_ref[...]

    rr = lax.broadcasted_iota(jnp.int32, (L, L), 0)
    cc = lax.broadcasted_iota(jnp.int32, (L, L), 1)
    causal = rr >= cc
    tril = causal.astype(F32)
    eye = (lax.broadcasted_iota(jnp.int32, (8, LANES), 0)
           == lax.broadcasted_iota(jnp.int32, (8, LANES), 1)).astype(F32)
    heads = [slice(h * dh, (h + 1) * dh) for h in range(H_M)]
    k_scale = dh ** -0.5

    for j in range(cps):
        cr = slice(j * L, (j + 1) * L)
        x0 = CONV_PAD + j * L
        ns = L // 8
        for g, gl in enumerate(groups):
            xpad[g, x0:x0 + L, :] = qk_ref[cr, gl]
            slabs = {s: xpad[g, pl.ds(x0 + s, ns, stride=8), :] for s in range(-(CONV_W - 1), 8)}
            for r in range(8):
                conv = bconv_ref[:, gl]
                for i in range(CONV_W):
                    conv = conv + slabs[r - (CONV_W - 1) + i] * wconv_ref[i:i + 1, gl]
                qk_scr[g, pl.ds(r, ns, stride=8), :] = conv * _sigmoid(conv)
        qk = jnp.concatenate([qk_scr[g] for g in range(len(groups))], axis=1)
        q = qk[:, :dm]
        k = qk[:, dm:]
        qb = q.astype(BF16)
        kb = k.astype(BF16)
        vv = v_ref[cr, :]
        vb = vv.astype(BF16)

        g = gates_ref[cr, :]
        col = lax.broadcasted_iota(jnp.int32, g.shape, 1)
        lf = jnp.minimum(g, 0.0) - jnp.log1p(jnp.exp(-jnp.abs(g)))
        x_cols = jnp.where(col < H_M, g, lf)
        cum = jnp.dot(tril, x_cols, preferred_element_type=F32, precision=HIGHEST)
        xc = jnp.where(col < H_M, g, cum)
        rows = lax.dot_general(eye, xc, NT_DIMS, preferred_element_type=F32, precision=HIGHEST)

        m_all = m_scr[...]
        e_s, sc_inter, inv_floor, decay, wgt, m_new = [], [], [], [], [], []
        for h in range(H_M):
            bcol = cum[:, H_M + h:H_M + h + 1]
            igcol = g[:, h:h + 1]
            brow = rows[H_M + h:H_M + h + 1, :]
            igrow = rows[h:h + 1, :]
            m_prev = m_all[:, h:h + 1]
            dlog = jnp.where(causal, bcol - brow + igrow, -jnp.inf)
            m_inter = bcol + m_prev
            m_t = jnp.maximum(m_inter, jnp.max(dlog, axis=-1, keepdims=True))
            e_s.append(jnp.exp(dlog - m_t) * k_scale)
            sc_inter.append(jnp.exp(m_inter - m_t))
            inv_floor.append(jnp.exp(-m_t))
            mn = m_t[L - 1:L, :]
            b_last = bcol[L - 1:L, :]
            m_new.append(mn)
            decay.append(jnp.exp(b_last + m_prev - mn))
            wgt.append(jnp.exp(b_last - bcol + igcol - mn) * k_scale)

        qn = []
        for h, sl in enumerate(heads):
            n_old = n_scr[h:h + 1, :]
            qn.append(jnp.sum(q[:, sl] * n_old, axis=-1, keepdims=True))
            n_scr[h:h + 1, :] = decay[h] * n_old + jnp.sum(wgt[h] * k[:, sl], axis=0, keepdims=True)
            m_scr[:, h:h + 1] = m_new[h]

        c_old = [c_scr[h] for h in range(H_M)]
        s_raw = [lax.dot_general(qb[:, sl], kb[:, sl], NT_DIMS, preferred_element_type=F32) for sl in heads]
        qc = [lax.dot_general(qb[:, sl], c_old[h].astype(BF16), NT_DIMS, preferred_element_type=F32)
              for h, sl in enumerate(heads)]
        upd = [lax.dot_general((vv[:, sl] * wgt[h]).astype(BF16), kb[:, sl], TN_DIMS, preferred_element_type=F32)
               for h, sl in enumerate(heads)]
        for h in range(H_M):
            c_scr[h] = decay[h] * c_old[h] + upd[h]
        s = [s_raw[h] * e_s[h] for h in range(H_M)]
        pv = [jnp.dot(s[h].astype(BF16), vb[:, sl], preferred_element_type=F32) for h, sl in enumerate(heads)]
        oo = o_ref[cr, :]
        for h, sl in enumerate(heads):
            num = sc_inter[h] * qc[h] + pv[h]
            den = sc_inter[h] * qn[h] + jnp.sum(s[h], axis=-1, keepdims=True)
            hout = num / jnp.maximum(jnp.abs(den), inv_floor[h])
            y = _rms(hout, gnorm_ref[:, sl]) * _sigmoid(oo[:, sl])
            hmg_ref[cr, sl] = y.astype(BF16)

    x_end = CONV_PAD + cps * L
    hist_g = [xpad[g, x_end - (CONV_W - 1):x_end, :] for g in range(len(groups))]
    for g in range(len(groups)):
        xpad[g, hist0:CONV_PAD, :] = hist_g[g]
    hist = jnp.concatenate(hist_g, axis=1)

    @pl.when(c == nc - 1)
    def _():
        newconv_ref[...] = hist
        c1_ref[...] = c_scr[...]
        n1_ref[...] = n_scr[...]
        m1_ref[...] = m_scr[...]


def _mlstm(proj, gates, conv_prev, c0, n0, m0, w_conv, b_conv, g_mnorm, *, B, T):
    L = MLSTM_CHUNK if T % MLSTM_CHUNK == 0 else min(CHUNK, T)
    nc = T // L
    dm = c0.shape[1] * c0.shape[2]
    dh = c0.shape[2]
    cps = MLSTM_CPS if nc % MLSTM_CPS == 0 else 1
    ns = nc // cps
    row = lambda b, c: b * ns + c
    kern = functools.partial(_mlstm_kernel, L=L, dm=dm, dh=dh, cps=cps)
    return pl.pallas_call(
        kern,
        grid=(B, ns),
        in_specs=[
            pl.BlockSpec((cps * L, 2 * dm), lambda b, c: (row(b, c), 0)),
            pl.BlockSpec((cps * L, dm), lambda b, c: (row(b, c), 2)),
            pl.BlockSpec((cps * L, dm), lambda b, c: (row(b, c), 3)),
            pl.BlockSpec((cps * L, LANES), lambda b, c: (row(b, c), 0)),
            pl.BlockSpec((None, CONV_W - 1, 2 * dm), lambda b, c: (b, 0, 0)),
            pl.BlockSpec((None, H_M, dh, dh), lambda b, c: (b, 0, 0, 0)),
            pl.BlockSpec((None, H_M, dh), lambda b, c: (b, 0, 0)),
            pl.BlockSpec((None, 1, H_M), lambda b, c: (b, 0, 0)),
            pl.BlockSpec((CONV_W, 2 * dm), lambda b, c: (0, 0)),
            pl.BlockSpec((1, 2 * dm), lambda b, c: (0, 0)),
            pl.BlockSpec((1, dm), lambda b, c: (0, 0)),
        ],
        out_specs=[
            pl.BlockSpec((cps * L, dm), lambda b, c: (row(b, c), 0)),
            pl.BlockSpec((None, CONV_W - 1, 2 * dm), lambda b, c: (b, 0, 0)),
            pl.BlockSpec((None, H_M, dh, dh), lambda b, c: (b, 0, 0, 0)),
            pl.BlockSpec((None, H_M, dh), lambda b, c: (b, 0, 0)),
            pl.BlockSpec((None, 1, H_M), lambda b, c: (b, 0, 0)),
        ],
        out_shape=[
            jax.ShapeDtypeStruct((B * T, dm), BF16),
            jax.ShapeDtypeStruct((B, CONV_W - 1, 2 * dm), F32),
            jax.ShapeDtypeStruct((B, H_M, dh, dh), F32),
            jax.ShapeDtypeStruct((B, H_M, dh), F32),
            jax.ShapeDtypeStruct((B, 1, H_M), F32),
        ],
        scratch_shapes=[
            pltpu.VMEM((2 * dm // LANES, CONV_PAD + cps * L, LANES), F32),
            pltpu.VMEM((2 * dm // LANES, L, LANES), F32),
            pltpu.VMEM((H_M, dh, dh), F32),
            pltpu.VMEM((H_M, dh), F32),
            pltpu.VMEM((1, H_M), F32),
        ],
        compiler_params=_params(("parallel", "arbitrary")),
        name="mlstm",
    )(proj, proj, proj, gates, conv_prev, c0, n0, m0, w_conv, b_conv, g_mnorm)


def _hgrn_factors(qq, kk, a, qa, sub, nb):
    L = a.shape[0]
    refs = [jnp.zeros_like(a[0:1, :])] + [a[i * sub - 1:i * sub, :] for i in range(1, nb)]
    qs = []
    ks = []
    for i in range(nb):
        rows = slice(i * sub, (i + 1) * sub)
        nk = (i + 1) * sub
        q_i = qa[rows, :] if i == 0 else (qq[rows, :] * jnp.exp(a[rows, :] - refs[i])).astype(BF16)
        k_i = (kk[:nk, :] * jnp.exp(jnp.minimum(refs[i] - a[:nk, :], SAFE_SPAN))).astype(BF16)
        zq = lambda n: jnp.zeros((n, q_i.shape[1]), BF16)
        qs.append(jnp.concatenate([z for z in (zq(i * sub), q_i, zq(L - nk)) if z.shape[0]], axis=0))
        ks.append(jnp.concatenate([z for z in (k_i, zq(L - nk)) if z.shape[0]], axis=0))
    return refs, qs, ks


def _hgrn_scores(qs, ks, sl, L):
    lhs = jnp.concatenate([q[:, sl] for q in qs], axis=1) if len(qs) > 1 else qs[0][:, sl]
    rhs = jnp.concatenate([k[:, sl] for k in ks], axis=1) if len(ks) > 1 else ks[0][:, sl]
    return lax.dot_general(lhs, rhs, NT_DIMS, preferred_element_type=F32)


def _hgrn_kernel(f_ref, i_ref, q_ref, g_ref, lbl_ref, s0_ref, gnorm_ref,
                 ohg_ref, s1_ref,
                 st_scr, a_scr, k_scr, sc_scr, o_scr, in_scr, *, L, nh, cps):
    c = pl.program_id(1)
    nc = pl.num_programs(1)
    sub = min(SUB, L)
    nb = L // sub

    @pl.when(c == 0)
    def _():
        for h in range(nh):
            st_scr[h] = s0_ref[h].T

    lbl = lbl_ref[...]
    mx = jnp.max(lbl, axis=0, keepdims=True)
    ex = jnp.exp(lbl - mx)
    lb = ex[0:1, :] / jnp.sum(ex, axis=0, keepdims=True)
    gnorm = gnorm_ref[...]
    rr = lax.broadcasted_iota(jnp.int32, (L, L), 0)
    cc = lax.broadcasted_iota(jnp.int32, (L, L), 1)
    causal = cc <= rr
    tril = causal.astype(BF16)
    tril3 = jnp.concatenate([tril, tril, tril], axis=1)

    def finish(rows):
        gg = g_ref[rows, :]
        ohg_ref[rows, :] = (_rms(o_scr[rows, :], gnorm) * (gg * _sigmoid(gg))).astype(BF16)

    span = None
    for j in range(cps):
        cr = slice(j * L, (j + 1) * L)
        f = lb + (1.0 - lb) * _sigmoid(f_ref[cr, :])
        lf = jnp.log(f)
        kk = 1.0 - f
        qq = q_ref[cr, :]
        vb = i_ref[cr, :].astype(BF16)

        hi = lf.astype(BF16)
        r1 = lf - hi.astype(F32)
        mid = r1.astype(BF16)
        lo = (r1 - mid.astype(F32)).astype(BF16)
        a = jnp.dot(tril3, jnp.concatenate([hi, mid, lo], axis=0), preferred_element_type=F32)
        a_scr[cr, :] = a
        k_scr[cr, :] = kk

        a_last = a[L - 1:L, :]
        qa = (qq * jnp.exp(a)).astype(BF16)
        kdec = (kk * jnp.exp(a_last - a)).astype(BF16)
        dec_last = jnp.exp(a_last)
        refs, qs, ks = _hgrn_factors(qq, kk, a, qa, sub, nb)
        for i in range(nb):
            sp = refs[i] - a[(i + 1) * sub - 1:(i + 1) * sub, :]
            span = sp if span is None else jnp.maximum(span, sp)

        heads = [slice(h * DH_H, (h + 1) * DH_H) for h in range(nh)]
        st_old = [st_scr[h] for h in range(nh)]
        inter = [lax.dot_general(qa[:, sl], st_old[h].astype(BF16), NT_DIMS, preferred_element_type=F32)
                 for h, sl in enumerate(heads)]
        raw = [_hgrn_scores(qs, ks, sl, L) for sl in heads]
        upd = [lax.dot_general(vb[:, sl], kdec[:, sl], TN_DIMS, preferred_element_type=F32) for sl in heads]
        for h, sl in enumerate(heads):
            st_scr[h] = st_old[h] * dec_last[:, sl] + upd[h]
            in_scr[cr, sl] = inter[h]
        blk = [jnp.where(causal, r, 0.0).astype(BF16) for r in raw]
        for h, sl in enumerate(heads):
            o_scr[cr, sl] = inter[h] + jnp.dot(blk[h], vb[:, sl], preferred_element_type=F32)
        finish(cr)

    use_exact = jnp.logical_not(jnp.max(span) <= SAFE_SPAN)

    @pl.when(use_exact)
    def _():
        sc_scr[...] = jnp.zeros_like(sc_scr)
        lane = lax.broadcasted_iota(jnp.int32, (sub, LANES), 1)
        in_block = jnp.logical_and(causal, (cc // sub) == (rr // sub))

        def fix_chunk(j, carry):
            base = pl.multiple_of(j * L, L)

            def diag_body(s, carry2):
                r0 = pl.multiple_of((s // sub) * sub, sub)
                a_blk = a_scr[pl.ds(base + r0, sub), :]
                q_blk = q_ref[pl.ds(base + r0, sub), :]
                a_s = a_scr[pl.ds(base + s, 1), :]
                k_s = k_scr[pl.ds(base + s, 1), :]
                p = q_blk * jnp.exp(jnp.minimum(a_blk - a_s, 0.0)) * k_s
                for h in range(nh):
                    colv = jnp.sum(p[:, h * DH_H:(h + 1) * DH_H], axis=-1, keepdims=True)
                    old = sc_scr[h, pl.ds(r0, sub), :]
                    sc_scr[h, pl.ds(r0, sub), :] = jnp.where(lane == s, colv, old)
                return carry2

            lax.fori_loop(0, L, diag_body, 0)

            cr = pl.ds(base, L)
            a = a_scr[cr, :]
            kk = k_scr[cr, :]
            qq = q_ref[cr, :]
            vb = i_ref[cr, :].astype(BF16)
            qa = (qq * jnp.exp(a)).astype(BF16)
            _, qs, ks = _hgrn_factors(qq, kk, a, qa, sub, nb)
            for h in range(nh):
                sl = slice(h * DH_H, (h + 1) * DH_H)
                blk = jnp.where(causal, _hgrn_scores(qs, ks, sl, L), 0.0)
                blk = jnp.where(in_block, sc_scr[h][:, :L], blk).astype(BF16)
                o_scr[cr, sl] = in_scr[cr, sl] + jnp.dot(blk, vb[:, sl], preferred_element_type=F32)
            finish(cr)
            return carry

        lax.fori_loop(0, cps, fix_chunk, 0)

    @pl.when(c == nc - 1)
    def _():
        for h in range(nh):
            s1_ref[h] = st_scr[h].T


def _hgrn(proj, lb_logits, s0, g_hnorm, *, B, T):
    L = min(CHUNK, T)
    nc = T // L
    cps = HGRN_CPS if nc % HGRN_CPS == 0 else 1
    ns = nc // cps
    nh = s0.shape[1]
    dhh = nh * DH_H
    row = lambda b, c: b * ns + c
    kern = functools.partial(_hgrn_kernel, L=L, nh=nh, cps=cps)
    return pl.pallas_call(
        kern,
        grid=(B, ns),
        in_specs=[
            pl.BlockSpec((cps * L, dhh), lambda b, c: (row(b, c), 0)),
            pl.BlockSpec((cps * L, dhh), lambda b, c: (row(b, c), 1)),
            pl.BlockSpec((cps * L, dhh), lambda b, c: (row(b, c), 2)),
            pl.BlockSpec((cps * L, dhh), lambda b, c: (row(b, c), 3)),
            pl.BlockSpec((2, dhh), lambda b, c: (0, 0)),
            pl.BlockSpec((None, nh, DH_H, DH_H), lambda b, c: (b, 0, 0, 0)),
            pl.BlockSpec((1, dhh), lambda b, c: (0, 0)),
        ],
        out_specs=[
            pl.BlockSpec((cps * L, dhh), lambda b, c: (row(b, c), 0)),
            pl.BlockSpec((None, nh, DH_H, DH_H), lambda b, c: (b, 0, 0, 0)),
        ],
        out_shape=[
            jax.ShapeDtypeStruct((B * T, dhh), BF16),
            jax.ShapeDtypeStruct((B, nh, DH_H, DH_H), F32),
        ],
        scratch_shapes=[
            pltpu.VMEM((nh, DH_H, DH_H), F32),
            pltpu.VMEM((cps * L, dhh), F32),
            pltpu.VMEM((cps * L, dhh), F32),
            pltpu.VMEM((nh, L, LANES), F32),
            pltpu.VMEM((cps * L, dhh), F32),
            pltpu.VMEM((cps * L, dhh), F32),
        ],
        compiler_params=_params(("parallel", "arbitrary")),
        name="hgrn",
    )(proj, proj, proj, proj, lb_logits, s0, g_hnorm)


def _merge_kernel(hmg_ref, ohg_ref, ga_ref, gb_ref, x_ref, wa_ref, wb_ref, wo_ref, gffn_ref, x1_ref, h2_ref):
    ya = jnp.dot(hmg_ref[...], wa_ref[...], preferred_element_type=F32)
    yb = jnp.dot(ohg_ref[...], wb_ref[...], preferred_element_type=F32)
    u = _sigmoid(ga_ref[...]) * ya + _sigmoid(gb_ref[...]) * yb
    x1 = x_ref[...] + jnp.dot(u.astype(BF16), wo_ref[...], preferred_element_type=F32)
    x1_ref[...] = x1
    h2_ref[...] = _rms(x1, gffn_ref[...]).astype(BF16)


def _merge(hmg, ohg, proj, x2d, w_a, w_b, w_o, g_ffn):
    m, d = x2d.shape
    dm = hmg.shape[1]
    tm = min(m, 256)
    const = lambda i: (0, 0)
    return pl.pallas_call(
        _merge_kernel,
        grid=(m // tm,),
        in_specs=[
            pl.BlockSpec((tm, dm), lambda i: (i, 0)),
            pl.BlockSpec((tm, dm), lambda i: (i, 0)),
            pl.BlockSpec((tm, d), lambda i: (i, 2)),
            pl.BlockSpec((tm, d), lambda i: (i, 3)),
            pl.BlockSpec((tm, d), lambda i: (i, 0)),
            pl.BlockSpec((dm, d), const, pipeline_mode=pl.Buffered(1)),
            pl.BlockSpec((dm, d), const, pipeline_mode=pl.Buffered(1)),
            pl.BlockSpec((d, d), const, pipeline_mode=pl.Buffered(1)),
            pl.BlockSpec((1, d), const),
        ],
        out_specs=[pl.BlockSpec((tm, d), lambda i: (i, 0)), pl.BlockSpec((tm, d), lambda i: (i, 0))],
        out_shape=[jax.ShapeDtypeStruct((m, d), F32), jax.ShapeDtypeStruct((m, d), BF16)],
        compiler_params=_params(("parallel",)),
        name="merge",
    )(hmg, ohg, proj, proj, x2d, w_a, w_b, w_o, g_ffn)


def _ffn_kernel(x1_ref, h2_ref, w1_ref, w2_ref, gfin_ref, y_ref, acc_scr):
    j = pl.program_id(1)

    h = h2_ref[...]
    tf = w1_ref.shape[1]
    pieces = [slice(p, p + FFN_PIECE) for p in range(0, tf, FFN_PIECE)]
    ts = []
    for ps in pieces:
        t = jnp.maximum(jnp.dot(h, w1_ref[:, ps], preferred_element_type=F32), 0.0)
        ts.append((t * t).astype(BF16))
    acc = jnp.where(j > 0, acc_scr[...], 0.0)
    for ps, t in zip(pieces, ts):
        acc = acc + jnp.dot(t, w2_ref[ps, :], preferred_element_type=F32)
    acc_scr[...] = acc

    @pl.when(j == pl.num_programs(1) - 1)
    def _():
        y_ref[...] = _rms(x1_ref[...] + acc_scr[...], gfin_ref[...])


def _ffn(x1, h2, w1, w2, g_final):
    m, d = x1.shape
    dff = w1.shape[1]
    tm = min(m, 512)
    tf = 1024
    return pl.pallas_call(
        _ffn_kernel,
        grid=(m // tm, dff // tf),
        in_specs=[
            pl.BlockSpec((tm, d), lambda i, j: (i, 0)),
            pl.BlockSpec((tm, d), lambda i, j: (i, 0)),
            pl.BlockSpec((d, tf), lambda i, j: (0, j)),
            pl.BlockSpec((tf, d), lambda i, j: (j, 0)),
            pl.BlockSpec((1, d), lambda i, j: (0, 0)),
        ],
        out_specs=pl.BlockSpec((tm, d), lambda i, j: (i, 0)),
        out_shape=jax.ShapeDtypeStruct((m, d), F32),
        scratch_shapes=[pltpu.VMEM((tm, d), F32)],
        compiler_params=_params(("parallel", "arbitrary")),
        name="ffn",
    )(x1, h2, w1, w2, g_final)


def _group(x, conv_prev, c0, n0, m0, s0, wts):
    B, T, d = x.shape
    x2d = x.reshape(B * T, d)
    proj_lo, proj_hi, gates = _inproj(x2d, wts["g_mix"], wts["w_all"], wts["n_lo"], wts["w_hi"], wts["w_if"], wts["b_if"])
    hmg, new_conv, c1, n1, m1 = _mlstm(proj_lo, gates, conv_prev, c0, n0, m0.reshape(B, 1, -1),
                                       wts["w_conv"], wts["b_conv"], wts["g_mnorm"], B=B, T=T)
    ohg, s1 = _hgrn(proj_hi, wts["lb_logits"], s0, wts["g_hnorm"], B=B, T=T)
    x1, h2 = _merge(hmg, ohg, proj_hi, x2d, wts["w_a"], wts["w_b"], wts["w_o"], wts["g_ffn"])
    y = _ffn(x1, h2, wts["w_ff1"], wts["w_ff2"], wts["g_final"])
    return (y.reshape(B, T, d), new_conv[None], c1[None], n1[None], m1.reshape(1, B, -1), s1[None])


def kernel(x_prompt, x_sample, cache_mlstm_conv, state_mlstm_C, state_mlstm_n, state_mlstm_m, state_hgrn_S,
           g_mix, w_in, b_if, w_conv, b_conv, g_mnorm, g_hnorm, hgrn_lb_logits, w_branch_a, w_branch_b,
           w_out, g_ffn, w_ff1, w_ff2, g_final):
    depth = w_in.shape[0]
    assert depth == 1, "single-layer trunk only"
    d = x_prompt.shape[-1]
    dm = w_branch_a.shape[1]
    dhh = w_branch_b.shape[1]
    nh_h = dhh // DH_H
    dh_m = dm // H_M
    assert hgrn_lb_logits.shape[0] == depth + 1
    assert 2 * dm == d and dhh == dm

    w = w_in[0]
    n_if = 2 * H_M
    p_if = 4 * dm
    w_all = w.astype(BF16)
    w_hi = w_all[:, p_if + n_if:]
    w_if = jnp.pad(w_all[:, p_if:p_if + n_if], ((0, 0), (0, LANES - n_if)))
    bif = jnp.pad(b_if[0].astype(F32), (0, LANES - n_if)).reshape(1, LANES)
    wts = dict(
        g_mix=g_mix[0].reshape(1, d), w_all=w_all, n_lo=p_if, w_hi=w_hi, w_if=w_if, b_if=bif,
        w_conv=w_conv[0], b_conv=b_conv[0].reshape(1, -1), g_mnorm=g_mnorm[0].reshape(1, dm),
        lb_logits=hgrn_lb_logits.astype(F32), g_hnorm=g_hnorm[0].reshape(1, dhh),
        w_a=w_branch_a[0].astype(BF16), w_b=w_branch_b[0].astype(BF16), w_o=w_out[0].astype(BF16),
        g_ffn=g_ffn[0].reshape(1, d), w_ff1=w_ff1[0].astype(BF16), w_ff2=w_ff2[0].astype(BF16),
        g_final=g_final.reshape(1, d),
    )

    bp = x_prompt.shape[0]
    zeros = lambda *s: jnp.zeros(s, F32)
    outs_p = _group(x_prompt, zeros(bp, CONV_W - 1, 2 * dm), zeros(bp, H_M, dh_m, dh_m), zeros(bp, H_M, dh_m),
                    zeros(bp, H_M), zeros(bp, nh_h, DH_H, DH_H), wts)
    outs_s = _group(x_sample, cache_mlstm_conv[0], state_mlstm_C[0].astype(F32), state_mlstm_n[0].astype(F32),
                    state_mlstm_m[0].astype(F32), state_hgrn_S[0].astype(F32), wts)
    return (outs_p[0], outs_s[0]) + outs_p[1:] + outs_s[1:]
```

```python
import functools

import jax
import jax.numpy as jnp
from jax import lax
from jax.experimental import pallas as pl
from jax.experimental.pallas import tpu as pltpu

F32 = jnp.float32
BF16 = jnp.bfloat16
HIGHEST = lax.Precision.HIGHEST

EPS = 1e-6
CHUNK = 64
MLSTM_CHUNK = 128
MLSTM_CPS = 2
CONV_W = 4
H_M = 4
DH_H = 128
SUB = 32
HGRN_CPS = 8
SAFE_SPAN = 60.0
LANES = 128
CONV_PAD = 8
FFN_PIECE = 256
VMEM_LIMIT = 56 * 1024 * 1024

NT_DIMS = (((1,), (1,)), ((), ()))
TN_DIMS = (((0,), (0,)), ((), ()))


def _rms(x, g):
    return x * lax.rsqrt(jnp.mean(x * x, axis=-1, keepdims=True) + EPS) * g


def _sigmoid(x):
    return 0.5 * jnp.tanh(0.5 * x) + 0.5


def _params(sem):
    return pltpu.CompilerParams(dimension_semantics=sem, vmem_limit_bytes=VMEM_LIMIT)


def _norm_kernel(x_ref, g_ref, wif_ref, bif_ref, h_ref, gates_ref):
    hb = _rms(x_ref[...], g_ref[...]).astype(BF16)
    h_ref[...] = hb
    gates_ref[...] = jnp.dot(hb, wif_ref[...], preferred_element_type=F32) + bif_ref[...]


def _matmul_kernel(h_ref, w_ref, o_ref):
    o_ref[...] = jnp.dot(h_ref[...], w_ref[...], preferred_element_type=F32)


def _matmul(h, w, n, name):
    m, d = h.shape
    tm = min(m, 1024)
    tn = 2048
    return pl.pallas_call(
        _matmul_kernel,
        grid=(m // tm, n // tn),
        in_specs=[
            pl.BlockSpec((tm, d), lambda i, j: (i, 0)),
            pl.BlockSpec((d, tn), lambda i, j: (0, j)),
        ],
        out_specs=pl.BlockSpec((tm, tn), lambda i, j: (i, j)),
        out_shape=jax.ShapeDtypeStruct((m, n), F32),
        compiler_params=_params(("parallel", "arbitrary")),
        name=name,
    )(h, w)


def _inproj(x2d, g_mix, w_all, n_lo, w_hi, w_if, b_if):
    m, d = x2d.shape
    tm = min(m, 1024)
    h, gates = pl.pallas_call(
        _norm_kernel,
        grid=(m // tm,),
        in_specs=[
            pl.BlockSpec((tm, d), lambda i: (i, 0)),
            pl.BlockSpec((1, d), lambda i: (0, 0)),
            pl.BlockSpec((d, LANES), lambda i: (0, 0)),
            pl.BlockSpec((1, LANES), lambda i: (0, 0)),
        ],
        out_specs=[pl.BlockSpec((tm, d), lambda i: (i, 0)), pl.BlockSpec((tm, LANES), lambda i: (i, 0))],
        out_shape=[jax.ShapeDtypeStruct((m, d), BF16), jax.ShapeDtypeStruct((m, LANES), F32)],
        compiler_params=_params(("parallel",)),
        name="norm",
    )(x2d, g_mix, w_if, b_if)
    return _matmul(h, w_all, n_lo, "inproj_lo"), _matmul(h, w_hi, w_hi.shape[1], "inproj_hi"), gates


def _mlstm_kernel(qk_ref, v_ref, o_ref, gates_ref, convprev_ref, c0_ref, n0_ref, m0_ref,
                  wconv_ref, bconv_ref, gnorm_ref,
                  hmg_ref, newconv_ref, c1_ref, n1_ref, m1_ref,
                  xpad, qk_scr, c_scr, n_scr, m_scr, *, L, dm, dh, cps):
    c = pl.program_id(1)
    nc = pl.num_programs(1)
    hist0 = CONV_PAD - (CONV_W - 1)
    groups = [slice(g * LANES, (g + 1) * LANES) for g in range(2 * dm // LANES)]

    @pl.when(c == 0)
    def _():
        for g, gl in enumerate(groups):
            xpad[g, hist0:CONV_PAD, :] = convprev_ref[:, gl]
        c_scr[...] = c0_ref[...]
        n_scr[...] = n0_ref[...]
        m_scr[...] = m0_ref[...]

    rr = lax.broadcasted_iota(jnp.int32, (L, L), 0)
    cc = lax.broadcasted_iota(jnp.int32, (L, L), 1)
    causal = rr >= cc
    tril = causal.astype(F32)
    eye = (lax.broadcasted_iota(jnp.int32, (8, LANES), 0)
           == lax.broadcasted_iota(jnp.int32, (8, LANES), 1)).astype(F32)
    heads = [slice(h * dh, (h + 1) * dh) for h in range(H_M)]
    k_scale = dh ** -0.5

    for j in range(cps):
        cr = slice(j * L, (j + 1) * L)
        x0 = CONV_PAD + j * L
        ns = L // 8
        for g, gl in enumerate(groups):
            xpad[g, x0:x0 + L, :] = qk_ref[cr, gl]
            slabs = {s: xpad[g, pl.ds(x0 + s, ns, stride=8), :] for s in range(-(CONV_W - 1), 8)}
            for r in range(8):
                conv = bconv_ref[:, gl]
                for i in range(CONV_W):
                    conv = conv + slabs[r - (CONV_W - 1) + i] * wconv_ref[i:i + 1, gl]
                qk_scr[g, pl.ds(r, ns, stride=8), :] = conv * _sigmoid(conv)
        qk = jnp.concatenate([qk_scr[g] for g in range(len(groups))], axis=1)
        q = qk[:, :dm]
        k = qk[:, dm:]
        qb = q.astype(BF16)
        kb = k.astype(BF16)
        vv = v_ref[cr, :]
        vb = vv.astype(BF16)

        g = gates_ref[cr, :]
        col = lax.broadcasted_iota(jnp.int32, g.shape, 1)
        lf = jnp.minimum(g, 0.0) - jnp.log1p(jnp.exp(-jnp.abs(g)))
        x_cols = jnp.where(col < H_M, g, lf)
        cum = jnp.dot(tril, x_cols, preferred_element_type=F32, precision=HIGHEST)
        xc = jnp.where(col < H_M, g, cum)
        rows = lax.dot_general(eye, xc, NT_DIMS, preferred_element_type=F32, precision=HIGHEST)

        m_all = m_scr[...]
        e_s, sc_inter, inv_floor, decay, wgt, m_new = [], [], [], [], [], []
        for h in range(H_M):
            bcol = cum[:, H_M + h:H_M + h + 1]
            igcol = g[:, h:h + 1]
            brow = rows[H_M + h:H_M + h + 1, :]
            igrow = rows[h:h + 1, :]
            m_prev = m_all[:, h:h + 1]
            dlog = jnp.where(causal, bcol - brow + igrow, -jnp.inf)
            m_inter = bcol + m_prev
            m_t = jnp.maximum(m_inter, jnp.max(dlog, axis=-1, keepdims=True))
            e_s.append(jnp.exp(dlog - m_t) * k_scale)
            sc_inter.append(jnp.exp(m_inter - m_t))
            inv_floor.append(jnp.exp(-m_t))
            mn = m_t[L - 1:L, :]
            b_last = bcol[L - 1:L, :]
            m_new.append(mn)
            decay.append(jnp.exp(b_last + m_prev - mn))
            wgt.append(jnp.exp(b_last - bcol + igcol - mn) * k_scale)

        qn = []
        for h, sl in enumerate(heads):
            n_old = n_scr[h:h + 1, :]
            qn.append(jnp.sum(q[:, sl] * n_old, axis=-1, keepdims=True))
            n_scr[h:h + 1, :] = decay[h] * n_old + jnp.sum(wgt[h] * k[:, sl], axis=0, keepdims=True)
            m_scr[:, h:h + 1] = m_new[h]

        c_old = [c_scr[h] for h in range(H_M)]
        s_raw = [lax.dot_general(qb[:, sl], kb[:, sl], NT_DIMS, preferred_element_type=F32) for sl in heads]
        qc = [lax.dot_general(qb[:, sl], c_old[h].astype(BF16), NT_DIMS, preferred_element_type=F32)
              for h, sl in enumerate(heads)]
        upd = [lax.dot_general((vv[:, sl] * wgt[h]).astype(BF16), kb[:, sl], TN_DIMS, preferred_element_type=F32)
               for h, sl in enumerate(heads)]
        for h in range(H_M):
            c_scr[h] = decay[h] * c_old[h] + upd[h]
        s = [s_raw[h] * e_s[h] for h in range(H_M)]
        pv = [jnp.dot(s[h].astype(BF16), vb[:, sl], preferred_element_type=F32) for h, sl in enumerate(heads)]
        oo = o_ref[cr, :]
        for h, sl in enumerate(heads):
            num = sc_inter[h] * qc[h] + pv[h]
            den = sc_inter[h] * qn[h] + jnp.sum(s[h], axis=-1, keepdims=True)
            hout = num / jnp.maximum(jnp.abs(den), inv_floor[h])
            y = _rms(hout, gnorm_ref[:, sl]) * _sigmoid(oo[:, sl])
            hmg_ref[cr, sl] = y.astype(BF16)

    x_end = CONV_PAD + cps * L
    hist_g = [xpad[g, x_end - (CONV_W - 1):x_end, :] for g in range(len(groups))]
    for g in range(len(groups)):
        xpad[g, hist0:CONV_PAD, :] = hist_g[g]
    hist = jnp.concatenate(hist_g, axis=1)

    @pl.when(c == nc - 1)
    def _():
        newconv_ref[...] = hist
        c1_ref[...] = c_scr[...]
        n1_ref[...] = n_scr[...]
        m1_ref[...] = m_scr[...]


def _mlstm(proj, gates, conv_prev, c0, n0, m0, w_conv, b_conv, g_mnorm, *, B, T):
    L = MLSTM_CHUNK if T % MLSTM_CHUNK == 0 else min(CHUNK, T)
    nc = T // L
    dm = c0.shape[1] * c0.shape[2]
    dh = c0.shape[2]
    cps = MLSTM_CPS if nc % MLSTM_CPS == 0 else 1
    ns = nc // cps
    row = lambda b, c: b * ns + c
    kern = functools.partial(_mlstm_kernel, L=L, dm=dm, dh=dh, cps=cps)
    return pl.pallas_call(
        kern,
        grid=(B, ns),
        in_specs=[
            pl.BlockSpec((cps * L, 2 * dm), lambda b, c: (row(b, c), 0)),
            pl.BlockSpec((cps * L, dm), lambda b, c: (row(b, c), 2)),
            pl.BlockSpec((cps * L, dm), lambda b, c: (row(b, c), 3)),
            pl.BlockSpec((cps * L, LANES), lambda b, c: (row(b, c), 0)),
            pl.BlockSpec((None, CONV_W - 1, 2 * dm), lambda b, c: (b, 0, 0)),
            pl.BlockSpec((None, H_M, dh, dh), lambda b, c: (b, 0, 0, 0)),
            pl.BlockSpec((None, H_M, dh), lambda b, c: (b, 0, 0)),
            pl.BlockSpec((None, 1, H_M), lambda b, c: (b, 0, 0)),
            pl.BlockSpec((CONV_W, 2 * dm), lambda b, c: (0, 0)),
            pl.BlockSpec((1, 2 * dm), lambda b, c: (0, 0)),
            pl.BlockSpec((1, dm), lambda b, c: (0, 0)),
        ],
        out_specs=[
            pl.BlockSpec((cps * L, dm), lambda b, c: (row(b, c), 0)),
            pl.BlockSpec((None, CONV_W - 1, 2 * dm), lambda b, c: (b, 0, 0)),
            pl.BlockSpec((None, H_M, dh, dh), lambda b, c: (b, 0, 0, 0)),
            pl.BlockSpec((None, H_M, dh), lambda b, c: (b, 0, 0)),
            pl.BlockSpec((None, 1, H_M), lambda b, c: (b, 0, 0)),
        ],
        out_shape=[
            jax.ShapeDtypeStruct((B * T, dm), BF16),
            jax.ShapeDtypeStruct((B, CONV_W - 1, 2 * dm), F32),
            jax.ShapeDtypeStruct((B, H_M, dh, dh), F32),
            jax.ShapeDtypeStruct((B, H_M, dh), F32),
            jax.ShapeDtypeStruct((B, 1, H_M), F32),
        ],
        scratch_shapes=[
            pltpu.VMEM((2 * dm // LANES, CONV_PAD + cps * L, LANES), F32),
            pltpu.VMEM((2 * dm // LANES, L, LANES), F32),
            pltpu.VMEM((H_M, dh, dh), F32),
            pltpu.VMEM((H_M, dh), F32),
            pltpu.VMEM((1, H_M), F32),
        ],
        compiler_params=_params(("parallel", "arbitrary")),
        name="mlstm",
    )(proj, proj, proj, gates, conv_prev, c0, n0, m0, w_conv, b_conv, g_mnorm)


def _hgrn_factors(qq, kk, a, qa, sub, nb):
    L = a.shape[0]
    refs = [jnp.zeros_like(a[0:1, :])] + [a[i * sub - 1:i * sub, :] for i in range(1, nb)]
    qs = []
    ks = []
    for i in range(nb):
        rows = slice(i * sub, (i + 1) * sub)
        nk = (i + 1) * sub
        q_i = qa[rows, :] if i == 0 else (qq[rows, :] * jnp.exp(a[rows, :] - refs[i])).astype(BF16)
        k_i = (kk[:nk, :] * jnp.exp(jnp.minimum(refs[i] - a[:nk, :], SAFE_SPAN))).astype(BF16)
        zq = lambda n: jnp.zeros((n, q_i.shape[1]), BF16)
        qs.append(jnp.concatenate([z for z in (zq(i * sub), q_i, zq(L - nk)) if z.shape[0]], axis=0))
        ks.append(jnp.concatenate([z for z in (k_i, zq(L - nk)) if z.shape[0]], axis=0))
    return refs, qs, ks


def _hgrn_scores(qs, ks, sl, L):
    lhs = jnp.concatenate([q[:, sl] for q in qs], axis=1) if len(qs) > 1 else qs[0][:, sl]
    rhs = jnp.concatenate([k[:, sl] for k in ks], axis=1) if len(ks) > 1 else ks[0][:, sl]
    return lax.dot_general(lhs, rhs, NT_DIMS, preferred_element_type=F32)


def _hgrn_kernel(f_ref, i_ref, q_ref, g_ref, lbl_ref, s0_ref, gnorm_ref,
                 ohg_ref, s1_ref,
                 st_scr, a_scr, k_scr, sc_scr, o_scr, in_scr, *, L, nh, cps):
    c = pl.program_id(1)
    nc = pl.num_programs(1)
    sub = min(SUB, L)
    nb = L // sub

    @pl.when(c == 0)
    def _():
        for h in range(nh):
            st_scr[h] = s0_ref[h].T

    lbl = lbl_ref[...]
    mx = jnp.max(lbl, axis=0, keepdims=True)
    ex = jnp.exp(lbl - mx)
    lb = ex[0:1, :] / jnp.sum(ex, axis=0, keepdims=True)
    gnorm = gnorm_ref[...]
    rr = lax.broadcasted_iota(jnp.int32, (L, L), 0)
    cc = lax.broadcasted_iota(jnp.int32, (L, L), 1)
    causal = cc <= rr
    tril = causal.astype(BF16)
    tril3 = jnp.concatenate([tril, tril, tril], axis=1)

    def finish(rows):
        gg = g_ref[rows, :]
        ohg_ref[rows, :] = (_rms(o_scr[rows, :], gnorm) * (gg * _sigmoid(gg))).astype(BF16)

    span = None
    for j in range(cps):
        cr = slice(j * L, (j + 1) * L)
        f = lb + (1.0 - lb) * _sigmoid(f_ref[cr, :])
        lf = jnp.log(f)
        kk = 1.0 - f
        qq = q_ref[cr, :]
        vb = i_ref[cr, :].astype(BF16)

        hi = lf.astype(BF16)
        r1 = lf - hi.astype(F32)
        mid = r1.astype(BF16)
        lo = (r1 - mid.astype(F32)).astype(BF16)
        a = jnp.dot(tril3, jnp.concatenate([hi, mid, lo], axis=0), preferred_element_type=F32)
        a_scr[cr, :] = a
        k_scr[cr, :] = kk

        a_last = a[L - 1:L, :]
        qa = (qq * jnp.exp(a)).astype(BF16)
        kdec = (kk * jnp.exp(a_last - a)).astype(BF16)
        dec_last = jnp.exp(a_last)
        refs, qs, ks = _hgrn_factors(qq, kk, a, qa, sub, nb)
        for i in range(nb):
            sp = refs[i] - a[(i + 1) * sub - 1:(i + 1) * sub, :]
            span = sp if span is None else jnp.maximum(span, sp)

        heads = [slice(h * DH_H, (h + 1) * DH_H) for h in range(nh)]
        st_old = [st_scr[h] for h in range(nh)]
        inter = [lax.dot_general(qa[:, sl], st_old[h].astype(BF16), NT_DIMS, preferred_element_type=F32)
                 for h, sl in enumerate(heads)]
        raw = [_hgrn_scores(qs, ks, sl, L) for sl in heads]
        upd = [lax.dot_general(vb[:, sl], kdec[:, sl], TN_DIMS, preferred_element_type=F32) for sl in heads]
        for h, sl in enumerate(heads):
            st_scr[h] = st_old[h] * dec_last[:, sl] + upd[h]
            in_scr[cr, sl] = inter[h]
        blk = [jnp.where(causal, r, 0.0).astype(BF16) for r in raw]
        for h, sl in enumerate(heads):
            o_scr[cr, sl] = inter[h] + jnp.dot(blk[h], vb[:, sl], preferred_element_type=F32)
        finish(cr)

    use_exact = jnp.logical_not(jnp.max(span) <= SAFE_SPAN)

    @pl.when(use_exact)
    def _():
        sc_scr[...] = jnp.zeros_like(sc_scr)
        lane = lax.broadcasted_iota(jnp.int32, (sub, LANES), 1)
        in_block = jnp.logical_and(causal, (cc // sub) == (rr // sub))

        def fix_chunk(j, carry):
            base = pl.multiple_of(j * L, L)

            def diag_body(s, carry2):
                r0 = pl.multiple_of((s // sub) * sub, sub)
                a_blk = a_scr[pl.ds(base + r0, sub), :]
                q_blk = q_ref[pl.ds(base + r0, sub), :]
                a_s = a_scr[pl.ds(base + s, 1), :]
                k_s = k_scr[pl.ds(base + s, 1), :]
                p = q_blk * jnp.exp(jnp.minimum(a_blk - a_s, 0.0)) * k_s
                for h in range(nh):
                    colv = jnp.sum(p[:, h * DH_H:(h + 1) * DH_H], axis=-1, keepdims=True)
                    old = sc_scr[h, pl.ds(r0, sub), :]
                    sc_scr[h, pl.ds(r0, sub), :] = jnp.where(lane == s, colv, old)
                return carry2

            lax.fori_loop(0, L, diag_body, 0)

            cr = pl.ds(base, L)
            a = a_scr[cr, :]
            kk = k_scr[cr, :]
            qq = q_ref[cr, :]
            vb = i_ref[cr, :].astype(BF16)
            qa = (qq * jnp.exp(a)).astype(BF16)
            _, qs, ks = _hgrn_factors(qq, kk, a, qa, sub, nb)
            for h in range(nh):
                sl = slice(h * DH_H, (h + 1) * DH_H)
                blk = jnp.where(causal, _hgrn_scores(qs, ks, sl, L), 0.0)
                blk = jnp.where(in_block, sc_scr[h][:, :L], blk).astype(BF16)
                o_scr[cr, sl] = in_scr[cr, sl] + jnp.dot(blk, vb[:, sl], preferred_element_type=F32)
            finish(cr)
            return carry

        lax.fori_loop(0, cps, fix_chunk, 0)

    @pl.when(c == nc - 1)
    def _():
        for h in range(nh):
            s1_ref[h] = st_scr[h].T


def _hgrn(proj, lb_logits, s0, g_hnorm, *, B, T):
    L = min(CHUNK, T)
    nc = T // L
    cps = HGRN_CPS if nc % HGRN_CPS == 0 else 1
    ns = nc // cps
    nh = s0.shape[1]
    dhh = nh * DH_H
    row = lambda b, c: b * ns + c
    kern = functools.partial(_hgrn_kernel, L=L, nh=nh, cps=cps)
    return pl.pallas_call(
        kern,
        grid=(B, ns),
        in_specs=[
            pl.BlockSpec((cps * L, dhh), lambda b, c: (row(b, c), 0)),
            pl.BlockSpec((cps * L, dhh), lambda b, c: (row(b, c), 1)),
            pl.BlockSpec((cps * L, dhh), lambda b, c: (row(b, c), 2)),
            pl.BlockSpec((cps * L, dhh), lambda b, c: (row(b, c), 3)),
            pl.BlockSpec((2, dhh), lambda b, c: (0, 0)),
            pl.BlockSpec((None, nh, DH_H, DH_H), lambda b, c: (b, 0, 0, 0)),
            pl.BlockSpec((1, dhh), lambda b, c: (0, 0)),
        ],
        out_specs=[
            pl.BlockSpec((cps * L, dhh), lambda b, c: (row(b, c), 0)),
            pl.BlockSpec((None, nh, DH_H, DH_H), lambda b, c: (b, 0, 0, 0)),
        ],
        out_shape=[
            jax.ShapeDtypeStruct((B * T, dhh), BF16),
            jax.ShapeDtypeStruct((B, nh, DH_H, DH_H), F32),
        ],
        scratch_shapes=[
            pltpu.VMEM((nh, DH_H, DH_H), F32),
            pltpu.VMEM((cps * L, dhh), F32),
            pltpu.VMEM((cps * L, dhh), F32),
            pltpu.VMEM((nh, L, LANES), F32),
            pltpu.VMEM((cps * L, dhh), F32),
            pltpu.VMEM((cps * L, dhh), F32),
        ],
        compiler_params=_params(("parallel", "arbitrary")),
        name="hgrn",
    )(proj, proj, proj, proj, lb_logits, s0, g_hnorm)


def _merge_kernel(hmg_ref, ohg_ref, ga_ref, gb_ref, x_ref, wa_ref, wb_ref, wo_ref, gffn_ref, x1_ref, h2_ref):
    ya = jnp.dot(hmg_ref[...], wa_ref[...], preferred_element_type=F32)
    yb = jnp.dot(ohg_ref[...], wb_ref[...], preferred_element_type=F32)
    u = _sigmoid(ga_ref[...]) * ya + _sigmoid(gb_ref[...]) * yb
    x1 = x_ref[...] + jnp.dot(u.astype(BF16), wo_ref[...], preferred_element_type=F32)
    x1_ref[...] = x1
    h2_ref[...] = _rms(x1, gffn_ref[...]).astype(BF16)


def _merge(hmg, ohg, proj, x2d, w_a, w_b, w_o, g_ffn):
    m, d = x2d.shape
    dm = hmg.shape[1]
    tm = min(m, 256)
    const = lambda i: (0, 0)
    return pl.pallas_call(
        _merge_kernel,
        grid=(m // tm,),
        in_specs=[
            pl.BlockSpec((tm, dm), lambda i: (i, 0)),
            pl.BlockSpec((tm, dm), lambda i: (i, 0)),
            pl.BlockSpec((tm, d), lambda i: (i, 2)),
            pl.BlockSpec((tm, d), lambda i: (i, 3)),
            pl.BlockSpec((tm, d), lambda i: (i, 0)),
            pl.BlockSpec((dm, d), const, pipeline_mode=pl.Buffered(1)),
            pl.BlockSpec((dm, d), const, pipeline_mode=pl.Buffered(1)),
            pl.BlockSpec((d, d), const, pipeline_mode=pl.Buffered(1)),
            pl.BlockSpec((1, d), const),
        ],
        out_specs=[pl.BlockSpec((tm, d), lambda i: (i, 0)), pl.BlockSpec((tm, d), lambda i: (i, 0))],
        out_shape=[jax.ShapeDtypeStruct((m, d), F32), jax.ShapeDtypeStruct((m, d), BF16)],
        compiler_params=_params(("parallel",)),
        name="merge",
    )(hmg, ohg, proj, proj, x2d, w_a, w_b, w_o, g_ffn)


def _ffn_kernel(x1_ref, h2_ref, w1_ref, w2_ref, gfin_ref, y_ref, acc_scr):
    j = pl.program_id(1)

    h = h2_ref[...]
    tf = w1_ref.shape[1]
    pieces = [slice(p, p + FFN_PIECE) for p in range(0, tf, FFN_PIECE)]
    ts = []
    for ps in pieces:
        t = jnp.maximum(jnp.dot(h, w1_ref[:, ps], preferred_element_type=F32), 0.0)
        ts.append((t * t).astype(BF16))
    acc = jnp.where(j > 0, acc_scr[...], 0.0)
    for ps, t in zip(pieces, ts):
        acc = acc + jnp.dot(t, w2_ref[ps, :], preferred_element_type=F32)
    acc_scr[...] = acc

    @pl.when(j == pl.num_programs(1) - 1)
    def _():
        y_ref[...] = _rms(x1_ref[...] + acc_scr[...], gfin_ref[...])


def _ffn(x1, h2, w1, w2, g_final):
    m, d = x1.shape
    dff = w1.shape[1]
    tm = min(m, 512)
    tf = 1024
    return pl.pallas_call(
        _ffn_kernel,
        grid=(m // tm, dff // tf),
        in_specs=[
            pl.BlockSpec((tm, d), lambda i, j: (i, 0)),
            pl.BlockSpec((tm, d), lambda i, j: (i, 0)),
            pl.BlockSpec((d, tf), lambda i, j: (0, j)),
            pl.BlockSpec((tf, d), lambda i, j: (j, 0)),
            pl.BlockSpec((1, d), lambda i, j: (0, 0)),
        ],
        out_specs=pl.BlockSpec((tm, d), lambda i, j: (i, 0)),
        out_shape=jax.ShapeDtypeStruct((m, d), F32),
        scratch_shapes=[pltpu.VMEM((tm, d), F32)],
        compiler_params=_params(("parallel", "arbitrary")),
        name="ffn",
    )(x1, h2, w1, w2, g_final)


def _group(x, conv_prev, c0, n0, m0, s0, wts):
    B, T, d = x.shape
    x2d = x.reshape(B * T, d)
    proj_lo, proj_hi, gates = _inproj(x2d, wts["g_mix"], wts["w_all"], wts["n_lo"], wts["w_hi"], wts["w_if"], wts["b_if"])
    hmg, new_conv, c1, n1, m1 = _mlstm(proj_lo, gates, conv_prev, c0, n0, m0.reshape(B, 1, -1),
                                       wts["w_conv"], wts["b_conv"], wts["g_mnorm"], B=B, T=T)
    ohg, s1 = _hgrn(proj_hi, wts["lb_logits"], s0, wts["g_hnorm"], B=B, T=T)
    x1, h2 = _merge(hmg, ohg, proj_hi, x2d, wts["w_a"], wts["w_b"], wts["w_o"], wts["g_ffn"])
    y = _ffn(x1, h2, wts["w_ff1"], wts["w_ff2"], wts["g_final"])
    return (y.reshape(B, T, d), new_conv[None], c1[None], n1[None], m1.reshape(1, B, -1), s1[None])


def kernel(x_prompt, x_sample, cache_mlstm_conv, state_mlstm_C, state_mlstm_n, state_mlstm_m, state_hgrn_S,
           g_mix, w_in, b_if, w_conv, b_conv, g_mnorm, g_hnorm, hgrn_lb_logits, w_branch_a, w_branch_b,
           w_out, g_ffn, w_ff1, w_ff2, g_final):
    depth = w_in.shape[0]
    assert depth == 1, "single-layer trunk only"
    d = x_prompt.shape[-1]
    dm = w_branch_a.shape[1]
    dhh = w_branch_b.shape[1]
    nh_h = dhh // DH_H
    dh_m = dm // H_M
    assert hgrn_lb_logits.shape[0] == depth + 1
    assert 2 * dm == d and dhh == dm

    w = w_in[0]
    n_if = 2 * H_M
    p_if = 4 * dm
    w_all = w.astype(BF16)
    w_hi = w_all[:, p_if + n_if:]
    w_if = jnp.pad(w_all[:, p_if:p_if + n_if], ((0, 0), (0, LANES - n_if)))
    bif = jnp.pad(b_if[0].astype(F32), (0, LANES - n_if)).reshape(1, LANES)
    wts = dict(
        g_mix=g_mix[0].reshape(1, d), w_all=w_all, n_lo=p_if, w_hi=w_hi, w_if=w_if, b_if=bif,
        w_conv=w_conv[0], b_conv=b_conv[0].reshape(1, -1), g_mnorm=g_mnorm[0].reshape(1, dm),
        lb_logits=hgrn_lb_logits.astype(F32), g_hnorm=g_hnorm[0].reshape(1, dhh),
        w_a=w_branch_a[0].astype(BF16), w_b=w_branch_b[0].astype(BF16), w_o=w_out[0].astype(BF16),
        g_ffn=g_ffn[0].reshape(1, d), w_ff1=w_ff1[0].astype(BF16), w_ff2=w_ff2[0].astype(BF16),
        g_final=g_final.reshape(1, d),
    )

    bp = x_prompt.shape[0]
    zeros = lambda *s: jnp.zeros(s, F32)
    outs_p = _group(x_prompt, zeros(bp, CONV_W - 1, 2 * dm), zeros(bp, H_M, dh_m, dh_m), zeros(bp, H_M, dh_m),
                    zeros(bp, H_M), zeros(bp, nh_h, DH_H, DH_H), wts)
    outs_s = _group(x_sample, cache_mlstm_conv[0], state_mlstm_C[0].astype(F32), state_mlstm_n[0].astype(F32),
                    state_mlstm_m[0].astype(F32), state_hgrn_S[0].astype(F32), wts)
    return (outs_p[0], outs_s[0]) + outs_p[1:] + outs_s[1:]
```

```python
import functools

import jax
import jax.numpy as jnp
from jax import lax
from jax.experimental import pallas as pl
from jax.experimental.pallas import tpu as pltpu

F32 = jnp.float32
BF16 = jnp.bfloat16
HIGHEST = lax.Precision.HIGHEST

EPS = 1e-6
CHUNK = 64
MLSTM_CHUNK = 128
MLSTM_CPS = 2
CONV_W = 4
H_M = 4
DH_H = 128
SUB = 32
HGRN_CPS = 8
SAFE_SPAN = 60.0
LANES = 128
CONV_PAD = 8
FFN_PIECE = 256
VMEM_LIMIT = 56 * 1024 * 1024

MATMUL_ROWS, MATMUL_COLS = 1024, 2048
NORM_ROWS = 1024
MERGE_ROWS = 256
FFN_ROWS, FFN_HIDDEN = 512, 1024

NT_DIMS = (((1,), (1,)), ((), ()))
TN_DIMS = (((0,), (0,)), ((), ()))


def _rms(x, g):
    return x * lax.rsqrt(jnp.mean(x * x, axis=-1, keepdims=True) + EPS) * g


def _sigmoid(x):
    return 0.5 * jnp.tanh(0.5 * x) + 0.5


def _params(sem):
    return pltpu.CompilerParams(dimension_semantics=sem, vmem_limit_bytes=VMEM_LIMIT)


def _norm_kernel(x_ref, g_ref, wif_ref, bif_ref, h_ref, gates_ref):
    hb = _rms(x_ref[...], g_ref[...]).astype(BF16)
    h_ref[...] = hb
    gates_ref[...] = jnp.dot(hb, wif_ref[...], preferred_element_type=F32) + bif_ref[...]


def _matmul_kernel(h_ref, w_ref, o_ref):
    o_ref[...] = jnp.dot(h_ref[...], w_ref[...], preferred_element_type=F32)


def _matmul(h, w, n, name):
    m, d = h.shape
    tm = min(m, MATMUL_ROWS)
    tn = MATMUL_COLS
    return pl.pallas_call(
        _matmul_kernel,
        grid=(m // tm, n // tn),
        in_specs=[
            pl.BlockSpec((tm, d), lambda i, j: (i, 0)),
            pl.BlockSpec((d, tn), lambda i, j: (0, j)),
        ],
        out_specs=pl.BlockSpec((tm, tn), lambda i, j: (i, j)),
        out_shape=jax.ShapeDtypeStruct((m, n), F32),
        compiler_params=_params(("parallel", "arbitrary")),
        name=name,
    )(h, w)


def _inproj(x2d, g_mix, w_all, n_lo, w_hi, w_if, b_if):
    m, d = x2d.shape
    tm = min(m, NORM_ROWS)
    h, gates = pl.pallas_call(
        _norm_kernel,
        grid=(m // tm,),
        in_specs=[
            pl.BlockSpec((tm, d), lambda i: (i, 0)),
            pl.BlockSpec((1, d), lambda i: (0, 0)),
            pl.BlockSpec((d, LANES), lambda i: (0, 0)),
            pl.BlockSpec((1, LANES), lambda i: (0, 0)),
        ],
        out_specs=[pl.BlockSpec((tm, d), lambda i: (i, 0)), pl.BlockSpec((tm, LANES), lambda i: (i, 0))],
        out_shape=[jax.ShapeDtypeStruct((m, d), BF16), jax.ShapeDtypeStruct((m, LANES), F32)],
        compiler_params=_params(("parallel",)),
        name="norm",
    )(x2d, g_mix, w_if, b_if)
    return _matmul(h, w_all, n_lo, "inproj_lo"), _matmul(h, w_hi, w_hi.shape[1], "inproj_hi"), gates


def _mlstm_kernel(qk_ref, v_ref, o_ref, gates_ref, convprev_ref, c0_ref, n0_ref, m0_ref,
                  wconv_ref, bconv_ref, gnorm_ref,
                  hmg_ref, newconv_ref, c1_ref, n1_ref, m1_ref,
                  xpad, qk_scr, c_scr, n_scr, m_scr, *, L, dm, dh, cps):
    c = pl.program_id(1)
    nc = pl.num_programs(1)
    hist0 = CONV_PAD - (CONV_W - 1)
    groups = [slice(g * LANES, (g + 1) * LANES) for g in range(2 * dm // LANES)]

    @pl.when(c == 0)
    def _():
        for g, gl in enumerate(groups):
            xpad[g, hist0:CONV_PAD, :] = convprev_ref[:, gl]
        c_scr[...] = c0_ref[...]
        n_scr[...] = n0_ref[...]
        m_scr[...] = m0_ref[...]

    rr = lax.broadcasted_iota(jnp.int32, (L, L), 0)
    cc = lax.broadcasted_iota(jnp.int32, (L, L), 1)
    causal = rr >= cc
    tril = causal.astype(F32)
    eye = (lax.broadcasted_iota(jnp.int32, (8, LANES), 0)
           == lax.broadcasted_iota(jnp.int32, (8, LANES), 1)).astype(F32)
    heads = [slice(h * dh, (h + 1) * dh) for h in range(H_M)]
    k_scale = dh ** -0.5

    for j in range(cps):
        cr = slice(j * L, (j + 1) * L)
        x0 = CONV_PAD + j * L
        ns = L // 8
        for g, gl in enumerate(groups):
            xpad[g, x0:x0 + L, :] = qk_ref[cr, gl]
            slabs = {s: xpad[g, pl.ds(x0 + s, ns, stride=8), :] for s in range(-(CONV_W - 1), 8)}
            for r in range(8):
                conv = bconv_ref[:, gl]
                for i in range(CONV_W):
                    conv = conv + slabs[r - (CONV_W - 1) + i] * wconv_ref[i:i + 1, gl]
                qk_scr[g, pl.ds(r, ns, stride=8), :] = conv * _sigmoid(conv)
        qk = jnp.concatenate([qk_scr[g] for g in range(len(groups))], axis=1)
        q = qk[:, :dm]
        k = qk[:, dm:]
        qb = q.astype(BF16)
        kb = k.astype(BF16)
        vv = v_ref[cr, :]
        vb = vv.astype(BF16)

        g = gates_ref[cr, :]
        col = lax.broadcasted_iota(jnp.int32, g.shape, 1)
        lf = jnp.minimum(g, 0.0) - jnp.log1p(jnp.exp(-jnp.abs(g)))
        x_cols = jnp.where(col < H_M, g, lf)
        cum = jnp.dot(tril, x_cols, preferred_element_type=F32, precision=HIGHEST)
        xc = jnp.where(col < H_M, g, cum)
        rows = lax.dot_general(eye, xc, NT_DIMS, preferred_element_type=F32, precision=HIGHEST)

        m_all = m_scr[...]
        e_s, sc_inter, inv_floor, decay, wgt, m_new = [], [], [], [], [], []
        for h in range(H_M):
            bcol = cum[:, H_M + h:H_M + h + 1]
            igcol = g[:, h:h + 1]
            brow = rows[H_M + h:H_M + h + 1, :]
            igrow = rows[h:h + 1, :]
            m_prev = m_all[:, h:h + 1]
            dlog = jnp.where(causal, bcol - brow + igrow, -jnp.inf)
            m_inter = bcol + m_prev
            m_t = jnp.maximum(m_inter, jnp.max(dlog, axis=-1, keepdims=True))
            e_s.append(jnp.exp(dlog - m_t) * k_scale)
            sc_inter.append(jnp.exp(m_inter - m_t))
            inv_floor.append(jnp.exp(-m_t))
            mn = m_t[L - 1:L, :]
            b_last = bcol[L - 1:L, :]
            m_new.append(mn)
            decay.append(jnp.exp(b_last + m_prev - mn))
            wgt.append(jnp.exp(b_last - bcol + igcol - mn) * k_scale)

        qn = []
        for h, sl in enumerate(heads):
            n_old = n_scr[h:h + 1, :]
            qn.append(jnp.sum(q[:, sl] * n_old, axis=-1, keepdims=True))
            n_scr[h:h + 1, :] = decay[h] * n_old + jnp.sum(wgt[h] * k[:, sl], axis=0, keepdims=True)
            m_scr[:, h:h + 1] = m_new[h]

        c_old = [c_scr[h] for h in range(H_M)]
        s_raw = [lax.dot_general(qb[:, sl], kb[:, sl], NT_DIMS, preferred_element_type=F32) for sl in heads]
        qc = [lax.dot_general(qb[:, sl], c_old[h].astype(BF16), NT_DIMS, preferred_element_type=F32)
              for h, sl in enumerate(heads)]
        upd = [lax.dot_general((vv[:, sl] * wgt[h]).astype(BF16), kb[:, sl], TN_DIMS, preferred_element_type=F32)
               for h, sl in enumerate(heads)]
        for h in range(H_M):
            c_scr[h] = decay[h] * c_old[h] + upd[h]
        s = [s_raw[h] * e_s[h] for h in range(H_M)]
        pv = [jnp.dot(s[h].astype(BF16), vb[:, sl], preferred_element_type=F32) for h, sl in enumerate(heads)]
        oo = o_ref[cr, :]
        for h, sl in enumerate(heads):
            num = sc_inter[h] * qc[h] + pv[h]
            den = sc_inter[h] * qn[h] + jnp.sum(s[h], axis=-1, keepdims=True)
            hout = num / jnp.maximum(jnp.abs(den), inv_floor[h])
            y = _rms(hout, gnorm_ref[:, sl]) * _sigmoid(oo[:, sl])
            hmg_ref[cr, sl] = y.astype(BF16)

    x_end = CONV_PAD + cps * L
    hist_g = [xpad[g, x_end - (CONV_W - 1):x_end, :] for g in range(len(groups))]
    for g in range(len(groups)):
        xpad[g, hist0:CONV_PAD, :] = hist_g[g]
    hist = jnp.concatenate(hist_g, axis=1)

    @pl.when(c == nc - 1)
    def _():
        newconv_ref[...] = hist
        c1_ref[...] = c_scr[...]
        n1_ref[...] = n_scr[...]
        m1_ref[...] = m_scr[...]


def _mlstm(proj, gates, conv_prev, c0, n0, m0, w_conv, b_conv, g_mnorm, *, B, T):
    L = MLSTM_CHUNK if T % MLSTM_CHUNK == 0 else min(CHUNK, T)
    nc = T // L
    dm = c0.shape[1] * c0.shape[2]
    dh = c0.shape[2]
    cps = MLSTM_CPS if nc % MLSTM_CPS == 0 else 1
    ns = nc // cps
    row = lambda b, c: b * ns + c
    kern = functools.partial(_mlstm_kernel, L=L, dm=dm, dh=dh, cps=cps)
    return pl.pallas_call(
        kern,
        grid=(B, ns),
        in_specs=[
            pl.BlockSpec((cps * L, 2 * dm), lambda b, c: (row(b, c), 0)),
            pl.BlockSpec((cps * L, dm), lambda b, c: (row(b, c), 2)),
            pl.BlockSpec((cps * L, dm), lambda b, c: (row(b, c), 3)),
            pl.BlockSpec((cps * L, LANES), lambda b, c: (row(b, c), 0)),
            pl.BlockSpec((None, CONV_W - 1, 2 * dm), lambda b, c: (b, 0, 0)),
            pl.BlockSpec((None, H_M, dh, dh), lambda b, c: (b, 0, 0, 0)),
            pl.BlockSpec((None, H_M, dh), lambda b, c: (b, 0, 0)),
            pl.BlockSpec((None, 1, H_M), lambda b, c: (b, 0, 0)),
            pl.BlockSpec((CONV_W, 2 * dm), lambda b, c: (0, 0)),
            pl.BlockSpec((1, 2 * dm), lambda b, c: (0, 0)),
            pl.BlockSpec((1, dm), lambda b, c: (0, 0)),
        ],
        out_specs=[
            pl.BlockSpec((cps * L, dm), lambda b, c: (row(b, c), 0)),
            pl.BlockSpec((None, CONV_W - 1, 2 * dm), lambda b, c: (b, 0, 0)),
            pl.BlockSpec((None, H_M, dh, dh), lambda b, c: (b, 0, 0, 0)),
            pl.BlockSpec((None, H_M, dh), lambda b, c: (b, 0, 0)),
            pl.BlockSpec((None, 1, H_M), lambda b, c: (b, 0, 0)),
        ],
        out_shape=[
            jax.ShapeDtypeStruct((B * T, dm), BF16),
            jax.ShapeDtypeStruct((B, CONV_W - 1, 2 * dm), F32),
            jax.ShapeDtypeStruct((B, H_M, dh, dh), F32),
            jax.ShapeDtypeStruct((B, H_M, dh), F32),
            jax.ShapeDtypeStruct((B, 1, H_M), F32),
        ],
        scratch_shapes=[
            pltpu.VMEM((2 * dm // LANES, CONV_PAD + cps * L, LANES), F32),
            pltpu.VMEM((2 * dm // LANES, L, LANES), F32),
            pltpu.VMEM((H_M, dh, dh), F32),
            pltpu.VMEM((H_M, dh), F32),
            pltpu.VMEM((1, H_M), F32),
        ],
        compiler_params=_params(("parallel", "arbitrary")),
        name="mlstm",
    )(proj, proj, proj, gates, conv_prev, c0, n0, m0, w_conv, b_conv, g_mnorm)


def _hgrn_factors(qq, kk, a, qa, sub, nb):
    L = a.shape[0]
    refs = [jnp.zeros_like(a[0:1, :])] + [a[i * sub - 1:i * sub, :] for i in range(1, nb)]
    qs = []
    ks = []
    for i in range(nb):
        rows = slice(i * sub, (i + 1) * sub)
        nk = (i + 1) * sub
        q_i = qa[rows, :] if i == 0 else (qq[rows, :] * jnp.exp(a[rows, :] - refs[i])).astype(BF16)
        k_i = (kk[:nk, :] * jnp.exp(jnp.minimum(refs[i] - a[:nk, :], SAFE_SPAN))).astype(BF16)
        zq = lambda n: jnp.zeros((n, q_i.shape[1]), BF16)
        qs.append(jnp.concatenate([z for z in (zq(i * sub), q_i, zq(L - nk)) if z.shape[0]], axis=0))
        ks.append(jnp.concatenate([z for z in (k_i, zq(L - nk)) if z.shape[0]], axis=0))
    return refs, qs, ks


def _hgrn_scores(qs, ks, sl, L):
    lhs = jnp.concatenate([q[:, sl] for q in qs], axis=1) if len(qs) > 1 else qs[0][:, sl]
    rhs = jnp.concatenate([k[:, sl] for k in ks], axis=1) if len(ks) > 1 else ks[0][:, sl]
    return lax.dot_general(lhs, rhs, NT_DIMS, preferred_element_type=F32)


def _hgrn_kernel(f_ref, i_ref, q_ref, g_ref, lbl_ref, s0_ref, gnorm_ref,
                 ohg_ref, s1_ref,
                 st_scr, a_scr, k_scr, sc_scr, o_scr, in_scr, *, L, nh, cps):
    c = pl.program_id(1)
    nc = pl.num_programs(1)
    sub = min(SUB, L)
    nb = L // sub

    @pl.when(c == 0)
    def _():
        for h in range(nh):
            st_scr[h] = s0_ref[h].T

    lbl = lbl_ref[...]
    mx = jnp.max(lbl, axis=0, keepdims=True)
    ex = jnp.exp(lbl - mx)
    lb = ex[0:1, :] / jnp.sum(ex, axis=0, keepdims=True)
    gnorm = gnorm_ref[...]
    rr = lax.broadcasted_iota(jnp.int32, (L, L), 0)
    cc = lax.broadcasted_iota(jnp.int32, (L, L), 1)
    causal = cc <= rr
    tril = causal.astype(BF16)
    tril3 = jnp.concatenate([tril, tril, tril], axis=1)

    def finish(rows):
        gg = g_ref[rows, :]
        ohg_ref[rows, :] = (_rms(o_scr[rows, :], gnorm) * (gg * _sigmoid(gg))).astype(BF16)

    span = None
    for j in range(cps):
        cr = slice(j * L, (j + 1) * L)
        f = lb + (1.0 - lb) * _sigmoid(f_ref[cr, :])
        lf = jnp.log(f)
        kk = 1.0 - f
        qq = q_ref[cr, :]
        vb = i_ref[cr, :].astype(BF16)

        hi = lf.astype(BF16)
        r1 = lf - hi.astype(F32)
        mid = r1.astype(BF16)
        lo = (r1 - mid.astype(F32)).astype(BF16)
        a = jnp.dot(tril3, jnp.concatenate([hi, mid, lo], axis=0), preferred_element_type=F32)
        a_scr[cr, :] = a
        k_scr[cr, :] = kk

        a_last = a[L - 1:L, :]
        qa = (qq * jnp.exp(a)).astype(BF16)
        kdec = (kk * jnp.exp(a_last - a)).astype(BF16)
        dec_last = jnp.exp(a_last)
        refs, qs, ks = _hgrn_factors(qq, kk, a, qa, sub, nb)
        for i in range(nb):
            sp = refs[i] - a[(i + 1) * sub - 1:(i + 1) * sub, :]
            span = sp if span is None else jnp.maximum(span, sp)

        heads = [slice(h * DH_H, (h + 1) * DH_H) for h in range(nh)]
        st_old = [st_scr[h] for h in range(nh)]
        inter = [lax.dot_general(qa[:, sl], st_old[h].astype(BF16), NT_DIMS, preferred_element_type=F32)
                 for h, sl in enumerate(heads)]
        raw = [_hgrn_scores(qs, ks, sl, L) for sl in heads]
        upd = [lax.dot_general(vb[:, sl], kdec[:, sl], TN_DIMS, preferred_element_type=F32) for sl in heads]
        for h, sl in enumerate(heads):
            st_scr[h] = st_old[h] * dec_last[:, sl] + upd[h]
            in_scr[cr, sl] = inter[h]
        blk = [jnp.where(causal, r, 0.0).astype(BF16) for r in raw]
        for h, sl in enumerate(heads):
            o_scr[cr, sl] = inter[h] + jnp.dot(blk[h], vb[:, sl], preferred_element_type=F32)
        finish(cr)

    use_exact = jnp.logical_not(jnp.max(span) <= SAFE_SPAN)

    @pl.when(use_exact)
    def _():
        sc_scr[...] = jnp.zeros_like(sc_scr)
        lane = lax.broadcasted_iota(jnp.int32, (sub, LANES), 1)
        in_block = jnp.logical_and(causal, (cc // sub) == (rr // sub))

        def fix_chunk(j, carry):
            base = pl.multiple_of(j * L, L)

            def diag_body(s, carry2):
                r0 = pl.multiple_of((s // sub) * sub, sub)
                a_blk = a_scr[pl.ds(base + r0, sub), :]
                q_blk = q_ref[pl.ds(base + r0, sub), :]
                a_s = a_scr[pl.ds(base + s, 1), :]
                k_s = k_scr[pl.ds(base + s, 1), :]
                p = q_blk * jnp.exp(jnp.minimum(a_blk - a_s, 0.0)) * k_s
                for h in range(nh):
                    colv = jnp.sum(p[:, h * DH_H:(h + 1) * DH_H], axis=-1, keepdims=True)
                    old = sc_scr[h, pl.ds(r0, sub), :]
                    sc_scr[h, pl.ds(r0, sub), :] = jnp.where(lane == s, colv, old)
                return carry2

            lax.fori_loop(0, L, diag_body, 0)

            cr = pl.ds(base, L)
            a = a_scr[cr, :]
            kk = k_scr[cr, :]
            qq = q_ref[cr, :]
            vb = i_ref[cr, :].astype(BF16)
            qa = (qq * jnp.exp(a)).astype(BF16)
            _, qs, ks = _hgrn_factors(qq, kk, a, qa, sub, nb)
            for h in range(nh):
                sl = slice(h * DH_H, (h + 1) * DH_H)
                blk = jnp.where(causal, _hgrn_scores(qs, ks, sl, L), 0.0)
                blk = jnp.where(in_block, sc_scr[h][:, :L], blk).astype(BF16)
                o_scr[cr, sl] = in_scr[cr, sl] + jnp.dot(blk, vb[:, sl], preferred_element_type=F32)
            finish(cr)
            return carry

        lax.fori_loop(0, cps, fix_chunk, 0)

    @pl.when(c == nc - 1)
    def _():
        for h in range(nh):
            s1_ref[h] = st_scr[h].T


def _hgrn(proj, lb_logits, s0, g_hnorm, *, B, T):
    L = min(CHUNK, T)
    nc = T // L
    cps = HGRN_CPS if nc % HGRN_CPS == 0 else 1
    ns = nc // cps
    nh = s0.shape[1]
    dhh = nh * DH_H
    row = lambda b, c: b * ns + c
    kern = functools.partial(_hgrn_kernel, L=L, nh=nh, cps=cps)
    return pl.pallas_call(
        kern,
        grid=(B, ns),
        in_specs=[
            pl.BlockSpec((cps * L, dhh), lambda b, c: (row(b, c), 0)),
            pl.BlockSpec((cps * L, dhh), lambda b, c: (row(b, c), 1)),
            pl.BlockSpec((cps * L, dhh), lambda b, c: (row(b, c), 2)),
            pl.BlockSpec((cps * L, dhh), lambda b, c: (row(b, c), 3)),
            pl.BlockSpec((2, dhh), lambda b, c: (0, 0)),
            pl.BlockSpec((None, nh, DH_H, DH_H), lambda b, c: (b, 0, 0, 0)),
            pl.BlockSpec((1, dhh), lambda b, c: (0, 0)),
        ],
        out_specs=[
            pl.BlockSpec((cps * L, dhh), lambda b, c: (row(b, c), 0)),
            pl.BlockSpec((None, nh, DH_H, DH_H), lambda b, c: (b, 0, 0, 0)),
        ],
        out_shape=[
            jax.ShapeDtypeStruct((B * T, dhh), BF16),
            jax.ShapeDtypeStruct((B, nh, DH_H, DH_H), F32),
        ],
        scratch_shapes=[
            pltpu.VMEM((nh, DH_H, DH_H), F32),
            pltpu.VMEM((cps * L, dhh), F32),
            pltpu.VMEM((cps * L, dhh), F32),
            pltpu.VMEM((nh, L, LANES), F32),
            pltpu.VMEM((cps * L, dhh), F32),
            pltpu.VMEM((cps * L, dhh), F32),
        ],
        compiler_params=_params(("parallel", "arbitrary")),
        name="hgrn",
    )(proj, proj, proj, proj, lb_logits, s0, g_hnorm)


def _merge_kernel(hmg_ref, ohg_ref, ga_ref, gb_ref, x_ref, wa_ref, wb_ref, wo_ref, gffn_ref, x1_ref, h2_ref):
    ya = jnp.dot(hmg_ref[...], wa_ref[...], preferred_element_type=F32)
    yb = jnp.dot(ohg_ref[...], wb_ref[...], preferred_element_type=F32)
    u = _sigmoid(ga_ref[...]) * ya + _sigmoid(gb_ref[...]) * yb
    x1 = x_ref[...] + jnp.dot(u.astype(BF16), wo_ref[...], preferred_element_type=F32)
    x1_ref[...] = x1
    h2_ref[...] = _rms(x1, gffn_ref[...]).astype(BF16)


def _merge(hmg, ohg, proj, x2d, w_a, w_b, w_o, g_ffn):
    m, d = x2d.shape
    dm = hmg.shape[1]
    tm = min(m, MERGE_ROWS)
    const = lambda i: (0, 0)
    return pl.pallas_call(
        _merge_kernel,
        grid=(m // tm,),
        in_specs=[
            pl.BlockSpec((tm, dm), lambda i: (i, 0)),
            pl.BlockSpec((tm, dm), lambda i: (i, 0)),
            pl.BlockSpec((tm, d), lambda i: (i, 2)),
            pl.BlockSpec((tm, d), lambda i: (i, 3)),
            pl.BlockSpec((tm, d), lambda i: (i, 0)),
            pl.BlockSpec((dm, d), const, pipeline_mode=pl.Buffered(1)),
            pl.BlockSpec((dm, d), const, pipeline_mode=pl.Buffered(1)),
            pl.BlockSpec((d, d), const, pipeline_mode=pl.Buffered(1)),
            pl.BlockSpec((1, d), const),
        ],
        out_specs=[pl.BlockSpec((tm, d), lambda i: (i, 0)), pl.BlockSpec((tm, d), lambda i: (i, 0))],
        out_shape=[jax.ShapeDtypeStruct((m, d), F32), jax.ShapeDtypeStruct((m, d), BF16)],
        compiler_params=_params(("parallel",)),
        name="merge",
    )(hmg, ohg, proj, proj, x2d, w_a, w_b, w_o, g_ffn)


def _ffn_kernel(x1_ref, h2_ref, w1_ref, w2_ref, gfin_ref, y_ref, acc_scr):
    j = pl.program_id(1)

    h = h2_ref[...]
    tf = w1_ref.shape[1]
    pieces = [slice(p, p + FFN_PIECE) for p in range(0, tf, FFN_PIECE)]
    ts = []
    for ps in pieces:
        t = jnp.maximum(jnp.dot(h, w1_ref[:, ps], preferred_element_type=F32), 0.0)
        ts.append((t * t).astype(BF16))
    acc = jnp.where(j > 0, acc_scr[...], 0.0)
    for ps, t in zip(pieces, ts):
        acc = acc + jnp.dot(t, w2_ref[ps, :], preferred_element_type=F32)
    acc_scr[...] = acc

    @pl.when(j == pl.num_programs(1) - 1)
    def _():
        y_ref[...] = _rms(x1_ref[...] + acc_scr[...], gfin_ref[...])


def _ffn(x1, h2, w1, w2, g_final):
    m, d = x1.shape
    dff = w1.shape[1]
    tm = min(m, FFN_ROWS)
    tf = FFN_HIDDEN
    return pl.pallas_call(
        _ffn_kernel,
        grid=(m // tm, dff // tf),
        in_specs=[
            pl.BlockSpec((tm, d), lambda i, j: (i, 0)),
            pl.BlockSpec((tm, d), lambda i, j: (i, 0)),
            pl.BlockSpec((d, tf), lambda i, j: (0, j)),
            pl.BlockSpec((tf, d), lambda i, j: (j, 0)),
            pl.BlockSpec((1, d), lambda i, j: (0, 0)),
        ],
        out_specs=pl.BlockSpec((tm, d), lambda i, j: (i, 0)),
        out_shape=jax.ShapeDtypeStruct((m, d), F32),
        scratch_shapes=[pltpu.VMEM((tm, d), F32)],
        compiler_params=_params(("parallel", "arbitrary")),
        name="ffn",
    )(x1, h2, w1, w2, g_final)


def _group(x, conv_prev, c0, n0, m0, s0, wts):
    B, T, d = x.shape
    x2d = x.reshape(B * T, d)
    proj_lo, proj_hi, gates = _inproj(x2d, wts["g_mix"], wts["w_all"], wts["n_lo"], wts["w_hi"], wts["w_if"], wts["b_if"])
    hmg, new_conv, c1, n1, m1 = _mlstm(proj_lo, gates, conv_prev, c0, n0, m0.reshape(B, 1, -1),
                                       wts["w_conv"], wts["b_conv"], wts["g_mnorm"], B=B, T=T)
    ohg, s1 = _hgrn(proj_hi, wts["lb_logits"], s0, wts["g_hnorm"], B=B, T=T)
    x1, h2 = _merge(hmg, ohg, proj_hi, x2d, wts["w_a"], wts["w_b"], wts["w_o"], wts["g_ffn"])
    y = _ffn(x1, h2, wts["w_ff1"], wts["w_ff2"], wts["g_final"])
    return (y.reshape(B, T, d), new_conv[None], c1[None], n1[None], m1.reshape(1, B, -1), s1[None])


def kernel(x_prompt, x_sample, cache_mlstm_conv, state_mlstm_C, state_mlstm_n, state_mlstm_m, state_hgrn_S,
           g_mix, w_in, b_if, w_conv, b_conv, g_mnorm, g_hnorm, hgrn_lb_logits, w_branch_a, w_branch_b,
           w_out, g_ffn, w_ff1, w_ff2, g_final):
    depth = w_in.shape[0]
    assert depth == 1, "single-layer trunk only"
    d = x_prompt.shape[-1]
    dm = w_branch_a.shape[1]
    dhh = w_branch_b.shape[1]
    nh_h = dhh // DH_H
    dh_m = dm // H_M
    assert hgrn_lb_logits.shape[0] == depth + 1
    assert 2 * dm == d and dhh == dm

    w = w_in[0]
    n_if = 2 * H_M
    p_if = 4 * dm
    w_all = w.astype(BF16)
    w_hi = w_all[:, p_if + n_if:]
    w_if = jnp.pad(w_all[:, p_if:p_if + n_if], ((0, 0), (0, LANES - n_if)))
    bif = jnp.pad(b_if[0].astype(F32), (0, LANES - n_if)).reshape(1, LANES)
    wts = dict(
        g_mix=g_mix[0].reshape(1, d), w_all=w_all, n_lo=p_if, w_hi=w_hi, w_if=w_if, b_if=bif,
        w_conv=w_conv[0], b_conv=b_conv[0].reshape(1, -1), g_mnorm=g_mnorm[0].reshape(1, dm),
        lb_logits=hgrn_lb_logits.astype(F32), g_hnorm=g_hnorm[0].reshape(1, dhh),
        w_a=w_branch_a[0].astype(BF16), w_b=w_branch_b[0].astype(BF16), w_o=w_out[0].astype(BF16),
        g_ffn=g_ffn[0].reshape(1, d), w_ff1=w_ff1[0].astype(BF16), w_ff2=w_ff2[0].astype(BF16),
        g_final=g_final.reshape(1, d),
    )

    bp = x_prompt.shape[0]
    zeros = lambda *s: jnp.zeros(s, F32)
    outs_p = _group(x_prompt, zeros(bp, CONV_W - 1, 2 * dm), zeros(bp, H_M, dh_m, dh_m), zeros(bp, H_M, dh_m),
                    zeros(bp, H_M), zeros(bp, nh_h, DH_H, DH_H), wts)
    outs_s = _group(x_sample, cache_mlstm_conv[0], state_mlstm_C[0].astype(F32), state_mlstm_n[0].astype(F32),
                    state_mlstm_m[0].astype(F32), state_hgrn_S[0].astype(F32), wts)
    return (outs_p[0], outs_s[0]) + outs_p[1:] + outs_s[1:]
```

```python
import functools

import jax
import jax.numpy as jnp
from jax import lax
from jax.experimental import pallas as pl
from jax.experimental.pallas import tpu as pltpu

F32 = jnp.float32
BF16 = jnp.bfloat16
HIGHEST = lax.Precision.HIGHEST

EPS = 1e-6
CHUNK = 64
MLSTM_CHUNK = 128
MLSTM_CPS = 2
CONV_W = 4
H_M = 4
DH_H = 128
SUB = 32
HGRN_CPS = 8
SAFE_SPAN = 60.0
LANES = 128
CONV_PAD = 8
FFN_PIECE = 256
VMEM_LIMIT = 56 * 1024 * 1024

MATMUL_ROWS, MATMUL_COLS = 1024, 2048
NORM_ROWS = 1024
MERGE_ROWS = 256
FFN_ROWS, FFN_HIDDEN = 512, 1024

NT_DIMS = (((1,), (1,)), ((), ()))
TN_DIMS = (((0,), (0,)), ((), ()))


def _rms(x, g):
    return x * lax.rsqrt(jnp.mean(x * x, axis=-1, keepdims=True) + EPS) * g


def _sigmoid(x):
    return 0.5 * jnp.tanh(0.5 * x) + 0.5


def _params(sem):
    return pltpu.CompilerParams(dimension_semantics=sem, vmem_limit_bytes=VMEM_LIMIT)


def _norm_kernel(x_ref, g_ref, wif_ref, bif_ref, h_ref, gates_ref):
    hb = _rms(x_ref[...], g_ref[...]).astype(BF16)
    h_ref[...] = hb
    gates_ref[...] = jnp.dot(hb, wif_ref[...], preferred_element_type=F32) + bif_ref[...]


def _matmul_kernel(h_ref, w_ref, o_ref):
    o_ref[...] = jnp.dot(h_ref[...], w_ref[...], preferred_element_type=F32)


def _matmul(h, w, n, name):
    m, d = h.shape
    tm = min(m, MATMUL_ROWS)
    tn = MATMUL_COLS
    return pl.pallas_call(
        _matmul_kernel,
        grid=(m // tm, n // tn),
        in_specs=[
            pl.BlockSpec((tm, d), lambda i, j: (i, 0)),
            pl.BlockSpec((d, tn), lambda i, j: (0, j)),
        ],
        out_specs=pl.BlockSpec((tm, tn), lambda i, j: (i, j)),
        out_shape=jax.ShapeDtypeStruct((m, n), F32),
        compiler_params=_params(("parallel", "arbitrary")),
        name=name,
    )(h, w)


def _inproj(x2d, g_mix, w_all, n_lo, w_hi, w_if, b_if):
    m, d = x2d.shape
    tm = min(m, NORM_ROWS)
    h, gates = pl.pallas_call(
        _norm_kernel,
        grid=(m // tm,),
        in_specs=[
            pl.BlockSpec((tm, d), lambda i: (i, 0)),
            pl.BlockSpec((1, d), lambda i: (0, 0)),
            pl.BlockSpec((d, LANES), lambda i: (0, 0)),
            pl.BlockSpec((1, LANES), lambda i: (0, 0)),
        ],
        out_specs=[pl.BlockSpec((tm, d), lambda i: (i, 0)), pl.BlockSpec((tm, LANES), lambda i: (i, 0))],
        out_shape=[jax.ShapeDtypeStruct((m, d), BF16), jax.ShapeDtypeStruct((m, LANES), F32)],
        compiler_params=_params(("parallel",)),
        name="norm",
    )(x2d, g_mix, w_if, b_if)
    return _matmul(h, w_all, n_lo, "inproj_lo"), _matmul(h, w_hi, w_hi.shape[1], "inproj_hi"), gates


def _mlstm_kernel(qk_ref, v_ref, o_ref, gates_ref, convprev_ref, c0_ref, n0_ref, m0_ref,
                  wconv_ref, bconv_ref, gnorm_ref,
                  hmg_ref, newconv_ref, c1_ref, n1_ref, m1_ref,
                  xpad, qk_scr, c_scr, n_scr, m_scr, *, L, dm, dh, cps):
    c = pl.program_id(1)
    nc = pl.num_programs(1)
    hist0 = CONV_PAD - (CONV_W - 1)
    groups = [slice(g * LANES, (g + 1) * LANES) for g in range(2 * dm // LANES)]

    @pl.when(c == 0)
    def _():
        for g, gl in enumerate(groups):
            xpad[g, hist0:CONV_PAD, :] = convprev_ref[:, gl]
        c_scr[...] = c0_ref[...]
        n_scr[...] = n0_ref[...]
        m_scr[...] = m0_ref[...]

    rr = lax.broadcasted_iota(jnp.int32, (L, L), 0)
    cc = lax.broadcasted_iota(jnp.int32, (L, L), 1)
    causal = rr >= cc
    tril = causal.astype(F32)
    eye = (lax.broadcasted_iota(jnp.int32, (8, LANES), 0)
           == lax.broadcasted_iota(jnp.int32, (8, LANES), 1)).astype(F32)
    heads = [slice(h * dh, (h + 1) * dh) for h in range(H_M)]
    k_scale = dh ** -0.5

    pending = []
    for j in range(cps):
        cr = slice(j * L, (j + 1) * L)
        x0 = CONV_PAD + j * L
        ns = L // 8
        for g, gl in enumerate(groups):
            xpad[g, x0:x0 + L, :] = qk_ref[cr, gl]
            slabs = {s: xpad[g, pl.ds(x0 + s, ns, stride=8), :] for s in range(-(CONV_W - 1), 8)}
            for r in range(8):
                conv = bconv_ref[:, gl]
                for i in range(CONV_W):
                    conv = conv + slabs[r - (CONV_W - 1) + i] * wconv_ref[i:i + 1, gl]
                qk_scr[g, pl.ds(r, ns, stride=8), :] = conv * _sigmoid(conv)
        qk = jnp.concatenate([qk_scr[g] for g in range(len(groups))], axis=1)
        q = qk[:, :dm]
        k = qk[:, dm:]
        qb = q.astype(BF16)
        kb = k.astype(BF16)
        vv = v_ref[cr, :]
        vb = vv.astype(BF16)

        g = gates_ref[cr, :]
        col = lax.broadcasted_iota(jnp.int32, g.shape, 1)
        lf = jnp.minimum(g, 0.0) - jnp.log1p(jnp.exp(-jnp.abs(g)))
        x_cols = jnp.where(col < H_M, g, lf)
        cum = jnp.dot(tril, x_cols, preferred_element_type=F32, precision=HIGHEST)
        xc = jnp.where(col < H_M, g, cum)
        rows = lax.dot_general(eye, xc, NT_DIMS, preferred_element_type=F32, precision=HIGHEST)

        m_all = m_scr[...]
        e_s, sc_inter, inv_floor, decay, wgt, m_new = [], [], [], [], [], []
        for h in range(H_M):
            bcol = cum[:, H_M + h:H_M + h + 1]
            igcol = g[:, h:h + 1]
            brow = rows[H_M + h:H_M + h + 1, :]
            igrow = rows[h:h + 1, :]
            m_prev = m_all[:, h:h + 1]
            dlog = jnp.where(causal, bcol - brow + igrow, -jnp.inf)
            m_inter = bcol + m_prev
            m_t = jnp.maximum(m_inter, jnp.max(dlog, axis=-1, keepdims=True))
            e_s.append(jnp.exp(dlog - m_t) * k_scale)
            sc_inter.append(jnp.exp(m_inter - m_t))
            inv_floor.append(jnp.exp(-m_t))
            mn = m_t[L - 1:L, :]
            b_last = bcol[L - 1:L, :]
            m_new.append(mn)
            decay.append(jnp.exp(b_last + m_prev - mn))
            wgt.append(jnp.exp(b_last - bcol + igcol - mn) * k_scale)

        qn = []
        for h, sl in enumerate(heads):
            n_old = n_scr[h:h + 1, :]
            qn.append(jnp.sum(q[:, sl] * n_old, axis=-1, keepdims=True))
            n_scr[h:h + 1, :] = decay[h] * n_old + jnp.sum(wgt[h] * k[:, sl], axis=0, keepdims=True)
            m_scr[:, h:h + 1] = m_new[h]

        c_old = [c_scr[h] for h in range(H_M)]
        s_raw = [lax.dot_general(qb[:, sl], kb[:, sl], NT_DIMS, preferred_element_type=F32) for sl in heads]
        qc = [lax.dot_general(qb[:, sl], c_old[h].astype(BF16), NT_DIMS, preferred_element_type=F32)
              for h, sl in enumerate(heads)]
        upd = [lax.dot_general((vv[:, sl] * wgt[h]).astype(BF16), kb[:, sl], TN_DIMS, preferred_element_type=F32)
               for h, sl in enumerate(heads)]
        for h in range(H_M):
            c_scr[h] = decay[h] * c_old[h] + upd[h]
        pending.append((cr, s_raw, e_s, sc_inter, inv_floor, qc, qn, vb))

    for cr, s_raw, e_s, sc_inter, inv_floor, qc, qn, vb in pending:
        s = [s_raw[h] * e_s[h] for h in range(H_M)]
        pv = [jnp.dot(s[h].astype(BF16), vb[:, sl], preferred_element_type=F32) for h, sl in enumerate(heads)]
        oo = o_ref[cr, :]
        for h, sl in enumerate(heads):
            num = sc_inter[h] * qc[h] + pv[h]
            den = sc_inter[h] * qn[h] + jnp.sum(s[h], axis=-1, keepdims=True)
            hout = num / jnp.maximum(jnp.abs(den), inv_floor[h])
            y = _rms(hout, gnorm_ref[:, sl]) * _sigmoid(oo[:, sl])
            hmg_ref[cr, sl] = y.astype(BF16)

    x_end = CONV_PAD + cps * L
    hist_g = [xpad[g, x_end - (CONV_W - 1):x_end, :] for g in range(len(groups))]
    for g in range(len(groups)):
        xpad[g, hist0:CONV_PAD, :] = hist_g[g]
    hist = jnp.concatenate(hist_g, axis=1)

    @pl.when(c == nc - 1)
    def _():
        newconv_ref[...] = hist
        c1_ref[...] = c_scr[...]
        n1_ref[...] = n_scr[...]
        m1_ref[...] = m_scr[...]


def _mlstm(proj, gates, conv_prev, c0, n0, m0, w_conv, b_conv, g_mnorm, *, B, T):
    L = MLSTM_CHUNK if T % MLSTM_CHUNK == 0 else min(CHUNK, T)
    nc = T // L
    dm = c0.shape[1] * c0.shape[2]
    dh = c0.shape[2]
    cps = MLSTM_CPS if nc % MLSTM_CPS == 0 else 1
    ns = nc // cps
    row = lambda b, c: b * ns + c
    kern = functools.partial(_mlstm_kernel, L=L, dm=dm, dh=dh, cps=cps)
    return pl.pallas_call(
        kern,
        grid=(B, ns),
        in_specs=[
            pl.BlockSpec((cps * L, 2 * dm), lambda b, c: (row(b, c), 0)),
            pl.BlockSpec((cps * L, dm), lambda b, c: (row(b, c), 2)),
            pl.BlockSpec((cps * L, dm), lambda b, c: (row(b, c), 3)),
            pl.BlockSpec((cps * L, LANES), lambda b, c: (row(b, c), 0)),
            pl.BlockSpec((None, CONV_W - 1, 2 * dm), lambda b, c: (b, 0, 0)),
            pl.BlockSpec((None, H_M, dh, dh), lambda b, c: (b, 0, 0, 0)),
            pl.BlockSpec((None, H_M, dh), lambda b, c: (b, 0, 0)),
            pl.BlockSpec((None, 1, H_M), lambda b, c: (b, 0, 0)),
            pl.BlockSpec((CONV_W, 2 * dm), lambda b, c: (0, 0)),
            pl.BlockSpec((1, 2 * dm), lambda b, c: (0, 0)),
            pl.BlockSpec((1, dm), lambda b, c: (0, 0)),
        ],
        out_specs=[
            pl.BlockSpec((cps * L, dm), lambda b, c: (row(b, c), 0)),
            pl.BlockSpec((None, CONV_W - 1, 2 * dm), lambda b, c: (b, 0, 0)),
            pl.BlockSpec((None, H_M, dh, dh), lambda b, c: (b, 0, 0, 0)),
            pl.BlockSpec((None, H_M, dh), lambda b, c: (b, 0, 0)),
            pl.BlockSpec((None, 1, H_M), lambda b, c: (b, 0, 0)),
        ],
        out_shape=[
            jax.ShapeDtypeStruct((B * T, dm), BF16),
            jax.ShapeDtypeStruct((B, CONV_W - 1, 2 * dm), F32),
            jax.ShapeDtypeStruct((B, H_M, dh, dh), F32),
            jax.ShapeDtypeStruct((B, H_M, dh), F32),
            jax.ShapeDtypeStruct((B, 1, H_M), F32),
        ],
        scratch_shapes=[
            pltpu.VMEM((2 * dm // LANES, CONV_PAD + cps * L, LANES), F32),
            pltpu.VMEM((2 * dm // LANES, L, LANES), F32),
            pltpu.VMEM((H_M, dh, dh), F32),
            pltpu.VMEM((H_M, dh), F32),
            pltpu.VMEM((1, H_M), F32),
        ],
        compiler_params=_params(("parallel", "arbitrary")),
        name="mlstm",
    )(proj, proj, proj, gates, conv_prev, c0, n0, m0, w_conv, b_conv, g_mnorm)


def _hgrn_factors(qq, kk, a, qa, sub, nb):
    L = a.shape[0]
    refs = [jnp.zeros_like(a[0:1, :])] + [a[i * sub - 1:i * sub, :] for i in range(1, nb)]
    qs = []
    ks = []
    for i in range(nb):
        rows = slice(i * sub, (i + 1) * sub)
        nk = (i + 1) * sub
        q_i = qa[rows, :] if i == 0 else (qq[rows, :] * jnp.exp(a[rows, :] - refs[i])).astype(BF16)
        k_i = (kk[:nk, :] * jnp.exp(jnp.minimum(refs[i] - a[:nk, :], SAFE_SPAN))).astype(BF16)
        zq = lambda n: jnp.zeros((n, q_i.shape[1]), BF16)
        qs.append(jnp.concatenate([z for z in (zq(i * sub), q_i, zq(L - nk)) if z.shape[0]], axis=0))
        ks.append(jnp.concatenate([z for z in (k_i, zq(L - nk)) if z.shape[0]], axis=0))
    return refs, qs, ks


def _hgrn_scores(qs, ks, sl, L):
    lhs = jnp.concatenate([q[:, sl] for q in qs], axis=1) if len(qs) > 1 else qs[0][:, sl]
    rhs = jnp.concatenate([k[:, sl] for k in ks], axis=1) if len(ks) > 1 else ks[0][:, sl]
    return lax.dot_general(lhs, rhs, NT_DIMS, preferred_element_type=F32)


def _hgrn_kernel(f_ref, i_ref, q_ref, g_ref, lbl_ref, s0_ref, gnorm_ref,
                 ohg_ref, s1_ref,
                 st_scr, a_scr, k_scr, sc_scr, o_scr, in_scr, *, L, nh, cps):
    c = pl.program_id(1)
    nc = pl.num_programs(1)
    sub = min(SUB, L)
    nb = L // sub

    @pl.when(c == 0)
    def _():
        for h in range(nh):
            st_scr[h] = s0_ref[h].T

    lbl = lbl_ref[...]
    mx = jnp.max(lbl, axis=0, keepdims=True)
    ex = jnp.exp(lbl - mx)
    lb = ex[0:1, :] / jnp.sum(ex, axis=0, keepdims=True)
    gnorm = gnorm_ref[...]
    rr = lax.broadcasted_iota(jnp.int32, (L, L), 0)
    cc = lax.broadcasted_iota(jnp.int32, (L, L), 1)
    causal = cc <= rr
    tril = causal.astype(BF16)
    tril3 = jnp.concatenate([tril, tril, tril], axis=1)

    def finish(rows):
        gg = g_ref[rows, :]
        ohg_ref[rows, :] = (_rms(o_scr[rows, :], gnorm) * (gg * _sigmoid(gg))).astype(BF16)

    span = None
    for j in range(cps):
        cr = slice(j * L, (j + 1) * L)
        f = lb + (1.0 - lb) * _sigmoid(f_ref[cr, :])
        lf = jnp.log(f)
        kk = 1.0 - f
        qq = q_ref[cr, :]
        vb = i_ref[cr, :].astype(BF16)

        hi = lf.astype(BF16)
        r1 = lf - hi.astype(F32)
        mid = r1.astype(BF16)
        lo = (r1 - mid.astype(F32)).astype(BF16)
        a = jnp.dot(tril3, jnp.concatenate([hi, mid, lo], axis=0), preferred_element_type=F32)
        a_scr[cr, :] = a
        k_scr[cr, :] = kk

        a_last = a[L - 1:L, :]
        qa = (qq * jnp.exp(a)).astype(BF16)
        kdec = (kk * jnp.exp(a_last - a)).astype(BF16)
        dec_last = jnp.exp(a_last)
        refs, qs, ks = _hgrn_factors(qq, kk, a, qa, sub, nb)
        for i in range(nb):
            sp = refs[i] - a[(i + 1) * sub - 1:(i + 1) * sub, :]
            span = sp if span is None else jnp.maximum(span, sp)

        heads = [slice(h * DH_H, (h + 1) * DH_H) for h in range(nh)]
        st_old = [st_scr[h] for h in range(nh)]
        inter = [lax.dot_general(qa[:, sl], st_old[h].astype(BF16), NT_DIMS, preferred_element_type=F32)
                 for h, sl in enumerate(heads)]
        raw = [_hgrn_scores(qs, ks, sl, L) for sl in heads]
        upd = [lax.dot_general(vb[:, sl], kdec[:, sl], TN_DIMS, preferred_element_type=F32) for sl in heads]
        for h, sl in enumerate(heads):
            st_scr[h] = st_old[h] * dec_last[:, sl] + upd[h]
            in_scr[cr, sl] = inter[h]
        blk = [jnp.where(causal, r, 0.0).astype(BF16) for r in raw]
        for h, sl in enumerate(heads):
            o_scr[cr, sl] = inter[h] + jnp.dot(blk[h], vb[:, sl], preferred_element_type=F32)
        finish(cr)

    use_exact = jnp.logical_not(jnp.max(span) <= SAFE_SPAN)

    @pl.when(use_exact)
    def _():
        sc_scr[...] = jnp.zeros_like(sc_scr)
        lane = lax.broadcasted_iota(jnp.int32, (sub, LANES), 1)
        in_block = jnp.logical_and(causal, (cc // sub) == (rr // sub))

        def fix_chunk(j, carry):
            base = pl.multiple_of(j * L, L)

            def diag_body(s, carry2):
                r0 = pl.multiple_of((s // sub) * sub, sub)
                a_blk = a_scr[pl.ds(base + r0, sub), :]
                q_blk = q_ref[pl.ds(base + r0, sub), :]
                a_s = a_scr[pl.ds(base + s, 1), :]
                k_s = k_scr[pl.ds(base + s, 1), :]
                p = q_blk * jnp.exp(jnp.minimum(a_blk - a_s, 0.0)) * k_s
                for h in range(nh):
                    colv = jnp.sum(p[:, h * DH_H:(h + 1) * DH_H], axis=-1, keepdims=True)
                    old = sc_scr[h, pl.ds(r0, sub), :]
                    sc_scr[h, pl.ds(r0, sub), :] = jnp.where(lane == s, colv, old)
                return carry2

            lax.fori_loop(0, L, diag_body, 0)

            cr = pl.ds(base, L)
            a = a_scr[cr, :]
            kk = k_scr[cr, :]
            qq = q_ref[cr, :]
            vb = i_ref[cr, :].astype(BF16)
            qa = (qq * jnp.exp(a)).astype(BF16)
            _, qs, ks = _hgrn_factors(qq, kk, a, qa, sub, nb)
            for h in range(nh):
                sl = slice(h * DH_H, (h + 1) * DH_H)
                blk = jnp.where(causal, _hgrn_scores(qs, ks, sl, L), 0.0)
                blk = jnp.where(in_block, sc_scr[h][:, :L], blk).astype(BF16)
                o_scr[cr, sl] = in_scr[cr, sl] + jnp.dot(blk, vb[:, sl], preferred_element_type=F32)
            finish(cr)
            return carry

        lax.fori_loop(0, cps, fix_chunk, 0)

    @pl.when(c == nc - 1)
    def _():
        for h in range(nh):
            s1_ref[h] = st_scr[h].T


def _hgrn(proj, lb_logits, s0, g_hnorm, *, B, T):
    L = min(CHUNK, T)
    nc = T // L
    cps = HGRN_CPS if nc % HGRN_CPS == 0 else 1
    ns = nc // cps
    nh = s0.shape[1]
    dhh = nh * DH_H
    row = lambda b, c: b * ns + c
    kern = functools.partial(_hgrn_kernel, L=L, nh=nh, cps=cps)
    return pl.pallas_call(
        kern,
        grid=(B, ns),
        in_specs=[
            pl.BlockSpec((cps * L, dhh), lambda b, c: (row(b, c), 0)),
            pl.BlockSpec((cps * L, dhh), lambda b, c: (row(b, c), 1)),
            pl.BlockSpec((cps * L, dhh), lambda b, c: (row(b, c), 2)),
            pl.BlockSpec((cps * L, dhh), lambda b, c: (row(b, c), 3)),
            pl.BlockSpec((2, dhh), lambda b, c: (0, 0)),
            pl.BlockSpec((None, nh, DH_H, DH_H), lambda b, c: (b, 0, 0, 0)),
            pl.BlockSpec((1, dhh), lambda b, c: (0, 0)),
        ],
        out_specs=[
            pl.BlockSpec((cps * L, dhh), lambda b, c: (row(b, c), 0)),
            pl.BlockSpec((None, nh, DH_H, DH_H), lambda b, c: (b, 0, 0, 0)),
        ],
        out_shape=[
            jax.ShapeDtypeStruct((B * T, dhh), BF16),
            jax.ShapeDtypeStruct((B, nh, DH_H, DH_H), F32),
        ],
        scratch_shapes=[
            pltpu.VMEM((nh, DH_H, DH_H), F32),
            pltpu.VMEM((cps * L, dhh), F32),
            pltpu.VMEM((cps * L, dhh), F32),
            pltpu.VMEM((nh, L, LANES), F32),
            pltpu.VMEM((cps * L, dhh), F32),
            pltpu.VMEM((cps * L, dhh), F32),
        ],
        compiler_params=_params(("parallel", "arbitrary")),
        name="hgrn",
    )(proj, proj, proj, proj, lb_logits, s0, g_hnorm)


def _merge_kernel(hmg_ref, ohg_ref, ga_ref, gb_ref, x_ref, wa_ref, wb_ref, wo_ref, gffn_ref, x1_ref, h2_ref):
    ya = jnp.dot(hmg_ref[...], wa_ref[...], preferred_element_type=F32)
    yb = jnp.dot(ohg_ref[...], wb_ref[...], preferred_element_type=F32)
    u = _sigmoid(ga_ref[...]) * ya + _sigmoid(gb_ref[...]) * yb
    x1 = x_ref[...] + jnp.dot(u.astype(BF16), wo_ref[...], preferred_element_type=F32)
    x1_ref[...] = x1
    h2_ref[...] = _rms(x1, gffn_ref[...]).astype(BF16)


def _merge(hmg, ohg, proj, x2d, w_a, w_b, w_o, g_ffn):
    m, d = x2d.shape
    dm = hmg.shape[1]
    tm = min(m, MERGE_ROWS)
    const = lambda i: (0, 0)
    return pl.pallas_call(
        _merge_kernel,
        grid=(m // tm,),
        in_specs=[
            pl.BlockSpec((tm, dm), lambda i: (i, 0)),
            pl.BlockSpec((tm, dm), lambda i: (i, 0)),
            pl.BlockSpec((tm, d), lambda i: (i, 2)),
            pl.BlockSpec((tm, d), lambda i: (i, 3)),
            pl.BlockSpec((tm, d), lambda i: (i, 0)),
            pl.BlockSpec((dm, d), const, pipeline_mode=pl.Buffered(1)),
            pl.BlockSpec((dm, d), const, pipeline_mode=pl.Buffered(1)),
            pl.BlockSpec((d, d), const, pipeline_mode=pl.Buffered(1)),
            pl.BlockSpec((1, d), const),
        ],
        out_specs=[pl.BlockSpec((tm, d), lambda i: (i, 0)), pl.BlockSpec((tm, d), lambda i: (i, 0))],
        out_shape=[jax.ShapeDtypeStruct((m, d), F32), jax.ShapeDtypeStruct((m, d), BF16)],
        compiler_params=_params(("parallel",)),
        name="merge",
    )(hmg, ohg, proj, proj, x2d, w_a, w_b, w_o, g_ffn)


def _ffn_kernel(x1_ref, h2_ref, w1_ref, w2_ref, gfin_ref, y_ref, acc_scr):
    j = pl.program_id(1)

    h = h2_ref[...]
    tf = w1_ref.shape[1]
    pieces = [slice(p, p + FFN_PIECE) for p in range(0, tf, FFN_PIECE)]
    ts = []
    for ps in pieces:
        t = jnp.maximum(jnp.dot(h, w1_ref[:, ps], preferred_element_type=F32), 0.0)
        ts.append((t * t).astype(BF16))
    acc = jnp.where(j > 0, acc_scr[...], 0.0)
    for ps, t in zip(pieces, ts):
        acc = acc + jnp.dot(t, w2_ref[ps, :], preferred_element_type=F32)
    acc_scr[...] = acc

    @pl.when(j == pl.num_programs(1) - 1)
    def _():
        y_ref[...] = _rms(x1_ref[...] + acc_scr[...], gfin_ref[...])


def _ffn(x1, h2, w1, w2, g_final):
    m, d = x1.shape
    dff = w1.shape[1]
    tm = min(m, FFN_ROWS)
    tf = FFN_HIDDEN
    return pl.pallas_call(
        _ffn_kernel,
        grid=(m // tm, dff // tf),
        in_specs=[
            pl.BlockSpec((tm, d), lambda i, j: (i, 0)),
            pl.BlockSpec((tm, d), lambda i, j: (i, 0)),
            pl.BlockSpec((d, tf), lambda i, j: (0, j)),
            pl.BlockSpec((tf, d), lambda i, j: (j, 0)),
            pl.BlockSpec((1, d), lambda i, j: (0, 0)),
        ],
        out_specs=pl.BlockSpec((tm, d), lambda i, j: (i, 0)),
        out_shape=jax.ShapeDtypeStruct((m, d), F32),
        scratch_shapes=[pltpu.VMEM((tm, d), F32)],
        compiler_params=_params(("parallel", "arbitrary")),
        name="ffn",
    )(x1, h2, w1, w2, g_final)


def _group(x, conv_prev, c0, n0, m0, s0, wts):
    B, T, d = x.shape
    x2d = x.reshape(B * T, d)
    proj_lo, proj_hi, gates = _inproj(x2d, wts["g_mix"], wts["w_all"], wts["n_lo"], wts["w_hi"], wts["w_if"], wts["b_if"])
    hmg, new_conv, c1, n1, m1 = _mlstm(proj_lo, gates, conv_prev, c0, n0, m0.reshape(B, 1, -1),
                                       wts["w_conv"], wts["b_conv"], wts["g_mnorm"], B=B, T=T)
    ohg, s1 = _hgrn(proj_hi, wts["lb_logits"], s0, wts["g_hnorm"], B=B, T=T)
    x1, h2 = _merge(hmg, ohg, proj_hi, x2d, wts["w_a"], wts["w_b"], wts["w_o"], wts["g_ffn"])
    y = _ffn(x1, h2, wts["w_ff1"], wts["w_ff2"], wts["g_final"])
    return (y.reshape(B, T, d), new_conv[None], c1[None], n1[None], m1.reshape(1, B, -1), s1[None])


def kernel(x_prompt, x_sample, cache_mlstm_conv, state_mlstm_C, state_mlstm_n, state_mlstm_m, state_hgrn_S,
           g_mix, w_in, b_if, w_conv, b_conv, g_mnorm, g_hnorm, hgrn_lb_logits, w_branch_a, w_branch_b,
           w_out, g_ffn, w_ff1, w_ff2, g_final):
    depth = w_in.shape[0]
    assert depth == 1, "single-layer trunk only"
    d = x_prompt.shape[-1]
    dm = w_branch_a.shape[1]
    dhh = w_branch_b.shape[1]
    nh_h = dhh // DH_H
    dh_m = dm // H_M
    assert hgrn_lb_logits.shape[0] == depth + 1
    assert 2 * dm == d and dhh == dm

    w = w_in[0]
    n_if = 2 * H_M
    p_if = 4 * dm
    w_all = w.astype(BF16)
    w_hi = w_all[:, p_if + n_if:]
    w_if = jnp.pad(w_all[:, p_if:p_if + n_if], ((0, 0), (0, LANES - n_if)))
    bif = jnp.pad(b_if[0].astype(F32), (0, LANES - n_if)).reshape(1, LANES)
    wts = dict(
        g_mix=g_mix[0].reshape(1, d), w_all=w_all, n_lo=p_if, w_hi=w_hi, w_if=w_if, b_if=bif,
        w_conv=w_conv[0], b_conv=b_conv[0].reshape(1, -1), g_mnorm=g_mnorm[0].reshape(1, dm),
        lb_logits=hgrn_lb_logits.astype(F32), g_hnorm=g_hnorm[0].reshape(1, dhh),
        w_a=w_branch_a[0].astype(BF16), w_b=w_branch_b[0].astype(BF16), w_o=w_out[0].astype(BF16),
        g_ffn=g_ffn[0].reshape(1, d), w_ff1=w_ff1[0].astype(BF16), w_ff2=w_ff2[0].astype(BF16),
        g_final=g_final.reshape(1, d),
    )

    bp = x_prompt.shape[0]
    zeros = lambda *s: jnp.zeros(s, F32)
    outs_p = _group(x_prompt, zeros(bp, CONV_W - 1, 2 * dm), zeros(bp, H_M, dh_m, dh_m), zeros(bp, H_M, dh_m),
                    zeros(bp, H_M), zeros(bp, nh_h, DH_H, DH_H), wts)
    outs_s = _group(x_sample, cache_mlstm_conv[0], state_mlstm_C[0].astype(F32), state_mlstm_n[0].astype(F32),
                    state_mlstm_m[0].astype(F32), state_hgrn_S[0].astype(F32), wts)
    return (outs_p[0], outs_s[0]) + outs_p[1:] + outs_s[1:]
```

```python
import functools

import jax
import jax.numpy as jnp
from jax import lax
from jax.experimental import pallas as pl
from jax.experimental.pallas import tpu as pltpu

F32 = jnp.float32
BF16 = jnp.bfloat16
HIGHEST = lax.Precision.HIGHEST

EPS = 1e-6
CHUNK = 64
MLSTM_CHUNK = 256
MLSTM_CPS = 2
CONV_W = 4
H_M = 4
DH_H = 128
SUB = 32
HGRN_CPS = 8
SAFE_SPAN = 60.0
LANES = 128
CONV_PAD = 8
FFN_PIECE = 256
VMEM_LIMIT = 56 * 1024 * 1024

MATMUL_ROWS, MATMUL_COLS = 1024, 2048
NORM_ROWS = 1024
MERGE_ROWS = 256
FFN_ROWS, FFN_HIDDEN = 512, 1024

NT_DIMS = (((1,), (1,)), ((), ()))
TN_DIMS = (((0,), (0,)), ((), ()))


def _rms(x, g):
    return x * lax.rsqrt(jnp.mean(x * x, axis=-1, keepdims=True) + EPS) * g


def _sigmoid(x):
    return 0.5 * jnp.tanh(0.5 * x) + 0.5


def _params(sem):
    return pltpu.CompilerParams(dimension_semantics=sem, vmem_limit_bytes=VMEM_LIMIT)


def _norm_kernel(x_ref, g_ref, wif_ref, bif_ref, h_ref, gates_ref):
    hb = _rms(x_ref[...], g_ref[...]).astype(BF16)
    h_ref[...] = hb
    gates_ref[...] = jnp.dot(hb, wif_ref[...], preferred_element_type=F32) + bif_ref[...]


def _matmul_kernel(h_ref, w_ref, o_ref):
    o_ref[...] = jnp.dot(h_ref[...], w_ref[...], preferred_element_type=F32)


def _matmul(h, w, n, name):
    m, d = h.shape
    tm = min(m, MATMUL_ROWS)
    tn = MATMUL_COLS
    return pl.pallas_call(
        _matmul_kernel,
        grid=(m // tm, n // tn),
        in_specs=[
            pl.BlockSpec((tm, d), lambda i, j: (i, 0)),
            pl.BlockSpec((d, tn), lambda i, j: (0, j)),
        ],
        out_specs=pl.BlockSpec((tm, tn), lambda i, j: (i, j)),
        out_shape=jax.ShapeDtypeStruct((m, n), F32),
        compiler_params=_params(("parallel", "arbitrary")),
        name=name,
    )(h, w)


def _inproj(x2d, g_mix, w_all, n_lo, w_hi, w_if, b_if):
    m, d = x2d.shape
    tm = min(m, NORM_ROWS)
    h, gates = pl.pallas_call(
        _norm_kernel,
        grid=(m // tm,),
        in_specs=[
            pl.BlockSpec((tm, d), lambda i: (i, 0)),
            pl.BlockSpec((1, d), lambda i: (0, 0)),
            pl.BlockSpec((d, LANES), lambda i: (0, 0)),
            pl.BlockSpec((1, LANES), lambda i: (0, 0)),
        ],
        out_specs=[pl.BlockSpec((tm, d), lambda i: (i, 0)), pl.BlockSpec((tm, LANES), lambda i: (i, 0))],
        out_shape=[jax.ShapeDtypeStruct((m, d), BF16), jax.ShapeDtypeStruct((m, LANES), F32)],
        compiler_params=_params(("parallel",)),
        name="norm",
    )(x2d, g_mix, w_if, b_if)
    return _matmul(h, w_all, n_lo, "inproj_lo"), _matmul(h, w_hi, w_hi.shape[1], "inproj_hi"), gates


def _mlstm_kernel(qk_ref, v_ref, o_ref, gates_ref, convprev_ref, c0_ref, n0_ref, m0_ref,
                  wconv_ref, bconv_ref, gnorm_ref,
                  hmg_ref, newconv_ref, c1_ref, n1_ref, m1_ref,
                  xpad, qk_scr, c_scr, n_scr, m_scr, *, L, dm, dh, cps):
    c = pl.program_id(1)
    nc = pl.num_programs(1)
    hist0 = CONV_PAD - (CONV_W - 1)
    groups = [slice(g * LANES, (g + 1) * LANES) for g in range(2 * dm // LANES)]

    @pl.when(c == 0)
    def _():
        for g, gl in enumerate(groups):
            xpad[g, hist0:CONV_PAD, :] = convprev_ref[:, gl]
        c_scr[...] = c0_ref[...]
        n_scr[...] = n0_ref[...]
        m_scr[...] = m0_ref[...]

    rr = lax.broadcasted_iota(jnp.int32, (L, L), 0)
    cc = lax.broadcasted_iota(jnp.int32, (L, L), 1)
    causal = rr >= cc
    tril = causal.astype(F32)
    eye = (lax.broadcasted_iota(jnp.int32, (8, LANES), 0)
           == lax.broadcasted_iota(jnp.int32, (8, LANES), 1)).astype(F32)
    heads = [slice(h * dh, (h + 1) * dh) for h in range(H_M)]
    k_scale = dh ** -0.5

    pending = []
    for j in range(cps):
        cr = slice(j * L, (j + 1) * L)
        x0 = CONV_PAD + j * L
        ns = L // 8
        for g, gl in enumerate(groups):
            xpad[g, x0:x0 + L, :] = qk_ref[cr, gl]
            slabs = {s: xpad[g, pl.ds(x0 + s, ns, stride=8), :] for s in range(-(CONV_W - 1), 8)}
            for r in range(8):
                conv = bconv_ref[:, gl]
                for i in range(CONV_W):
                    conv = conv + slabs[r - (CONV_W - 1) + i] * wconv_ref[i:i + 1, gl]
                qk_scr[g, pl.ds(r, ns, stride=8), :] = conv * _sigmoid(conv)
        qk = jnp.concatenate([qk_scr[g] for g in range(len(groups))], axis=1)
        q = qk[:, :dm]
        k = qk[:, dm:]
        qb = q.astype(BF16)
        kb = k.astype(BF16)
        vv = v_ref[cr, :]
        vb = vv.astype(BF16)

        g = gates_ref[cr, :]
        col = lax.broadcasted_iota(jnp.int32, g.shape, 1)
        lf = jnp.minimum(g, 0.0) - jnp.log1p(jnp.exp(-jnp.abs(g)))
        x_cols = jnp.where(col < H_M, g, lf)
        cum = jnp.dot(tril, x_cols, preferred_element_type=F32, precision=HIGHEST)
        xc = jnp.where(col < H_M, g, cum)
        rows = lax.dot_general(eye, xc, NT_DIMS, preferred_element_type=F32, precision=HIGHEST)

        m_all = m_scr[...]
        e_s, sc_inter, inv_floor, decay, wgt, m_new = [], [], [], [], [], []
        for h in range(H_M):
            bcol = cum[:, H_M + h:H_M + h + 1]
            igcol = g[:, h:h + 1]
            brow = rows[H_M + h:H_M + h + 1, :]
            igrow = rows[h:h + 1, :]
            m_prev = m_all[:, h:h + 1]
            dlog = jnp.where(causal, bcol - brow + igrow, -jnp.inf)
            m_inter = bcol + m_prev
            m_t = jnp.maximum(m_inter, jnp.max(dlog, axis=-1, keepdims=True))
            e_s.append(jnp.exp(dlog - m_t) * k_scale)
            sc_inter.append(jnp.exp(m_inter - m_t))
            inv_floor.append(jnp.exp(-m_t))
            mn = m_t[L - 1:L, :]
            b_last = bcol[L - 1:L, :]
            m_new.append(mn)
            decay.append(jnp.exp(b_last + m_prev - mn))
            wgt.append(jnp.exp(b_last - bcol + igcol - mn) * k_scale)

        qn = []
        for h, sl in enumerate(heads):
            n_old = n_scr[h:h + 1, :]
            qn.append(jnp.sum(q[:, sl] * n_old, axis=-1, keepdims=True))
            n_scr[h:h + 1, :] = decay[h] * n_old + jnp.sum(wgt[h] * k[:, sl], axis=0, keepdims=True)
            m_scr[:, h:h + 1] = m_new[h]

        c_old = [c_scr[h] for h in range(H_M)]
        s_raw = [lax.dot_general(qb[:, sl], kb[:, sl], NT_DIMS, preferred_element_type=F32) for sl in heads]
        qc = [lax.dot_general(qb[:, sl], c_old[h].astype(BF16), NT_DIMS, preferred_element_type=F32)
              for h, sl in enumerate(heads)]
        upd = [lax.dot_general((vv[:, sl] * wgt[h]).astype(BF16), kb[:, sl], TN_DIMS, preferred_element_type=F32)
               for h, sl in enumerate(heads)]
        for h in range(H_M):
            c_scr[h] = decay[h] * c_old[h] + upd[h]
        pending.append((cr, s_raw, e_s, sc_inter, inv_floor, qc, qn, vb))

    for cr, s_raw, e_s, sc_inter, inv_floor, qc, qn, vb in pending:
        s = [s_raw[h] * e_s[h] for h in range(H_M)]
        pv = [jnp.dot(s[h].astype(BF16), vb[:, sl], preferred_element_type=F32) for h, sl in enumerate(heads)]
        oo = o_ref[cr, :]
        for h, sl in enumerate(heads):
            num = sc_inter[h] * qc[h] + pv[h]
            den = sc_inter[h] * qn[h] + jnp.sum(s[h], axis=-1, keepdims=True)
            hout = num / jnp.maximum(jnp.abs(den), inv_floor[h])
            y = _rms(hout, gnorm_ref[:, sl]) * _sigmoid(oo[:, sl])
            hmg_ref[cr, sl] = y.astype(BF16)

    x_end = CONV_PAD + cps * L
    hist_g = [xpad[g, x_end - (CONV_W - 1):x_end, :] for g in range(len(groups))]
    for g in range(len(groups)):
        xpad[g, hist0:CONV_PAD, :] = hist_g[g]
    hist = jnp.concatenate(hist_g, axis=1)

    @pl.when(c == nc - 1)
    def _():
        newconv_ref[...] = hist
        c1_ref[...] = c_scr[...]
        n1_ref[...] = n_scr[...]
        m1_ref[...] = m_scr[...]


def _mlstm(proj, gates, conv_prev, c0, n0, m0, w_conv, b_conv, g_mnorm, *, B, T):
    L = MLSTM_CHUNK if T % MLSTM_CHUNK == 0 else min(CHUNK, T)
    nc = T // L
    dm = c0.shape[1] * c0.shape[2]
    dh = c0.shape[2]
    cps = MLSTM_CPS if nc % MLSTM_CPS == 0 else 1
    ns = nc // cps
    row = lambda b, c: b * ns + c
    kern = functools.partial(_mlstm_kernel, L=L, dm=dm, dh=dh, cps=cps)
    return pl.pallas_call(
        kern,
        grid=(B, ns),
        in_specs=[
            pl.BlockSpec((cps * L, 2 * dm), lambda b, c: (row(b, c), 0)),
            pl.BlockSpec((cps * L, dm), lambda b, c: (row(b, c), 2)),
            pl.BlockSpec((cps * L, dm), lambda b, c: (row(b, c), 3)),
            pl.BlockSpec((cps * L, LANES), lambda b, c: (row(b, c), 0)),
            pl.BlockSpec((None, CONV_W - 1, 2 * dm), lambda b, c: (b, 0, 0)),
            pl.BlockSpec((None, H_M, dh, dh), lambda b, c: (b, 0, 0, 0)),
            pl.BlockSpec((None, H_M, dh), lambda b, c: (b, 0, 0)),
            pl.BlockSpec((None, 1, H_M), lambda b, c: (b, 0, 0)),
            pl.BlockSpec((CONV_W, 2 * dm), lambda b, c: (0, 0)),
            pl.BlockSpec((1, 2 * dm), lambda b, c: (0, 0)),
            pl.BlockSpec((1, dm), lambda b, c: (0, 0)),
        ],
        out_specs=[
            pl.BlockSpec((cps * L, dm), lambda b, c: (row(b, c), 0)),
            pl.BlockSpec((None, CONV_W - 1, 2 * dm), lambda b, c: (b, 0, 0)),
            pl.BlockSpec((None, H_M, dh, dh), lambda b, c: (b, 0, 0, 0)),
            pl.BlockSpec((None, H_M, dh), lambda b, c: (b, 0, 0)),
            pl.BlockSpec((None, 1, H_M), lambda b, c: (b, 0, 0)),
        ],
        out_shape=[
            jax.ShapeDtypeStruct((B * T, dm), BF16),
            jax.ShapeDtypeStruct((B, CONV_W - 1, 2 * dm), F32),
            jax.ShapeDtypeStruct((B, H_M, dh, dh), F32),
            jax.ShapeDtypeStruct((B, H_M, dh), F32),
            jax.ShapeDtypeStruct((B, 1, H_M), F32),
        ],
        scratch_shapes=[
            pltpu.VMEM((2 * dm // LANES, CONV_PAD + cps * L, LANES), F32),
            pltpu.VMEM((2 * dm // LANES, L, LANES), F32),
            pltpu.VMEM((H_M, dh, dh), F32),
            pltpu.VMEM((H_M, dh), F32),
            pltpu.VMEM((1, H_M), F32),
        ],
        compiler_params=_params(("parallel", "arbitrary")),
        name="mlstm",
    )(proj, proj, proj, gates, conv_prev, c0, n0, m0, w_conv, b_conv, g_mnorm)


def _hgrn_factors(qq, kk, a, qa, sub, nb):
    L = a.shape[0]
    refs = [jnp.zeros_like(a[0:1, :])] + [a[i * sub - 1:i * sub, :] for i in range(1, nb)]
    qs = []
    ks = []
    for i in range(nb):
        rows = slice(i * sub, (i + 1) * sub)
        nk = (i + 1) * sub
        q_i = qa[rows, :] if i == 0 else (qq[rows, :] * jnp.exp(a[rows, :] - refs[i])).astype(BF16)
        k_i = (kk[:nk, :] * jnp.exp(jnp.minimum(refs[i] - a[:nk, :], SAFE_SPAN))).astype(BF16)
        zq = lambda n: jnp.zeros((n, q_i.shape[1]), BF16)
        qs.append(jnp.concatenate([z for z in (zq(i * sub), q_i, zq(L - nk)) if z.shape[0]], axis=0))
        ks.append(jnp.concatenate([z for z in (k_i, zq(L - nk)) if z.shape[0]], axis=0))
    return refs, qs, ks


def _hgrn_scores(qs, ks, sl, L):
    lhs = jnp.concatenate([q[:, sl] for q in qs], axis=1) if len(qs) > 1 else qs[0][:, sl]
    rhs = jnp.concatenate([k[:, sl] for k in ks], axis=1) if len(ks) > 1 else ks[0][:, sl]
    return lax.dot_general(lhs, rhs, NT_DIMS, preferred_element_type=F32)


def _hgrn_kernel(f_ref, i_ref, q_ref, g_ref, lbl_ref, s0_ref, gnorm_ref,
                 ohg_ref, s1_ref,
                 st_scr, a_scr, k_scr, sc_scr, o_scr, in_scr, *, L, nh, cps):
    c = pl.program_id(1)
    nc = pl.num_programs(1)
    sub = min(SUB, L)
    nb = L // sub

    @pl.when(c == 0)
    def _():
        for h in range(nh):
            st_scr[h] = s0_ref[h].T

    lbl = lbl_ref[...]
    mx = jnp.max(lbl, axis=0, keepdims=True)
    ex = jnp.exp(lbl - mx)
    lb = ex[0:1, :] / jnp.sum(ex, axis=0, keepdims=True)
    gnorm = gnorm_ref[...]
    rr = lax.broadcasted_iota(jnp.int32, (L, L), 0)
    cc = lax.broadcasted_iota(jnp.int32, (L, L), 1)
    causal = cc <= rr
    tril = causal.astype(BF16)
    tril3 = jnp.concatenate([tril, tril, tril], axis=1)

    def finish(rows):
        gg = g_ref[rows, :]
        ohg_ref[rows, :] = (_rms(o_scr[rows, :], gnorm) * (gg * _sigmoid(gg))).astype(BF16)

    span = None
    for j in range(cps):
        cr = slice(j * L, (j + 1) * L)
        f = lb + (1.0 - lb) * _sigmoid(f_ref[cr, :])
        lf = jnp.log(f)
        kk = 1.0 - f
        qq = q_ref[cr, :]
        vb = i_ref[cr, :].astype(BF16)

        hi = lf.astype(BF16)
        r1 = lf - hi.astype(F32)
        mid = r1.astype(BF16)
        lo = (r1 - mid.astype(F32)).astype(BF16)
        a = jnp.dot(tril3, jnp.concatenate([hi, mid, lo], axis=0), preferred_element_type=F32)
        a_scr[cr, :] = a
        k_scr[cr, :] = kk

        a_last = a[L - 1:L, :]
        qa = (qq * jnp.exp(a)).astype(BF16)
        kdec = (kk * jnp.exp(a_last - a)).astype(BF16)
        dec_last = jnp.exp(a_last)
        refs, qs, ks = _hgrn_factors(qq, kk, a, qa, sub, nb)
        for i in range(nb):
            sp = refs[i] - a[(i + 1) * sub - 1:(i + 1) * sub, :]
            span = sp if span is None else jnp.maximum(span, sp)

        heads = [slice(h * DH_H, (h + 1) * DH_H) for h in range(nh)]
        st_old = [st_scr[h] for h in range(nh)]
        inter = [lax.dot_general(qa[:, sl], st_old[h].astype(BF16), NT_DIMS, preferred_element_type=F32)
                 for h, sl in enumerate(heads)]
        raw = [_hgrn_scores(qs, ks, sl, L) for sl in heads]
        upd = [lax.dot_general(vb[:, sl], kdec[:, sl], TN_DIMS, preferred_element_type=F32) for sl in heads]
        for h, sl in enumerate(heads):
            st_scr[h] = st_old[h] * dec_last[:, sl] + upd[h]
            in_scr[cr, sl] = inter[h]
        blk = [jnp.where(causal, r, 0.0).astype(BF16) for r in raw]
        for h, sl in enumerate(heads):
            o_scr[cr, sl] = inter[h] + jnp.dot(blk[h], vb[:, sl], preferred_element_type=F32)
        finish(cr)

    use_exact = jnp.logical_not(jnp.max(span) <= SAFE_SPAN)

    @pl.when(use_exact)
    def _():
        sc_scr[...] = jnp.zeros_like(sc_scr)
        lane = lax.broadcasted_iota(jnp.int32, (sub, LANES), 1)
        in_block = jnp.logical_and(causal, (cc // sub) == (rr // sub))

        def fix_chunk(j, carry):
            base = pl.multiple_of(j * L, L)

            def diag_body(s, carry2):
                r0 = pl.multiple_of((s // sub) * sub, sub)
                a_blk = a_scr[pl.ds(base + r0, sub), :]
                q_blk = q_ref[pl.ds(base + r0, sub), :]
                a_s = a_scr[pl.ds(base + s, 1), :]
                k_s = k_scr[pl.ds(base + s, 1), :]
                p = q_blk * jnp.exp(jnp.minimum(a_blk - a_s, 0.0)) * k_s
                for h in range(nh):
                    colv = jnp.sum(p[:, h * DH_H:(h + 1) * DH_H], axis=-1, keepdims=True)
                    old = sc_scr[h, pl.ds(r0, sub), :]
                    sc_scr[h, pl.ds(r0, sub), :] = jnp.where(lane == s, colv, old)
                return carry2

            lax.fori_loop(0, L, diag_body, 0)

            cr = pl.ds(base, L)
            a = a_scr[cr, :]
            kk = k_scr[cr, :]
            qq = q_ref[cr, :]
            vb = i_ref[cr, :].astype(BF16)
            qa = (qq * jnp.exp(a)).astype(BF16)
            _, qs, ks = _hgrn_factors(qq, kk, a, qa, sub, nb)
            for h in range(nh):
                sl = slice(h * DH_H, (h + 1) * DH_H)
                blk = jnp.where(causal, _hgrn_scores(qs, ks, sl, L), 0.0)
                blk = jnp.where(in_block, sc_scr[h][:, :L], blk).astype(BF16)
                o_scr[cr, sl] = in_scr[cr, sl] + jnp.dot(blk, vb[:, sl], preferred_element_type=F32)
            finish(cr)
            return carry

        lax.fori_loop(0, cps, fix_chunk, 0)

    @pl.when(c == nc - 1)
    def _():
        for h in range(nh):
            s1_ref[h] = st_scr[h].T


def _hgrn(proj, lb_logits, s0, g_hnorm, *, B, T):
    L = min(CHUNK, T)
    nc = T // L
    cps = HGRN_CPS if nc % HGRN_CPS == 0 else 1
    ns = nc // cps
    nh = s0.shape[1]
    dhh = nh * DH_H
    row = lambda b, c: b * ns + c
    kern = functools.partial(_hgrn_kernel, L=L, nh=nh, cps=cps)
    return pl.pallas_call(
        kern,
        grid=(B, ns),
        in_specs=[
            pl.BlockSpec((cps * L, dhh), lambda b, c: (row(b, c), 0)),
            pl.BlockSpec((cps * L, dhh), lambda b, c: (row(b, c), 1)),
            pl.BlockSpec((cps * L, dhh), lambda b, c: (row(b, c), 2)),
            pl.BlockSpec((cps * L, dhh), lambda b, c: (row(b, c), 3)),
            pl.BlockSpec((2, dhh), lambda b, c: (0, 0)),
            pl.BlockSpec((None, nh, DH_H, DH_H), lambda b, c: (b, 0, 0, 0)),
            pl.BlockSpec((1, dhh), lambda b, c: (0, 0)),
        ],
        out_specs=[
            pl.BlockSpec((cps * L, dhh), lambda b, c: (row(b, c), 0)),
            pl.BlockSpec((None, nh, DH_H, DH_H), lambda b, c: (b, 0, 0, 0)),
        ],
        out_shape=[
            jax.ShapeDtypeStruct((B * T, dhh), BF16),
            jax.ShapeDtypeStruct((B, nh, DH_H, DH_H), F32),
        ],
        scratch_shapes=[
            pltpu.VMEM((nh, DH_H, DH_H), F32),
            pltpu.VMEM((cps * L, dhh), F32),
            pltpu.VMEM((cps * L, dhh), F32),
            pltpu.VMEM((nh, L, LANES), F32),
            pltpu.VMEM((cps * L, dhh), F32),
            pltpu.VMEM((cps * L, dhh), F32),
        ],
        compiler_params=_params(("parallel", "arbitrary")),
        name="hgrn",
    )(proj, proj, proj, proj, lb_logits, s0, g_hnorm)


def _merge_kernel(hmg_ref, ohg_ref, ga_ref, gb_ref, x_ref, wa_ref, wb_ref, wo_ref, gffn_ref, x1_ref, h2_ref):
    ya = jnp.dot(hmg_ref[...], wa_ref[...], preferred_element_type=F32)
    yb = jnp.dot(ohg_ref[...], wb_ref[...], preferred_element_type=F32)
    u = _sigmoid(ga_ref[...]) * ya + _sigmoid(gb_ref[...]) * yb
    x1 = x_ref[...] + jnp.dot(u.astype(BF16), wo_ref[...], preferred_element_type=F32)
    x1_ref[...] = x1
    h2_ref[...] = _rms(x1, gffn_ref[...]).astype(BF16)


def _merge(hmg, ohg, proj, x2d, w_a, w_b, w_o, g_ffn):
    m, d = x2d.shape
    dm = hmg.shape[1]
    tm = min(m, MERGE_ROWS)
    const = lambda i: (0, 0)
    return pl.pallas_call(
        _merge_kernel,
        grid=(m // tm,),
        in_specs=[
            pl.BlockSpec((tm, dm), lambda i: (i, 0)),
            pl.BlockSpec((tm, dm), lambda i: (i, 0)),
            pl.BlockSpec((tm, d), lambda i: (i, 2)),
            pl.BlockSpec((tm, d), lambda i: (i, 3)),
            pl.BlockSpec((tm, d), lambda i: (i, 0)),
            pl.BlockSpec((dm, d), const, pipeline_mode=pl.Buffered(1)),
            pl.BlockSpec((dm, d), const, pipeline_mode=pl.Buffered(1)),
            pl.BlockSpec((d, d), const, pipeline_mode=pl.Buffered(1)),
            pl.BlockSpec((1, d), const),
        ],
        out_specs=[pl.BlockSpec((tm, d), lambda i: (i, 0)), pl.BlockSpec((tm, d), lambda i: (i, 0))],
        out_shape=[jax.ShapeDtypeStruct((m, d), F32), jax.ShapeDtypeStruct((m, d), BF16)],
        compiler_params=_params(("parallel",)),
        name="merge",
    )(hmg, ohg, proj, proj, x2d, w_a, w_b, w_o, g_ffn)


def _ffn_kernel(x1_ref, h2_ref, w1_ref, w2_ref, gfin_ref, y_ref, acc_scr):
    j = pl.program_id(1)

    h = h2_ref[...]
    tf = w1_ref.shape[1]
    pieces = [slice(p, p + FFN_PIECE) for p in range(0, tf, FFN_PIECE)]
    ts = []
    for ps in pieces:
        t = jnp.maximum(jnp.dot(h, w1_ref[:, ps], preferred_element_type=F32), 0.0)
        ts.append((t * t).astype(BF16))
    acc = jnp.where(j > 0, acc_scr[...], 0.0)
    for ps, t in zip(pieces, ts):
        acc = acc + jnp.dot(t, w2_ref[ps, :], preferred_element_type=F32)
    acc_scr[...] = acc

    @pl.when(j == pl.num_programs(1) - 1)
    def _():
        y_ref[...] = _rms(x1_ref[...] + acc_scr[...], gfin_ref[...])


def _ffn(x1, h2, w1, w2, g_final):
    m, d = x1.shape
    dff = w1.shape[1]
    tm = min(m, FFN_ROWS)
    tf = FFN_HIDDEN
    return pl.pallas_call(
        _ffn_kernel,
        grid=(m // tm, dff // tf),
        in_specs=[
            pl.BlockSpec((tm, d), lambda i, j: (i, 0)),
            pl.BlockSpec((tm, d), lambda i, j: (i, 0)),
            pl.BlockSpec((d, tf), lambda i, j: (0, j)),
            pl.BlockSpec((tf, d), lambda i, j: (j, 0)),
            pl.BlockSpec((1, d), lambda i, j: (0, 0)),
        ],
        out_specs=pl.BlockSpec((tm, d), lambda i, j: (i, 0)),
        out_shape=jax.ShapeDtypeStruct((m, d), F32),
        scratch_shapes=[pltpu.VMEM((tm, d), F32)],
        compiler_params=_params(("parallel", "arbitrary")),
        name="ffn",
    )(x1, h2, w1, w2, g_final)


def _group(x, conv_prev, c0, n0, m0, s0, wts):
    B, T, d = x.shape
    x2d = x.reshape(B * T, d)
    proj_lo, proj_hi, gates = _inproj(x2d, wts["g_mix"], wts["w_all"], wts["n_lo"], wts["w_hi"], wts["w_if"], wts["b_if"])
    hmg, new_conv, c1, n1, m1 = _mlstm(proj_lo, gates, conv_prev, c0, n0, m0.reshape(B, 1, -1),
                                       wts["w_conv"], wts["b_conv"], wts["g_mnorm"], B=B, T=T)
    ohg, s1 = _hgrn(proj_hi, wts["lb_logits"], s0, wts["g_hnorm"], B=B, T=T)
    x1, h2 = _merge(hmg, ohg, proj_hi, x2d, wts["w_a"], wts["w_b"], wts["w_o"], wts["g_ffn"])
    y = _ffn(x1, h2, wts["w_ff1"], wts["w_ff2"], wts["g_final"])
    return (y.reshape(B, T, d), new_conv[None], c1[None], n1[None], m1.reshape(1, B, -1), s1[None])


def kernel(x_prompt, x_sample, cache_mlstm_conv, state_mlstm_C, state_mlstm_n, state_mlstm_m, state_hgrn_S,
           g_mix, w_in, b_if, w_conv, b_conv, g_mnorm, g_hnorm, hgrn_lb_logits, w_branch_a, w_branch_b,
           w_out, g_ffn, w_ff1, w_ff2, g_final):
    depth = w_in.shape[0]
    assert depth == 1, "single-layer trunk only"
    d = x_prompt.shape[-1]
    dm = w_branch_a.shape[1]
    dhh = w_branch_b.shape[1]
    nh_h = dhh // DH_H
    dh_m = dm // H_M
    assert hgrn_lb_logits.shape[0] == depth + 1
    assert 2 * dm == d and dhh == dm

    w = w_in[0]
    n_if = 2 * H_M
    p_if = 4 * dm
    w_all = w.astype(BF16)
    w_hi = w_all[:, p_if + n_if:]
    w_if = jnp.pad(w_all[:, p_if:p_if + n_if], ((0, 0), (0, LANES - n_if)))
    bif = jnp.pad(b_if[0].astype(F32), (0, LANES - n_if)).reshape(1, LANES)
    wts = dict(
        g_mix=g_mix[0].reshape(1, d), w_all=w_all, n_lo=p_if, w_hi=w_hi, w_if=w_if, b_if=bif,
        w_conv=w_conv[0], b_conv=b_conv[0].reshape(1, -1), g_mnorm=g_mnorm[0].reshape(1, dm),
        lb_logits=hgrn_lb_logits.astype(F32), g_hnorm=g_hnorm[0].reshape(1, dhh),
        w_a=w_branch_a[0].astype(BF16), w_b=w_branch_b[0].astype(BF16), w_o=w_out[0].astype(BF16),
        g_ffn=g_ffn[0].reshape(1, d), w_ff1=w_ff1[0].astype(BF16), w_ff2=w_ff2[0].astype(BF16),
        g_final=g_final.reshape(1, d),
    )

    bp = x_prompt.shape[0]
    zeros = lambda *s: jnp.zeros(s, F32)
    outs_p = _group(x_prompt, zeros(bp, CONV_W - 1, 2 * dm), zeros(bp, H_M, dh_m, dh_m), zeros(bp, H_M, dh_m),
                    zeros(bp, H_M), zeros(bp, nh_h, DH_H, DH_H), wts)
    outs_s = _group(x_sample, cache_mlstm_conv[0], state_mlstm_C[0].astype(F32), state_mlstm_n[0].astype(F32),
                    state_mlstm_m[0].astype(F32), state_hgrn_S[0].astype(F32), wts)
    return (outs_p[0], outs_s[0]) + outs_p[1:] + outs_s[1:]
```

```python
import functools

import jax
import jax.numpy as jnp
from jax import lax
from jax.experimental import pallas as pl
from jax.experimental.pallas import tpu as pltpu

F32 = jnp.float32
BF16 = jnp.bfloat16
HIGHEST = lax.Precision.HIGHEST

EPS = 1e-6
CHUNK = 64
MLSTM_CHUNK = 256
MLSTM_CPS = 2
CONV_W = 4
H_M = 4
DH_H = 128
SUB = 32
HGRN_CPS = 8
SAFE_SPAN = 60.0
LANES = 128
CONV_PAD = 8
FFN_PIECE = 256
VMEM_LIMIT = 56 * 1024 * 1024

MATMUL_ROWS, MATMUL_COLS = 1024, 2048
NORM_ROWS = 1024
MERGE_ROWS = 256
FFN_ROWS, FFN_HIDDEN = 512, 1024

NT_DIMS = (((1,), (1,)), ((), ()))
TN_DIMS = (((0,), (0,)), ((), ()))


def _rms(x, g):
    return x * lax.rsqrt(jnp.mean(x * x, axis=-1, keepdims=True) + EPS) * g


def _sigmoid(x):
    return 0.5 * jnp.tanh(0.5 * x) + 0.5


def _params(sem):
    return pltpu.CompilerParams(dimension_semantics=sem, vmem_limit_bytes=VMEM_LIMIT)


def _norm_kernel(x_ref, g_ref, wif_ref, bif_ref, h_ref, gates_ref):
    hb = _rms(x_ref[...], g_ref[...]).astype(BF16)
    h_ref[...] = hb
    gates_ref[...] = jnp.dot(hb, wif_ref[...], preferred_element_type=F32) + bif_ref[...]


def _matmul_kernel(h_ref, w_ref, o_ref):
    o_ref[...] = jnp.dot(h_ref[...], w_ref[...], preferred_element_type=F32)


def _matmul(h, w, n, name):
    m, d = h.shape
    tm = min(m, MATMUL_ROWS)
    tn = MATMUL_COLS
    return pl.pallas_call(
        _matmul_kernel,
        grid=(m // tm, n // tn),
        in_specs=[
            pl.BlockSpec((tm, d), lambda i, j: (i, 0)),
            pl.BlockSpec((d, tn), lambda i, j: (0, j)),
        ],
        out_specs=pl.BlockSpec((tm, tn), lambda i, j: (i, j)),
        out_shape=jax.ShapeDtypeStruct((m, n), F32),
        compiler_params=_params(("parallel", "arbitrary")),
        name=name,
    )(h, w)


def _inproj(x2d, g_mix, w_all, n_lo, w_hi, w_if, b_if):
    m, d = x2d.shape
    tm = min(m, NORM_ROWS)
    h, gates = pl.pallas_call(
        _norm_kernel,
        grid=(m // tm,),
        in_specs=[
            pl.BlockSpec((tm, d), lambda i: (i, 0)),
            pl.BlockSpec((1, d), lambda i: (0, 0)),
            pl.BlockSpec((d, LANES), lambda i: (0, 0)),
            pl.BlockSpec((1, LANES), lambda i: (0, 0)),
        ],
        out_specs=[pl.BlockSpec((tm, d), lambda i: (i, 0)), pl.BlockSpec((tm, LANES), lambda i: (i, 0))],
        out_shape=[jax.ShapeDtypeStruct((m, d), BF16), jax.ShapeDtypeStruct((m, LANES), F32)],
        compiler_params=_params(("parallel",)),
        name="norm",
    )(x2d, g_mix, w_if, b_if)
    return _matmul(h, w_all, n_lo, "inproj_lo"), _matmul(h, w_hi, w_hi.shape[1], "inproj_hi"), gates


def _mlstm_kernel(qk_ref, v_ref, o_ref, gates_ref, convprev_ref, c0_ref, n0_ref, m0_ref,
                  wconv_ref, bconv_ref, gnorm_ref,
                  hmg_ref, newconv_ref, c1_ref, n1_ref, m1_ref,
                  xpad, qk_scr, c_scr, n_scr, m_scr, *, L, dm, dh, cps):
    c = pl.program_id(1)
    nc = pl.num_programs(1)
    hist0 = CONV_PAD - (CONV_W - 1)
    groups = [slice(g * LANES, (g + 1) * LANES) for g in range(2 * dm // LANES)]

    @pl.when(c == 0)
    def _():
        for g, gl in enumerate(groups):
            xpad[g, hist0:CONV_PAD, :] = convprev_ref[:, gl]
        c_scr[...] = c0_ref[...]
        n_scr[...] = n0_ref[...]
        m_scr[...] = m0_ref[...]

    rr = lax.broadcasted_iota(jnp.int32, (L, L), 0)
    cc = lax.broadcasted_iota(jnp.int32, (L, L), 1)
    causal = rr >= cc
    tril = causal.astype(F32)
    eye = (lax.broadcasted_iota(jnp.int32, (8, LANES), 0)
           == lax.broadcasted_iota(jnp.int32, (8, LANES), 1)).astype(F32)
    heads = [slice(h * dh, (h + 1) * dh) for h in range(H_M)]
    k_scale = dh ** -0.5

    pending = []
    for j in range(cps):
        cr = slice(j * L, (j + 1) * L)
        x0 = CONV_PAD + j * L
        ns = L // 8
        for g, gl in enumerate(groups):
            xpad[g, x0:x0 + L, :] = qk_ref[cr, gl]
            slabs = {s: xpad[g, pl.ds(x0 + s, ns, stride=8), :] for s in range(-(CONV_W - 1), 8)}
            for r in range(8):
                conv = bconv_ref[:, gl]
                for i in range(CONV_W):
                    conv = conv + slabs[r - (CONV_W - 1) + i] * wconv_ref[i:i + 1, gl]
                qk_scr[g, pl.ds(r, ns, stride=8), :] = conv * _sigmoid(conv)
        qk = jnp.concatenate([qk_scr[g] for g in range(len(groups))], axis=1)
        q = qk[:, :dm]
        k = qk[:, dm:]
        qb = q.astype(BF16)
        kb = k.astype(BF16)
        vv = v_ref[cr, :]
        vb = vv.astype(BF16)

        g = gates_ref[cr, :]
        col = lax.broadcasted_iota(jnp.int32, g.shape, 1)
        lf = jnp.minimum(g, 0.0) - jnp.log1p(jnp.exp(-jnp.abs(g)))
        x_cols = jnp.where(col < H_M, g, lf)
        cum = jnp.dot(tril, x_cols, preferred_element_type=F32, precision=HIGHEST)
        xc = jnp.where(col < H_M, g, cum)
        rows = lax.dot_general(eye, xc, NT_DIMS, preferred_element_type=F32, precision=HIGHEST)

        m_all = m_scr[...]
        e_s, sc_inter, inv_floor, decay, wgt, m_new = [], [], [], [], [], []
        for h in range(H_M):
            bcol = cum[:, H_M + h:H_M + h + 1]
            igcol = g[:, h:h + 1]
            brow = rows[H_M + h:H_M + h + 1, :]
            igrow = rows[h:h + 1, :]
            m_prev = m_all[:, h:h + 1]
            dlog = jnp.where(causal, bcol - brow + igrow, -jnp.inf)
            m_inter = bcol + m_prev
            m_t = jnp.maximum(m_inter, jnp.max(dlog, axis=-1, keepdims=True))
            e_s.append(jnp.exp(dlog - m_t) * k_scale)
            sc_inter.append(jnp.exp(m_inter - m_t))
            inv_floor.append(jnp.exp(-m_t))
            mn = m_t[L - 1:L, :]
            b_last = bcol[L - 1:L, :]
            m_new.append(mn)
            decay.append(jnp.exp(b_last + m_prev - mn))
            wgt.append(jnp.exp(b_last - bcol + igcol - mn) * k_scale)

        qn = []
        for h, sl in enumerate(heads):
            n_old = n_scr[h:h + 1, :]
            qn.append(jnp.sum(q[:, sl] * n_old, axis=-1, keepdims=True))
            n_scr[h:h + 1, :] = decay[h] * n_old + jnp.sum(wgt[h] * k[:, sl], axis=0, keepdims=True)
            m_scr[:, h:h + 1] = m_new[h]

        c_old = [c_scr[h] for h in range(H_M)]
        s_raw = [lax.dot_general(qb[:, sl], kb[:, sl], NT_DIMS, preferred_element_type=F32) for sl in heads]
        qc = [lax.dot_general(qb[:, sl], c_old[h].astype(BF16), NT_DIMS, preferred_element_type=F32)
              for h, sl in enumerate(heads)]
        upd = [lax.dot_general((vv[:, sl] * wgt[h]).astype(BF16), kb[:, sl], TN_DIMS, preferred_element_type=F32)
               for h, sl in enumerate(heads)]
        for h in range(H_M):
            c_scr[h] = decay[h] * c_old[h] + upd[h]
        pending.append((cr, s_raw, e_s, sc_inter, inv_floor, qc, qn, vb))

    for cr, s_raw, e_s, sc_inter, inv_floor, qc, qn, vb in pending:
        s = [s_raw[h] * e_s[h] for h in range(H_M)]
        pv = [jnp.dot(s[h].astype(BF16), vb[:, sl], preferred_element_type=F32) for h, sl in enumerate(heads)]
        oo = o_ref[cr, :]
        for h, sl in enumerate(heads):
            num = sc_inter[h] * qc[h] + pv[h]
            den = sc_inter[h] * qn[h] + jnp.sum(s[h], axis=-1, keepdims=True)
            hout = num / jnp.maximum(jnp.abs(den), inv_floor[h])
            y = _rms(hout, gnorm_ref[:, sl]) * _sigmoid(oo[:, sl])
            hmg_ref[cr, sl] = y.astype(BF16)

    x_end = CONV_PAD + cps * L
    hist_g = [xpad[g, x_end - (CONV_W - 1):x_end, :] for g in range(len(groups))]
    for g in range(len(groups)):
        xpad[g, hist0:CONV_PAD, :] = hist_g[g]
    hist = jnp.concatenate(hist_g, axis=1)

    @pl.when(c == nc - 1)
    def _():
        newconv_ref[...] = hist
        c1_ref[...] = c_scr[...]
        n1_ref[...] = n_scr[...]
        m1_ref[...] = m_scr[...]


def _mlstm(proj, gates, conv_prev, c0, n0, m0, w_conv, b_conv, g_mnorm, *, B, T):
    L = MLSTM_CHUNK if T % MLSTM_CHUNK == 0 else min(CHUNK, T)
    nc = T // L
    dm = c0.shape[1] * c0.shape[2]
    dh = c0.shape[2]
    cps = MLSTM_CPS if nc % MLSTM_CPS == 0 else 1
    ns = nc // cps
    row = lambda b, c: b * ns + c
    kern = functools.partial(_mlstm_kernel, L=L, dm=dm, dh=dh, cps=cps)
    return pl.pallas_call(
        kern,
        grid=(B, ns),
        in_specs=[
            pl.BlockSpec((cps * L, 2 * dm), lambda b, c: (row(b, c), 0)),
            pl.BlockSpec((cps * L, dm), lambda b, c: (row(b, c), 2)),
            pl.BlockSpec((cps * L, dm), lambda b, c: (row(b, c), 3)),
            pl.BlockSpec((cps * L, LANES), lambda b, c: (row(b, c), 0)),
            pl.BlockSpec((None, CONV_W - 1, 2 * dm), lambda b, c: (b, 0, 0)),
            pl.BlockSpec((None, H_M, dh, dh), lambda b, c: (b, 0, 0, 0)),
            pl.BlockSpec((None, H_M, dh), lambda b, c: (b, 0, 0)),
            pl.BlockSpec((None, 1, H_M), lambda b, c: (b, 0, 0)),
            pl.BlockSpec((CONV_W, 2 * dm), lambda b, c: (0, 0)),
            pl.BlockSpec((1, 2 * dm), lambda b, c: (0, 0)),
            pl.BlockSpec((1, dm), lambda b, c: (0, 0)),
        ],
        out_specs=[
            pl.BlockSpec((cps * L, dm), lambda b, c: (row(b, c), 0)),
            pl.BlockSpec((None, CONV_W - 1, 2 * dm), lambda b, c: (b, 0, 0)),
            pl.BlockSpec((None, H_M, dh, dh), lambda b, c: (b, 0, 0, 0)),
            pl.BlockSpec((None, H_M, dh), lambda b, c: (b, 0, 0)),
            pl.BlockSpec((None, 1, H_M), lambda b, c: (b, 0, 0)),
        ],
        out_shape=[
            jax.ShapeDtypeStruct((B * T, dm), BF16),
            jax.ShapeDtypeStruct((B, CONV_W - 1, 2 * dm), F32),
            jax.ShapeDtypeStruct((B, H_M, dh, dh), F32),
            jax.ShapeDtypeStruct((B, H_M, dh), F32),
            jax.ShapeDtypeStruct((B, 1, H_M), F32),
        ],
        scratch_shapes=[
            pltpu.VMEM((2 * dm // LANES, CONV_PAD + cps * L, LANES), F32),
            pltpu.VMEM((2 * dm // LANES, L, LANES), F32),
            pltpu.VMEM((H_M, dh, dh), F32),
            pltpu.VMEM((H_M, dh), F32),
            pltpu.VMEM((1, H_M), F32),
        ],
        compiler_params=_params(("parallel", "arbitrary")),
        name="mlstm",
    )(proj, proj, proj, gates, conv_prev, c0, n0, m0, w_conv, b_conv, g_mnorm)


def _hgrn_factors(qq, kk, a, qa, sub, nb):
    L = a.shape[0]
    refs = [jnp.zeros_like(a[0:1, :])] + [a[i * sub - 1:i * sub, :] for i in range(1, nb)]
    qs = []
    ks = []
    for i in range(nb):
        rows = slice(i * sub, (i + 1) * sub)
        nk = (i + 1) * sub
        q_i = qa[rows, :] if i == 0 else (qq[rows, :] * jnp.exp(a[rows, :] - refs[i])).astype(BF16)
        k_i = (kk[:nk, :] * jnp.exp(jnp.minimum(refs[i] - a[:nk, :], SAFE_SPAN))).astype(BF16)
        zq = lambda n: jnp.zeros((n, q_i.shape[1]), BF16)
        qs.append(jnp.concatenate([z for z in (zq(i * sub), q_i, zq(L - nk)) if z.shape[0]], axis=0))
        ks.append(jnp.concatenate([z for z in (k_i, zq(L - nk)) if z.shape[0]], axis=0))
    return refs, qs, ks


def _hgrn_scores(qs, ks, sl, L):
    lhs = jnp.concatenate([q[:, sl] for q in qs], axis=1) if len(qs) > 1 else qs[0][:, sl]
    rhs = jnp.concatenate([k[:, sl] for k in ks], axis=1) if len(ks) > 1 else ks[0][:, sl]
    return lax.dot_general(lhs, rhs, NT_DIMS, preferred_element_type=F32)


def _hgrn_kernel(f_ref, i_ref, q_ref, g_ref, lbl_ref, s0_ref, gnorm_ref,
                 ohg_ref, s1_ref,
                 st_scr, a_scr, k_scr, sc_scr, o_scr, in_scr, *, L, nh, cps):
    c = pl.program_id(1)
    nc = pl.num_programs(1)
    sub = min(SUB, L)
    nb = L // sub

    @pl.when(c == 0)
    def _():
        for h in range(nh):
            st_scr[h] = s0_ref[h].T

    lbl = lbl_ref[...]
    mx = jnp.max(lbl, axis=0, keepdims=True)
    ex = jnp.exp(lbl - mx)
    lb = ex[0:1, :] / jnp.sum(ex, axis=0, keepdims=True)
    gnorm = gnorm_ref[...]
    rr = lax.broadcasted_iota(jnp.int32, (L, L), 0)
    cc = lax.broadcasted_iota(jnp.int32, (L, L), 1)
    causal = cc <= rr
    tril = causal.astype(BF16)
    tril3 = jnp.concatenate([tril, tril, tril], axis=1)

    def finish(rows):
        gg = g_ref[rows, :]
        ohg_ref[rows, :] = (_rms(o_scr[rows, :], gnorm) * (gg * _sigmoid(gg))).astype(BF16)

    span = None
    heads = [slice(h * DH_H, (h + 1) * DH_H) for h in range(nh)]
    every = slice(None)
    for j in range(cps):
        cr = slice(j * L, (j + 1) * L)
        qa, kdec, vb, dec_last, qs, ks = [], [], [], [], [], []
        for sl in heads:
            lb_h = lb[:, sl]
            f = lb_h + (1.0 - lb_h) * _sigmoid(f_ref[cr, sl])
            lf = jnp.log(f)
            kk = 1.0 - f
            qq = q_ref[cr, sl]
            vb.append(i_ref[cr, sl].astype(BF16))

            hi = lf.astype(BF16)
            r1 = lf - hi.astype(F32)
            mid = r1.astype(BF16)
            lo = (r1 - mid.astype(F32)).astype(BF16)
            a = jnp.dot(tril3, jnp.concatenate([hi, mid, lo], axis=0), preferred_element_type=F32)
            a_scr[cr, sl] = a
            k_scr[cr, sl] = kk

            a_last = a[L - 1:L, :]
            qa_h = (qq * jnp.exp(a)).astype(BF16)
            qa.append(qa_h)
            kdec.append((kk * jnp.exp(a_last - a)).astype(BF16))
            dec_last.append(jnp.exp(a_last))
            refs, qs_h, ks_h = _hgrn_factors(qq, kk, a, qa_h, sub, nb)
            qs.append(qs_h)
            ks.append(ks_h)
            for i in range(nb):
                sp = refs[i] - a[(i + 1) * sub - 1:(i + 1) * sub, :]
                span = sp if span is None else jnp.maximum(span, sp)

        st_old = [st_scr[h] for h in range(nh)]
        inter = [lax.dot_general(qa[h], st_old[h].astype(BF16), NT_DIMS, preferred_element_type=F32)
                 for h in range(nh)]
        raw = [_hgrn_scores(qs[h], ks[h], every, L) for h in range(nh)]
        upd = [lax.dot_general(vb[h], kdec[h], TN_DIMS, preferred_element_type=F32) for h in range(nh)]
        for h, sl in enumerate(heads):
            st_scr[h] = st_old[h] * dec_last[h] + upd[h]
            in_scr[cr, sl] = inter[h]
        blk = [jnp.where(causal, r, 0.0).astype(BF16) for r in raw]
        for h, sl in enumerate(heads):
            o_scr[cr, sl] = inter[h] + jnp.dot(blk[h], vb[h], preferred_element_type=F32)
        finish(cr)

    use_exact = jnp.logical_not(jnp.max(span) <= SAFE_SPAN)

    @pl.when(use_exact)
    def _():
        sc_scr[...] = jnp.zeros_like(sc_scr)
        lane = lax.broadcasted_iota(jnp.int32, (sub, LANES), 1)
        in_block = jnp.logical_and(causal, (cc // sub) == (rr // sub))

        def fix_chunk(j, carry):
            base = pl.multiple_of(j * L, L)

            def diag_body(s, carry2):
                r0 = pl.multiple_of((s // sub) * sub, sub)
                a_blk = a_scr[pl.ds(base + r0, sub), :]
                q_blk = q_ref[pl.ds(base + r0, sub), :]
                a_s = a_scr[pl.ds(base + s, 1), :]
                k_s = k_scr[pl.ds(base + s, 1), :]
                p = q_blk * jnp.exp(jnp.minimum(a_blk - a_s, 0.0)) * k_s
                for h in range(nh):
                    colv = jnp.sum(p[:, h * DH_H:(h + 1) * DH_H], axis=-1, keepdims=True)
                    old = sc_scr[h, pl.ds(r0, sub), :]
                    sc_scr[h, pl.ds(r0, sub), :] = jnp.where(lane == s, colv, old)
                return carry2

            lax.fori_loop(0, L, diag_body, 0)

            cr = pl.ds(base, L)
            a = a_scr[cr, :]
            kk = k_scr[cr, :]
            qq = q_ref[cr, :]
            vb = i_ref[cr, :].astype(BF16)
            qa = (qq * jnp.exp(a)).astype(BF16)
            _, qs, ks = _hgrn_factors(qq, kk, a, qa, sub, nb)
            for h in range(nh):
                sl = slice(h * DH_H, (h + 1) * DH_H)
                blk = jnp.where(causal, _hgrn_scores(qs, ks, sl, L), 0.0)
                blk = jnp.where(in_block, sc_scr[h][:, :L], blk).astype(BF16)
                o_scr[cr, sl] = in_scr[cr, sl] + jnp.dot(blk, vb[:, sl], preferred_element_type=F32)
            finish(cr)
            return carry

        lax.fori_loop(0, cps, fix_chunk, 0)

    @pl.when(c == nc - 1)
    def _():
        for h in range(nh):
            s1_ref[h] = st_scr[h].T


def _hgrn(proj, lb_logits, s0, g_hnorm, *, B, T):
    L = min(CHUNK, T)
    nc = T // L
    cps = HGRN_CPS if nc % HGRN_CPS == 0 else 1
    ns = nc // cps
    nh = s0.shape[1]
    dhh = nh * DH_H
    row = lambda b, c: b * ns + c
    kern = functools.partial(_hgrn_kernel, L=L, nh=nh, cps=cps)
    return pl.pallas_call(
        kern,
        grid=(B, ns),
        in_specs=[
            pl.BlockSpec((cps * L, dhh), lambda b, c: (row(b, c), 0)),
            pl.BlockSpec((cps * L, dhh), lambda b, c: (row(b, c), 1)),
            pl.BlockSpec((cps * L, dhh), lambda b, c: (row(b, c), 2)),
            pl.BlockSpec((cps * L, dhh), lambda b, c: (row(b, c), 3)),
            pl.BlockSpec((2, dhh), lambda b, c: (0, 0)),
            pl.BlockSpec((None, nh, DH_H, DH_H), lambda b, c: (b, 0, 0, 0)),
            pl.BlockSpec((1, dhh), lambda b, c: (0, 0)),
        ],
        out_specs=[
            pl.BlockSpec((cps * L, dhh), lambda b, c: (row(b, c), 0)),
            pl.BlockSpec((None, nh, DH_H, DH_H), lambda b, c: (b, 0, 0, 0)),
        ],
        out_shape=[
            jax.ShapeDtypeStruct((B * T, dhh), BF16),
            jax.ShapeDtypeStruct((B, nh, DH_H, DH_H), F32),
        ],
        scratch_shapes=[
            pltpu.VMEM((nh, DH_H, DH_H), F32),
            pltpu.VMEM((cps * L, dhh), F32),
            pltpu.VMEM((cps * L, dhh), F32),
            pltpu.VMEM((nh, L, LANES), F32),
            pltpu.VMEM((cps * L, dhh), F32),
            pltpu.VMEM((cps * L, dhh), F32),
        ],
        compiler_params=_params(("parallel", "arbitrary")),
        name="hgrn",
    )(proj, proj, proj, proj, lb_logits, s0, g_hnorm)


def _merge_kernel(hmg_ref, ohg_ref, ga_ref, gb_ref, x_ref, wa_ref, wb_ref, wo_ref, gffn_ref, x1_ref, h2_ref):
    ya = jnp.dot(hmg_ref[...], wa_ref[...], preferred_element_type=F32)
    yb = jnp.dot(ohg_ref[...], wb_ref[...], preferred_element_type=F32)
    u = _sigmoid(ga_ref[...]) * ya + _sigmoid(gb_ref[...]) * yb
    x1 = x_ref[...] + jnp.dot(u.astype(BF16), wo_ref[...], preferred_element_type=F32)
    x1_ref[...] = x1
    h2_ref[...] = _rms(x1, gffn_ref[...]).astype(BF16)


def _merge(hmg, ohg, proj, x2d, w_a, w_b, w_o, g_ffn):
    m, d = x2d.shape
    dm = hmg.shape[1]
    tm = min(m, MERGE_ROWS)
    const = lambda i: (0, 0)
    return pl.pallas_call(
        _merge_kernel,
        grid=(m // tm,),
        in_specs=[
            pl.BlockSpec((tm, dm), lambda i: (i, 0)),
            pl.BlockSpec((tm, dm), lambda i: (i, 0)),
            pl.BlockSpec((tm, d), lambda i: (i, 2)),
            pl.BlockSpec((tm, d), lambda i: (i, 3)),
            pl.BlockSpec((tm, d), lambda i: (i, 0)),
            pl.BlockSpec((dm, d), const, pipeline_mode=pl.Buffered(1)),
            pl.BlockSpec((dm, d), const, pipeline_mode=pl.Buffered(1)),
            pl.BlockSpec((d, d), const, pipeline_mode=pl.Buffered(1)),
            pl.BlockSpec((1, d), const),
        ],
        out_specs=[pl.BlockSpec((tm, d), lambda i: (i, 0)), pl.BlockSpec((tm, d), lambda i: (i, 0))],
        out_shape=[jax.ShapeDtypeStruct((m, d), F32), jax.ShapeDtypeStruct((m, d), BF16)],
        compiler_params=_params(("parallel",)),
        name="merge",
    )(hmg, ohg, proj, proj, x2d, w_a, w_b, w_o, g_ffn)


def _ffn_kernel(x1_ref, h2_ref, w1_ref, w2_ref, gfin_ref, y_ref, acc_scr):
    j = pl.program_id(1)

    h = h2_ref[...]
    tf = w1_ref.shape[1]
    pieces = [slice(p, p + FFN_PIECE) for p in range(0, tf, FFN_PIECE)]
    ts = []
    for ps in pieces:
        t = jnp.maximum(jnp.dot(h, w1_ref[:, ps], preferred_element_type=F32), 0.0)
        ts.append((t * t).astype(BF16))
    acc = jnp.where(j > 0, acc_scr[...], 0.0)
    for ps, t in zip(pieces, ts):
        acc = acc + jnp.dot(t, w2_ref[ps, :], preferred_element_type=F32)
    acc_scr[...] = acc

    @pl.when(j == pl.num_programs(1) - 1)
    def _():
        y_ref[...] = _rms(x1_ref[...] + acc_scr[...], gfin_ref[...])


def _ffn(x1, h2, w1, w2, g_final):
    m, d = x1.shape
    dff = w1.shape[1]
    tm = min(m, FFN_ROWS)
    tf = FFN_HIDDEN
    return pl.pallas_call(
        _ffn_kernel,
        grid=(m // tm, dff // tf),
        in_specs=[
            pl.BlockSpec((tm, d), lambda i, j: (i, 0)),
            pl.BlockSpec((tm, d), lambda i, j: (i, 0)),
            pl.BlockSpec((d, tf), lambda i, j: (0, j)),
            pl.BlockSpec((tf, d), lambda i, j: (j, 0)),
            pl.BlockSpec((1, d), lambda i, j: (0, 0)),
        ],
        out_specs=pl.BlockSpec((tm, d), lambda i, j: (i, 0)),
        out_shape=jax.ShapeDtypeStruct((m, d), F32),
        scratch_shapes=[pltpu.VMEM((tm, d), F32)],
        compiler_params=_params(("parallel", "arbitrary")),
        name="ffn",
    )(x1, h2, w1, w2, g_final)


def _group(x, conv_prev, c0, n0, m0, s0, wts):
    B, T, d = x.shape
    x2d = x.reshape(B * T, d)
    proj_lo, proj_hi, gates = _inproj(x2d, wts["g_mix"], wts["w_all"], wts["n_lo"], wts["w_hi"], wts["w_if"], wts["b_if"])
    hmg, new_conv, c1, n1, m1 = _mlstm(proj_lo, gates, conv_prev, c0, n0, m0.reshape(B, 1, -1),
                                       wts["w_conv"], wts["b_conv"], wts["g_mnorm"], B=B, T=T)
    ohg, s1 = _hgrn(proj_hi, wts["lb_logits"], s0, wts["g_hnorm"], B=B, T=T)
    x1, h2 = _merge(hmg, ohg, proj_hi, x2d, wts["w_a"], wts["w_b"], wts["w_o"], wts["g_ffn"])
    y = _ffn(x1, h2, wts["w_ff1"], wts["w_ff2"], wts["g_final"])
    return (y.reshape(B, T, d), new_conv[None], c1[None], n1[None], m1.reshape(1, B, -1), s1[None])


def kernel(x_prompt, x_sample, cache_mlstm_conv, state_mlstm_C, state_mlstm_n, state_mlstm_m, state_hgrn_S,
           g_mix, w_in, b_if, w_conv, b_conv, g_mnorm, g_hnorm, hgrn_lb_logits, w_branch_a, w_branch_b,
           w_out, g_ffn, w_ff1, w_ff2, g_final):
    depth = w_in.shape[0]
    assert depth == 1, "single-layer trunk only"
    d = x_prompt.shape[-1]
    dm = w_branch_a.shape[1]
    dhh = w_branch_b.shape[1]
    nh_h = dhh // DH_H
    dh_m = dm // H_M
    assert hgrn_lb_logits.shape[0] == depth + 1
    assert 2 * dm == d and dhh == dm

    w = w_in[0]
    n_if = 2 * H_M
    p_if = 4 * dm
    w_all = w.astype(BF16)
    w_hi = w_all[:, p_if + n_if:]
    w_if = jnp.pad(w_all[:, p_if:p_if + n_if], ((0, 0), (0, LANES - n_if)))
    bif = jnp.pad(b_if[0].astype(F32), (0, LANES - n_if)).reshape(1, LANES)
    wts = dict(
        g_mix=g_mix[0].reshape(1, d), w_all=w_all, n_lo=p_if, w_hi=w_hi, w_if=w_if, b_if=bif,
        w_conv=w_conv[0], b_conv=b_conv[0].reshape(1, -1), g_mnorm=g_mnorm[0].reshape(1, dm),
        lb_logits=hgrn_lb_logits.astype(F32), g_hnorm=g_hnorm[0].reshape(1, dhh),
        w_a=w_branch_a[0].astype(BF16), w_b=w_branch_b[0].astype(BF16), w_o=w_out[0].astype(BF16),
        g_ffn=g_ffn[0].reshape(1, d), w_ff1=w_ff1[0].astype(BF16), w_ff2=w_ff2[0].astype(BF16),
        g_final=g_final.reshape(1, d),
    )

    bp = x_prompt.shape[0]
    zeros = lambda *s: jnp.zeros(s, F32)
    outs_p = _group(x_prompt, zeros(bp, CONV_W - 1, 2 * dm), zeros(bp, H_M, dh_m, dh_m), zeros(bp, H_M, dh_m),
                    zeros(bp, H_M), zeros(bp, nh_h, DH_H, DH_H), wts)
    outs_s = _group(x_sample, cache_mlstm_conv[0], state_mlstm_C[0].astype(F32), state_mlstm_n[0].astype(F32),
                    state_mlstm_m[0].astype(F32), state_hgrn_S[0].astype(F32), wts)
    return (outs_p[0], outs_s[0]) + outs_p[1:] + outs_s[1:]
```

```python
import functools

import jax
import jax.numpy as jnp
from jax import lax
from jax.experimental import pallas as pl
from jax.experimental.pallas import tpu as pltpu

F32 = jnp.float32
BF16 = jnp.bfloat16
HIGHEST = lax.Precision.HIGHEST

EPS = 1e-6
CHUNK = 128
MLSTM_CHUNK = 256
MLSTM_CPS = 2
CONV_W = 4
H_M = 4
DH_H = 128
SUB = 32
HGRN_CPS = 4
SAFE_SPAN = 60.0
LANES = 128
CONV_PAD = 8
FFN_PIECE = 256
VMEM_LIMIT = 56 * 1024 * 1024

MATMUL_ROWS, MATMUL_COLS = 1024, 2048
NORM_ROWS = 1024
MERGE_ROWS = 256
FFN_ROWS, FFN_HIDDEN = 512, 1024

NT_DIMS = (((1,), (1,)), ((), ()))
TN_DIMS = (((0,), (0,)), ((), ()))


def _rms(x, g):
    return x * lax.rsqrt(jnp.mean(x * x, axis=-1, keepdims=True) + EPS) * g


def _sigmoid(x):
    return 0.5 * jnp.tanh(0.5 * x) + 0.5


def _params(sem):
    return pltpu.CompilerParams(dimension_semantics=sem, vmem_limit_bytes=VMEM_LIMIT)


def _norm_kernel(x_ref, g_ref, wif_ref, bif_ref, h_ref, gates_ref):
    hb = _rms(x_ref[...], g_ref[...]).astype(BF16)
    h_ref[...] = hb
    gates_ref[...] = jnp.dot(hb, wif_ref[...], preferred_element_type=F32) + bif_ref[...]


def _matmul_kernel(h_ref, w_ref, o_ref):
    o_ref[...] = jnp.dot(h_ref[...], w_ref[...], preferred_element_type=F32)


def _matmul(h, w, n, name):
    m, d = h.shape
    tm = min(m, MATMUL_ROWS)
    tn = MATMUL_COLS
    return pl.pallas_call(
        _matmul_kernel,
        grid=(m // tm, n // tn),
        in_specs=[
            pl.BlockSpec((tm, d), lambda i, j: (i, 0)),
            pl.BlockSpec((d, tn), lambda i, j: (0, j)),
        ],
        out_specs=pl.BlockSpec((tm, tn), lambda i, j: (i, j)),
        out_shape=jax.ShapeDtypeStruct((m, n), F32),
        compiler_params=_params(("parallel", "arbitrary")),
        name=name,
    )(h, w)


def _inproj(x2d, g_mix, w_all, n_lo, w_hi, w_if, b_if):
    m, d = x2d.shape
    tm = min(m, NORM_ROWS)
    h, gates = pl.pallas_call(
        _norm_kernel,
        grid=(m // tm,),
        in_specs=[
            pl.BlockSpec((tm, d), lambda i: (i, 0)),
            pl.BlockSpec((1, d), lambda i: (0, 0)),
            pl.BlockSpec((d, LANES), lambda i: (0, 0)),
            pl.BlockSpec((1, LANES), lambda i: (0, 0)),
        ],
        out_specs=[pl.BlockSpec((tm, d), lambda i: (i, 0)), pl.BlockSpec((tm, LANES), lambda i: (i, 0))],
        out_shape=[jax.ShapeDtypeStruct((m, d), BF16), jax.ShapeDtypeStruct((m, LANES), F32)],
        compiler_params=_params(("parallel",)),
        name="norm",
    )(x2d, g_mix, w_if, b_if)
    return _matmul(h, w_all, n_lo, "inproj_lo"), _matmul(h, w_hi, w_hi.shape[1], "inproj_hi"), gates


def _mlstm_kernel(qk_ref, v_ref, o_ref, gates_ref, convprev_ref, c0_ref, n0_ref, m0_ref,
                  wconv_ref, bconv_ref, gnorm_ref,
                  hmg_ref, newconv_ref, c1_ref, n1_ref, m1_ref,
                  xpad, qk_scr, c_scr, n_scr, m_scr, *, L, dm, dh, cps):
    c = pl.program_id(1)
    nc = pl.num_programs(1)
    hist0 = CONV_PAD - (CONV_W - 1)
    groups = [slice(g * LANES, (g + 1) * LANES) for g in range(2 * dm // LANES)]

    @pl.when(c == 0)
    def _():
        for g, gl in enumerate(groups):
            xpad[g, hist0:CONV_PAD, :] = convprev_ref[:, gl]
        c_scr[...] = c0_ref[...]
        n_scr[...] = n0_ref[...]
        m_scr[...] = m0_ref[...]

    rr = lax.broadcasted_iota(jnp.int32, (L, L), 0)
    cc = lax.broadcasted_iota(jnp.int32, (L, L), 1)
    causal = rr >= cc
    tril = causal.astype(F32)
    eye = (lax.broadcasted_iota(jnp.int32, (8, LANES), 0)
           == lax.broadcasted_iota(jnp.int32, (8, LANES), 1)).astype(F32)
    heads = [slice(h * dh, (h + 1) * dh) for h in range(H_M)]
    k_scale = dh ** -0.5

    pending = []
    for j in range(cps):
        cr = slice(j * L, (j + 1) * L)
        x0 = CONV_PAD + j * L
        ns = L // 8
        for g, gl in enumerate(groups):
            xpad[g, x0:x0 + L, :] = qk_ref[cr, gl]
            slabs = {s: xpad[g, pl.ds(x0 + s, ns, stride=8), :] for s in range(-(CONV_W - 1), 8)}
            for r in range(8):
                conv = bconv_ref[:, gl]
                for i in range(CONV_W):
                    conv = conv + slabs[r - (CONV_W - 1) + i] * wconv_ref[i:i + 1, gl]
                qk_scr[g, pl.ds(r, ns, stride=8), :] = conv * _sigmoid(conv)
        qk = jnp.concatenate([qk_scr[g] for g in range(len(groups))], axis=1)
        q = qk[:, :dm]
        k = qk[:, dm:]
        qb = q.astype(BF16)
        kb = k.astype(BF16)
        vv = v_ref[cr, :]
        vb = vv.astype(BF16)

        g = gates_ref[cr, :]
        col = lax.broadcasted_iota(jnp.int32, g.shape, 1)
        lf = jnp.minimum(g, 0.0) - jnp.log1p(jnp.exp(-jnp.abs(g)))
        x_cols = jnp.where(col < H_M, g, lf)
        cum = jnp.dot(tril, x_cols, preferred_element_type=F32, precision=HIGHEST)
        xc = jnp.where(col < H_M, g, cum)
        rows = lax.dot_general(eye, xc, NT_DIMS, preferred_element_type=F32, precision=HIGHEST)

        m_all = m_scr[...]
        e_s, sc_inter, inv_floor, decay, wgt, m_new = [], [], [], [], [], []
        for h in range(H_M):
            bcol = cum[:, H_M + h:H_M + h + 1]
            igcol = g[:, h:h + 1]
            brow = rows[H_M + h:H_M + h + 1, :]
            igrow = rows[h:h + 1, :]
            m_prev = m_all[:, h:h + 1]
            dlog = jnp.where(causal, bcol - brow + igrow, -jnp.inf)
            m_inter = bcol + m_prev
            m_t = jnp.maximum(m_inter, jnp.max(dlog, axis=-1, keepdims=True))
            e_s.append(jnp.exp(dlog - m_t) * k_scale)
            sc_inter.append(jnp.exp(m_inter - m_t))
            inv_floor.append(jnp.exp(-m_t))
            mn = m_t[L - 1:L, :]
            b_last = bcol[L - 1:L, :]
            m_new.append(mn)
            decay.append(jnp.exp(b_last + m_prev - mn))
            wgt.append(jnp.exp(b_last - bcol + igcol - mn) * k_scale)

        qn = []
        for h, sl in enumerate(heads):
            n_old = n_scr[h:h + 1, :]
            qn.append(jnp.sum(q[:, sl] * n_old, axis=-1, keepdims=True))
            n_scr[h:h + 1, :] = decay[h] * n_old + jnp.sum(wgt[h] * k[:, sl], axis=0, keepdims=True)
            m_scr[:, h:h + 1] = m_new[h]

        c_old = [c_scr[h] for h in range(H_M)]
        s_raw = [lax.dot_general(qb[:, sl], kb[:, sl], NT_DIMS, preferred_element_type=F32) for sl in heads]
        qc = [lax.dot_general(qb[:, sl], c_old[h].astype(BF16), NT_DIMS, preferred_element_type=F32)
              for h, sl in enumerate(heads)]
        upd = [lax.dot_general((vv[:, sl] * wgt[h]).astype(BF16), kb[:, sl], TN_DIMS, preferred_element_type=F32)
               for h, sl in enumerate(heads)]
        for h in range(H_M):
            c_scr[h] = decay[h] * c_old[h] + upd[h]
        pending.append((cr, s_raw, e_s, sc_inter, inv_floor, qc, qn, vb))

    for cr, s_raw, e_s, sc_inter, inv_floor, qc, qn, vb in pending:
        s = [s_raw[h] * e_s[h] for h in range(H_M)]
        pv = [jnp.dot(s[h].astype(BF16), vb[:, sl], preferred_element_type=F32) for h, sl in enumerate(heads)]
        oo = o_ref[cr, :]
        for h, sl in enumerate(heads):
            num = sc_inter[h] * qc[h] + pv[h]
            den = sc_inter[h] * qn[h] + jnp.sum(s[h], axis=-1, keepdims=True)
            hout = num / jnp.maximum(jnp.abs(den), inv_floor[h])
            y = _rms(hout, gnorm_ref[:, sl]) * _sigmoid(oo[:, sl])
            hmg_ref[cr, sl] = y.astype(BF16)

    x_end = CONV_PAD + cps * L
    hist_g = [xpad[g, x_end - (CONV_W - 1):x_end, :] for g in range(len(groups))]
    for g in range(len(groups)):
        xpad[g, hist0:CONV_PAD, :] = hist_g[g]
    hist = jnp.concatenate(hist_g, axis=1)

    @pl.when(c == nc - 1)
    def _():
        newconv_ref[...] = hist
        c1_ref[...] = c_scr[...]
        n1_ref[...] = n_scr[...]
        m1_ref[...] = m_scr[...]


def _mlstm(proj, gates, conv_prev, c0, n0, m0, w_conv, b_conv, g_mnorm, *, B, T):
    L = MLSTM_CHUNK if T % MLSTM_CHUNK == 0 else min(CHUNK, T)
    nc = T // L
    dm = c0.shape[1] * c0.shape[2]
    dh = c0.shape[2]
    cps = MLSTM_CPS if nc % MLSTM_CPS == 0 else 1
    ns = nc // cps
    row = lambda b, c: b * ns + c
    kern = functools.partial(_mlstm_kernel, L=L, dm=dm, dh=dh, cps=cps)
    return pl.pallas_call(
        kern,
        grid=(B, ns),
        in_specs=[
            pl.BlockSpec((cps * L, 2 * dm), lambda b, c: (row(b, c), 0)),
            pl.BlockSpec((cps * L, dm), lambda b, c: (row(b, c), 2)),
            pl.BlockSpec((cps * L, dm), lambda b, c: (row(b, c), 3)),
            pl.BlockSpec((cps * L, LANES), lambda b, c: (row(b, c), 0)),
            pl.BlockSpec((None, CONV_W - 1, 2 * dm), lambda b, c: (b, 0, 0)),
            pl.BlockSpec((None, H_M, dh, dh), lambda b, c: (b, 0, 0, 0)),
            pl.BlockSpec((None, H_M, dh), lambda b, c: (b, 0, 0)),
            pl.BlockSpec((None, 1, H_M), lambda b, c: (b, 0, 0)),
            pl.BlockSpec((CONV_W, 2 * dm), lambda b, c: (0, 0)),
            pl.BlockSpec((1, 2 * dm), lambda b, c: (0, 0)),
            pl.BlockSpec((1, dm), lambda b, c: (0, 0)),
        ],
        out_specs=[
            pl.BlockSpec((cps * L, dm), lambda b, c: (row(b, c), 0)),
            pl.BlockSpec((None, CONV_W - 1, 2 * dm), lambda b, c: (b, 0, 0)),
            pl.BlockSpec((None, H_M, dh, dh), lambda b, c: (b, 0, 0, 0)),
            pl.BlockSpec((None, H_M, dh), lambda b, c: (b, 0, 0)),
            pl.BlockSpec((None, 1, H_M), lambda b, c: (b, 0, 0)),
        ],
        out_shape=[
            jax.ShapeDtypeStruct((B * T, dm), BF16),
            jax.ShapeDtypeStruct((B, CONV_W - 1, 2 * dm), F32),
            jax.ShapeDtypeStruct((B, H_M, dh, dh), F32),
            jax.ShapeDtypeStruct((B, H_M, dh), F32),
            jax.ShapeDtypeStruct((B, 1, H_M), F32),
        ],
        scratch_shapes=[
            pltpu.VMEM((2 * dm // LANES, CONV_PAD + cps * L, LANES), F32),
            pltpu.VMEM((2 * dm // LANES, L, LANES), F32),
            pltpu.VMEM((H_M, dh, dh), F32),
            pltpu.VMEM((H_M, dh), F32),
            pltpu.VMEM((1, H_M), F32),
        ],
        compiler_params=_params(("parallel", "arbitrary")),
        name="mlstm",
    )(proj, proj, proj, gates, conv_prev, c0, n0, m0, w_conv, b_conv, g_mnorm)


def _hgrn_factors(qq, kk, a, qa, sub, nb):
    L = a.shape[0]
    refs = [jnp.zeros_like(a[0:1, :])] + [a[i * sub - 1:i * sub, :] for i in range(1, nb)]
    qs = []
    ks = []
    for i in range(nb):
        rows = slice(i * sub, (i + 1) * sub)
        nk = (i + 1) * sub
        q_i = qa[rows, :] if i == 0 else (qq[rows, :] * jnp.exp(a[rows, :] - refs[i])).astype(BF16)
        k_i = (kk[:nk, :] * jnp.exp(jnp.minimum(refs[i] - a[:nk, :], SAFE_SPAN))).astype(BF16)
        zq = lambda n: jnp.zeros((n, q_i.shape[1]), BF16)
        qs.append(jnp.concatenate([z for z in (zq(i * sub), q_i, zq(L - nk)) if z.shape[0]], axis=0))
        ks.append(jnp.concatenate([z for z in (k_i, zq(L - nk)) if z.shape[0]], axis=0))
    return refs, qs, ks


def _hgrn_scores(qs, ks, sl, L):
    lhs = jnp.concatenate([q[:, sl] for q in qs], axis=1) if len(qs) > 1 else qs[0][:, sl]
    rhs = jnp.concatenate([k[:, sl] for k in ks], axis=1) if len(ks) > 1 else ks[0][:, sl]
    return lax.dot_general(lhs, rhs, NT_DIMS, preferred_element_type=F32)


def _hgrn_kernel(f_ref, i_ref, q_ref, g_ref, lbl_ref, s0_ref, gnorm_ref,
                 ohg_ref, s1_ref,
                 st_scr, a_scr, k_scr, sc_scr, o_scr, in_scr, *, L, nh, cps):
    c = pl.program_id(1)
    nc = pl.num_programs(1)
    sub = min(SUB, L)
    nb = L // sub

    @pl.when(c == 0)
    def _():
        for h in range(nh):
            st_scr[h] = s0_ref[h].T

    lbl = lbl_ref[...]
    mx = jnp.max(lbl, axis=0, keepdims=True)
    ex = jnp.exp(lbl - mx)
    lb = ex[0:1, :] / jnp.sum(ex, axis=0, keepdims=True)
    gnorm = gnorm_ref[...]
    rr = lax.broadcasted_iota(jnp.int32, (L, L), 0)
    cc = lax.broadcasted_iota(jnp.int32, (L, L), 1)
    causal = cc <= rr
    tril = causal.astype(BF16)
    tril3 = jnp.concatenate([tril, tril, tril], axis=1)

    def finish(rows):
        gg = g_ref[rows, :]
        ohg_ref[rows, :] = (_rms(o_scr[rows, :], gnorm) * (gg * _sigmoid(gg))).astype(BF16)

    span = None
    for j in range(cps):
        cr = slice(j * L, (j + 1) * L)
        f = lb + (1.0 - lb) * _sigmoid(f_ref[cr, :])
        lf = jnp.log(f)
        kk = 1.0 - f
        qq = q_ref[cr, :]
        vb = i_ref[cr, :].astype(BF16)

        hi = lf.astype(BF16)
        r1 = lf - hi.astype(F32)
        mid = r1.astype(BF16)
        lo = (r1 - mid.astype(F32)).astype(BF16)
        a = jnp.dot(tril3, jnp.concatenate([hi, mid, lo], axis=0), preferred_element_type=F32)
        a_scr[cr, :] = a
        k_scr[cr, :] = kk

        a_last = a[L - 1:L, :]
        qa = (qq * jnp.exp(a)).astype(BF16)
        kdec = (kk * jnp.exp(a_last - a)).astype(BF16)
        dec_last = jnp.exp(a_last)
        refs, qs, ks = _hgrn_factors(qq, kk, a, qa, sub, nb)
        for i in range(nb):
            sp = refs[i] - a[(i + 1) * sub - 1:(i + 1) * sub, :]
            span = sp if span is None else jnp.maximum(span, sp)

        heads = [slice(h * DH_H, (h + 1) * DH_H) for h in range(nh)]
        st_old = [st_scr[h] for h in range(nh)]
        inter = [lax.dot_general(qa[:, sl], st_old[h].astype(BF16), NT_DIMS, preferred_element_type=F32)
                 for h, sl in enumerate(heads)]
        raw = [_hgrn_scores(qs, ks, sl, L) for sl in heads]
        upd = [lax.dot_general(vb[:, sl], kdec[:, sl], TN_DIMS, preferred_element_type=F32) for sl in heads]
        for h, sl in enumerate(heads):
            st_scr[h] = st_old[h] * dec_last[:, sl] + upd[h]
            in_scr[cr, sl] = inter[h]
        blk = [jnp.where(causal, r, 0.0).astype(BF16) for r in raw]
        for h, sl in enumerate(heads):
            o_scr[cr, sl] = inter[h] + jnp.dot(blk[h], vb[:, sl], preferred_element_type=F32)
        finish(cr)

    use_exact = jnp.logical_not(jnp.max(span) <= SAFE_SPAN)

    @pl.when(use_exact)
    def _():
        sc_scr[...] = jnp.zeros_like(sc_scr)
        lane = lax.broadcasted_iota(jnp.int32, (sub, LANES), 1)
        in_block = jnp.logical_and(causal, (cc // sub) == (rr // sub))

        def fix_chunk(j, carry):
            base = pl.multiple_of(j * L, L)

            def diag_body(s, carry2):
                r0 = pl.multiple_of((s // sub) * sub, sub)
                a_blk = a_scr[pl.ds(base + r0, sub), :]
                q_blk = q_ref[pl.ds(base + r0, sub), :]
                a_s = a_scr[pl.ds(base + s, 1), :]
                k_s = k_scr[pl.ds(base + s, 1), :]
                p = q_blk * jnp.exp(jnp.minimum(a_blk - a_s, 0.0)) * k_s
                for h in range(nh):
                    colv = jnp.sum(p[:, h * DH_H:(h + 1) * DH_H], axis=-1, keepdims=True)
                    old = sc_scr[h, pl.ds(r0, sub), :]
                    sc_scr[h, pl.ds(r0, sub), :] = jnp.where(lane == s, colv, old)
                return carry2

            lax.fori_loop(0, L, diag_body, 0)

            cr = pl.ds(base, L)
            a = a_scr[cr, :]
            kk = k_scr[cr, :]
            qq = q_ref[cr, :]
            vb = i_ref[cr, :].astype(BF16)
            qa = (qq * jnp.exp(a)).astype(BF16)
            _, qs, ks = _hgrn_factors(qq, kk, a, qa, sub, nb)
            for h in range(nh):
                sl = slice(h * DH_H, (h + 1) * DH_H)
                blk = jnp.where(causal, _hgrn_scores(qs, ks, sl, L), 0.0)
                blk = jnp.where(in_block, sc_scr[h][:, :L], blk).astype(BF16)
                o_scr[cr, sl] = in_scr[cr, sl] + jnp.dot(blk, vb[:, sl], preferred_element_type=F32)
            finish(cr)
            return carry

        lax.fori_loop(0, cps, fix_chunk, 0)

    @pl.when(c == nc - 1)
    def _():
        for h in range(nh):
            s1_ref[h] = st_scr[h].T


def _hgrn(proj, lb_logits, s0, g_hnorm, *, B, T):
    L = min(CHUNK, T)
    nc = T // L
    cps = HGRN_CPS if nc % HGRN_CPS == 0 else 1
    ns = nc // cps
    nh = s0.shape[1]
    dhh = nh * DH_H
    row = lambda b, c: b * ns + c
    kern = functools.partial(_hgrn_kernel, L=L, nh=nh, cps=cps)
    return pl.pallas_call(
        kern,
        grid=(B, ns),
        in_specs=[
            pl.BlockSpec((cps * L, dhh), lambda b, c: (row(b, c), 0)),
            pl.BlockSpec((cps * L, dhh), lambda b, c: (row(b, c), 1)),
            pl.BlockSpec((cps * L, dhh), lambda b, c: (row(b, c), 2)),
            pl.BlockSpec((cps * L, dhh), lambda b, c: (row(b, c), 3)),
            pl.BlockSpec((2, dhh), lambda b, c: (0, 0)),
            pl.BlockSpec((None, nh, DH_H, DH_H), lambda b, c: (b, 0, 0, 0)),
            pl.BlockSpec((1, dhh), lambda b, c: (0, 0)),
        ],
        out_specs=[
            pl.BlockSpec((cps * L, dhh), lambda b, c: (row(b, c), 0)),
            pl.BlockSpec((None, nh, DH_H, DH_H), lambda b, c: (b, 0, 0, 0)),
        ],
        out_shape=[
            jax.ShapeDtypeStruct((B * T, dhh), BF16),
            jax.ShapeDtypeStruct((B, nh, DH_H, DH_H), F32),
        ],
        scratch_shapes=[
            pltpu.VMEM((nh, DH_H, DH_H), F32),
            pltpu.VMEM((cps * L, dhh), F32),
            pltpu.VMEM((cps * L, dhh), F32),
            pltpu.VMEM((nh, L, LANES), F32),
            pltpu.VMEM((cps * L, dhh), F32),
            pltpu.VMEM((cps * L, dhh), F32),
        ],
        compiler_params=_params(("parallel", "arbitrary")),
        name="hgrn",
    )(proj, proj, proj, proj, lb_logits, s0, g_hnorm)


def _merge_kernel(hmg_ref, ohg_ref, ga_ref, gb_ref, x_ref, wa_ref, wb_ref, wo_ref, gffn_ref, x1_ref, h2_ref):
    ya = jnp.dot(hmg_ref[...], wa_ref[...], preferred_element_type=F32)
    yb = jnp.dot(ohg_ref[...], wb_ref[...], preferred_element_type=F32)
    u = _sigmoid(ga_ref[...]) * ya + _sigmoid(gb_ref[...]) * yb
    x1 = x_ref[...] + jnp.dot(u.astype(BF16), wo_ref[...], preferred_element_type=F32)
    x1_ref[...] = x1
    h2_ref[...] = _rms(x1, gffn_ref[...]).astype(BF16)


def _merge(hmg, ohg, proj, x2d, w_a, w_b, w_o, g_ffn):
    m, d = x2d.shape
    dm = hmg.shape[1]
    tm = min(m, MERGE_ROWS)
    const = lambda i: (0, 0)
    return pl.pallas_call(
        _merge_kernel,
        grid=(m // tm,),
        in_specs=[
            pl.BlockSpec((tm, dm), lambda i: (i, 0)),
            pl.BlockSpec((tm, dm), lambda i: (i, 0)),
            pl.BlockSpec((tm, d), lambda i: (i, 2)),
            pl.BlockSpec((tm, d), lambda i: (i, 3)),
            pl.BlockSpec((tm, d), lambda i: (i, 0)),
            pl.BlockSpec((dm, d), const, pipeline_mode=pl.Buffered(1)),
            pl.BlockSpec((dm, d), const, pipeline_mode=pl.Buffered(1)),
            pl.BlockSpec((d, d), const, pipeline_mode=pl.Buffered(1)),
            pl.BlockSpec((1, d), const),
        ],
        out_specs=[pl.BlockSpec((tm, d), lambda i: (i, 0)), pl.BlockSpec((tm, d), lambda i: (i, 0))],
        out_shape=[jax.ShapeDtypeStruct((m, d), F32), jax.ShapeDtypeStruct((m, d), BF16)],
        compiler_params=_params(("parallel",)),
        name="merge",
    )(hmg, ohg, proj, proj, x2d, w_a, w_b, w_o, g_ffn)


def _ffn_kernel(x1_ref, h2_ref, w1_ref, w2_ref, gfin_ref, y_ref, acc_scr):
    j = pl.program_id(1)

    h = h2_ref[...]
    tf = w1_ref.shape[1]
    pieces = [slice(p, p + FFN_PIECE) for p in range(0, tf, FFN_PIECE)]
    ts = []
    for ps in pieces:
        t = jnp.maximum(jnp.dot(h, w1_ref[:, ps], preferred_element_type=F32), 0.0)
        ts.append((t * t).astype(BF16))
    acc = jnp.where(j > 0, acc_scr[...], 0.0)
    for ps, t in zip(pieces, ts):
        acc = acc + jnp.dot(t, w2_ref[ps, :], preferred_element_type=F32)
    acc_scr[...] = acc

    @pl.when(j == pl.num_programs(1) - 1)
    def _():
        y_ref[...] = _rms(x1_ref[...] + acc_scr[...], gfin_ref[...])


def _ffn(x1, h2, w1, w2, g_final):
    m, d = x1.shape
    dff = w1.shape[1]
    tm = min(m, FFN_ROWS)
    tf = FFN_HIDDEN
    return pl.pallas_call(
        _ffn_kernel,
        grid=(m // tm, dff // tf),
        in_specs=[
            pl.BlockSpec((tm, d), lambda i, j: (i, 0)),
            pl.BlockSpec((tm, d), lambda i, j: (i, 0)),
            pl.BlockSpec((d, tf), lambda i, j: (0, j)),
            pl.BlockSpec((tf, d), lambda i, j: (j, 0)),
            pl.BlockSpec((1, d), lambda i, j: (0, 0)),
        ],
        out_specs=pl.BlockSpec((tm, d), lambda i, j: (i, 0)),
        out_shape=jax.ShapeDtypeStruct((m, d), F32),
        scratch_shapes=[pltpu.VMEM((tm, d), F32)],
        compiler_params=_params(("parallel", "arbitrary")),
        name="ffn",
    )(x1, h2, w1, w2, g_final)


def _group(x, conv_prev, c0, n0, m0, s0, wts):
    B, T, d = x.shape
    x2d = x.reshape(B * T, d)
    proj_lo, proj_hi, gates = _inproj(x2d, wts["g_mix"], wts["w_all"], wts["n_lo"], wts["w_hi"], wts["w_if"], wts["b_if"])
    hmg, new_conv, c1, n1, m1 = _mlstm(proj_lo, gates, conv_prev, c0, n0, m0.reshape(B, 1, -1),
                                       wts["w_conv"], wts["b_conv"], wts["g_mnorm"], B=B, T=T)
    ohg, s1 = _hgrn(proj_hi, wts["lb_logits"], s0, wts["g_hnorm"], B=B, T=T)
    x1, h2 = _merge(hmg, ohg, proj_hi, x2d, wts["w_a"], wts["w_b"], wts["w_o"], wts["g_ffn"])
    y = _ffn(x1, h2, wts["w_ff1"], wts["w_ff2"], wts["g_final"])
    return (y.reshape(B, T, d), new_conv[None], c1[None], n1[None], m1.reshape(1, B, -1), s1[None])


def kernel(x_prompt, x_sample, cache_mlstm_conv, state_mlstm_C, state_mlstm_n, state_mlstm_m, state_hgrn_S,
           g_mix, w_in, b_if, w_conv, b_conv, g_mnorm, g_hnorm, hgrn_lb_logits, w_branch_a, w_branch_b,
           w_out, g_ffn, w_ff1, w_ff2, g_final):
    depth = w_in.shape[0]
    assert depth == 1, "single-layer trunk only"
    d = x_prompt.shape[-1]
    dm = w_branch_a.shape[1]
    dhh = w_branch_b.shape[1]
    nh_h = dhh // DH_H
    dh_m = dm // H_M
    assert hgrn_lb_logits.shape[0] == depth + 1
    assert 2 * dm == d and dhh == dm

    w = w_in[0]
    n_if = 2 * H_M
    p_if = 4 * dm
    w_all = w.astype(BF16)
    w_hi = w_all[:, p_if + n_if:]
    w_if = jnp.pad(w_all[:, p_if:p_if + n_if], ((0, 0), (0, LANES - n_if)))
    bif = jnp.pad(b_if[0].astype(F32), (0, LANES - n_if)).reshape(1, LANES)
    wts = dict(
        g_mix=g_mix[0].reshape(1, d), w_all=w_all, n_lo=p_if, w_hi=w_hi, w_if=w_if, b_if=bif,
        w_conv=w_conv[0], b_conv=b_conv[0].reshape(1, -1), g_mnorm=g_mnorm[0].reshape(1, dm),
        lb_logits=hgrn_lb_logits.astype(F32), g_hnorm=g_hnorm[0].reshape(1, dhh),
        w_a=w_branch_a[0].astype(BF16), w_b=w_branch_b[0].astype(BF16), w_o=w_out[0].astype(BF16),
        g_ffn=g_ffn[0].reshape(1, d), w_ff1=w_ff1[0].astype(BF16), w_ff2=w_ff2[0].astype(BF16),
        g_final=g_final.reshape(1, d),
    )

    bp = x_prompt.shape[0]
    zeros = lambda *s: jnp.zeros(s, F32)
    outs_p = _group(x_prompt, zeros(bp, CONV_W - 1, 2 * dm), zeros(bp, H_M, dh_m, dh_m), zeros(bp, H_M, dh_m),
                    zeros(bp, H_M), zeros(bp, nh_h, DH_H, DH_H), wts)
    outs_s = _group(x_sample, cache_mlstm_conv[0], state_mlstm_C[0].astype(F32), state_mlstm_n[0].astype(F32),
                    state_mlstm_m[0].astype(F32), state_hgrn_S[0].astype(F32), wts)
    return (outs_p[0], outs_s[0]) + outs_p[1:] + outs_s[1:]
```
